```python
import jax, jax.numpy as jnp
from jax import lax
import numpy as np

D_MODEL = 1024
BATCH = 4
SEQ = 8192
DEPTH = 2

D_MIX = D_MODEL
CONV_DIM = D_MIX // 4
CONV_K = 3
MLA_HEADS = 8
MLA_NOPE = 64
MLA_ROPE = 32
MLA_V = 64
MLA_Q_RANK = 384
MLA_KV_RANK = 256
LRU_DIM = D_MIX // 4
LRU_BLOCKS = 4
LRU_BW = LRU_DIM // LRU_BLOCKS
LRU_CONV_K = 4
LRU_C = 8.0
ROPE_THETA = 10000.0
Q_BLOCK = 128
D_FF = 3584
N_EXPERTS = 8
TOP_K = 2
N_DENSE = (DEPTH + 1) // 2
N_MOE = DEPTH // 2
DN_ALPHA = (2.0 * DEPTH) ** 0.25
DN_BETA = (8.0 * DEPTH) ** -0.25
LN_EPS = 1e-5
RMS_EPS = 1e-6

IN_SPLITS = (CONV_DIM, CONV_DIM, CONV_DIM, MLA_Q_RANK, MLA_KV_RANK, MLA_ROPE, LRU_DIM, LRU_DIM)
D_IN = sum(IN_SPLITS)
MIX_SPLITS = (CONV_DIM, MLA_HEADS * MLA_V, LRU_DIM)

kernel_name = "hymba_style_conv_mla_rglru_moe_encoder"


def _offsets(sizes):
    return np.cumsum(sizes)[:-1].tolist()


def layer_norm(x, g, b):
    xf = x.astype(jnp.float32)
    mu = jnp.mean(xf, axis=-1, keepdims=True)
    xc = xf - mu
    var = jnp.mean(xc * xc, axis=-1, keepdims=True)
    return (xc * lax.rsqrt(var + LN_EPS) * g + b).astype(x.dtype)


def rms_norm(x, g):
    xf = x.astype(jnp.float32)
    ms = jnp.mean(xf * xf, axis=-1, keepdims=True)
    return (xf * lax.rsqrt(ms + RMS_EPS) * g).astype(x.dtype)


def depthwise_conv(x, w, pad):
    c = x.shape[-1]
    return lax.conv_general_dilated(
        x, w[:, None, :].astype(x.dtype), window_strides=(1,), padding=[pad],
        dimension_numbers=("NWC", "WIO", "NWC"), feature_group_count=c)


def rope_tables(seq):
    pos = jnp.arange(seq, dtype=jnp.float32)
    inv = ROPE_THETA ** (-jnp.arange(0, MLA_ROPE, 2, dtype=jnp.float32) / MLA_ROPE)
    ang = pos[:, None] * inv[None, :]
    return jnp.cos(ang), jnp.sin(ang)


def apply_rope(x, cos, sin):
    xf = x.astype(jnp.float32)
    x1, x2 = jnp.split(xf, 2, axis=-1)
    return jnp.concatenate([x1 * cos - x2 * sin, x2 * cos + x1 * sin], axis=-1).astype(x.dtype)


def short_conv_mixer(b_gate, c_gate, h, w_conv):
    return b_gate * depthwise_conv(c_gate * h, w_conv, (1, 1))


def mla_mixer(c_q, c_kv, k_r, q_norm_g, w_uq, kv_norm_g, w_ukv, cos, sin):
    b, s, _ = c_q.shape
    q = jnp.einsum('bsr,re->bse', rms_norm(c_q, q_norm_g), w_uq).reshape(b, s, MLA_HEADS, MLA_NOPE + MLA_ROPE)
    kv = jnp.einsum('bsr,re->bse', rms_norm(c_kv, kv_norm_g), w_ukv).reshape(b, s, MLA_HEADS, MLA_NOPE + MLA_V)
    scale = (MLA_NOPE + MLA_ROPE) ** -0.5
    q_nope = q[..., :MLA_NOPE] * scale
    q_rope = apply_rope(q[..., MLA_NOPE:], cos[:, None, :], sin[:, None, :]) * scale
    k_nope, v = kv[..., :MLA_NOPE], kv[..., MLA_NOPE:]
    k_rope = apply_rope(k_r, cos, sin)
    nb = s // Q_BLOCK
    qn = q_nope.reshape(b, nb, Q_BLOCK, MLA_HEADS, MLA_NOPE).transpose(1, 0, 2, 3, 4)
    qr = q_rope.reshape(b, nb, Q_BLOCK, MLA_HEADS, MLA_ROPE).transpose(1, 0, 2, 3, 4)

    def attend(blk):
        qn_b, qr_b = blk
        sc = (jnp.einsum('bqhd,bkhd->bhqk', qn_b, k_nope, preferred_element_type=jnp.float32)
              + jnp.einsum('bqhr,bkr->bhqk', qr_b, k_rope, preferred_element_type=jnp.float32))
        p = jax.nn.softmax(sc, axis=-1).astype(v.dtype)
        return jnp.einsum('bhqk,bkhd->bqhd', p, v)

    o = lax.map(attend, (qn, qr))
    return o.transpose(1, 0, 2, 3, 4).reshape(b, s, MLA_HEADS * MLA_V)


def rglru_scan(x, w_a, b_a, w_i, b_i, lam, reverse):
    b, s, _ = x.shape
    xb = x.reshape(b, s, LRU_BLOCKS, LRU_BW)
    ga = jnp.einsum('bsnc,ncd->bsnd', xb, w_a).reshape(b, s, LRU_DIM) + b_a
    gi = jnp.einsum('bsnc,ncd->bsnd', xb, w_i).reshape(b, s, LRU_DIM) + b_i
    rec = jax.nn.sigmoid(ga.astype(jnp.float32))
    inp = jax.nn.sigmoid(gi.astype(jnp.float32))
    log_a = -LRU_C * rec * jax.nn.softplus(-lam.astype(jnp.float32))
    a = jnp.exp(log_a)
    u = jnp.sqrt(-jnp.expm1(2.0 * log_a)) * (inp * x.astype(jnp.float32))

    def combine(left, right):
        a1, b1 = left
        a2, b2 = right
        return a1 * a2, a2 * b1 + b2

    _, h = lax.associative_scan(combine, (a, u), axis=1, reverse=reverse)
    return h


def griffin_recurrent(g_in, x_in, conv_w, conv_b, w_a, b_a, w_i, b_i, lam):
    xc = depthwise_conv(x_in, conv_w, (1, 2)) + conv_b
    h = (rglru_scan(xc, w_a[0], b_a[0], w_i[0], b_i[0], lam[0], False)
         + rglru_scan(xc, w_a[1], b_a[1], w_i[1], b_i[1], lam[1], True))
    return (jax.nn.gelu(g_in.astype(jnp.float32), approximate=True) * h).astype(g_in.dtype)


def hybrid_mixer(x, w_in, conv_w, q_norm_g, w_uq, kv_norm_g, w_ukv, lru_conv_w, lru_conv_b,
                 lru_wa, lru_ba, lru_wi, lru_bi, lru_lam, mix_norm_g, w_out, cos, sin):
    z = jnp.einsum('bsd,de->bse', x, w_in)
    cb, cc, ch, c_q, c_kv, k_r, lg, lx = jnp.split(z, _offsets(IN_SPLITS), axis=-1)
    y_conv = short_conv_mixer(cb, cc, ch, conv_w)
    y_mla = mla_mixer(c_q, c_kv, k_r, q_norm_g, w_uq, kv_norm_g, w_ukv, cos, sin)
    y_lru = griffin_recurrent(lg, lx, lru_conv_w, lru_conv_b, lru_wa, lru_ba, lru_wi, lru_bi, lru_lam)
    g_conv, g_mla, g_lru = jnp.split(mix_norm_g, _offsets(MIX_SPLITS))
    y = jnp.concatenate([rms_norm(y_conv, g_conv), rms_norm(y_mla, g_mla), rms_norm(y_lru, g_lru)], axis=-1)
    return jnp.einsum('bse,ed->bsd', y, w_out)


def swiglu(x, w_gate, w_up, w_down):
    h = jax.nn.silu(x @ w_gate) * (x @ w_up)
    return h @ w_down


def moe_swiglu(x, w_router, w_gate, w_up, w_down):
    b, s, d = x.shape
    xt = x.reshape(b * s, d)
    logits = (xt @ w_router).astype(jnp.float32)
    top_v, top_i = lax.top_k(logits, TOP_K)
    gates = jax.nn.softmax(top_v, axis=-1)
    combine = jnp.einsum('nk,nke->ne', gates, jax.nn.one_hot(top_i, N_EXPERTS, dtype=jnp.float32))
    out = jnp.zeros((b * s, d), jnp.float32)
    for e in range(N_EXPERTS):
        out = out + combine[:, e:e + 1] * swiglu(xt, w_gate[e], w_up[e], w_down[e]).astype(jnp.float32)
    return out.astype(x.dtype).reshape(b, s, d)


def setup_inputs(seed: int = 0) -> dict:
    key = jax.random.key(seed)
    ks = iter(jax.random.split(key, 32))

    def nrm(shape, scale):
        return scale * jax.random.normal(next(ks), shape, jnp.float32)

    def gain(shape):
        return 1.0 + nrm(shape, 0.02)

    x = nrm((BATCH, SEQ, D_MODEL), 1.0)
    ln_in_g = gain((D_MODEL,))
    ln_in_b = nrm((D_MODEL,), 0.02)
    w_in = nrm((DEPTH, D_MODEL, D_IN), D_MODEL ** -0.5)
    conv_w = nrm((DEPTH, CONV_K, CONV_DIM), CONV_K ** -0.5)
    q_norm_g = gain((DEPTH, MLA_Q_RANK))
    w_uq = nrm((DEPTH, MLA_Q_RANK, MLA_HEADS * (MLA_NOPE + MLA_ROPE)), MLA_Q_RANK ** -0.5)
    kv_norm_g = gain((DEPTH, MLA_KV_RANK))
    w_ukv = nrm((DEPTH, MLA_KV_RANK, MLA_HEADS * (MLA_NOPE + MLA_V)), MLA_KV_RANK ** -0.5)
    lru_conv_w = nrm((DEPTH, LRU_CONV_K, LRU_DIM), LRU_CONV_K ** -0.5)
    lru_conv_b = nrm((DEPTH, LRU_DIM), 0.02)
    lru_wa = nrm((DEPTH, 2, LRU_BLOCKS, LRU_BW, LRU_BW), LRU_BW ** -0.5)
    lru_ba = nrm((DEPTH, 2, LRU_DIM), 0.02)
    lru_wi = nrm((DEPTH, 2, LRU_BLOCKS, LRU_BW, LRU_BW), LRU_BW ** -0.5)
    lru_bi = nrm((DEPTH, 2, LRU_DIM), 0.02)
    u = jax.random.uniform(next(ks), (DEPTH, 2, LRU_DIM), jnp.float32, minval=0.9, maxval=0.999)
    a0 = u ** (1.0 / LRU_C)
    lru_lam = jnp.log(a0) - jnp.log1p(-a0)
    mix_norm_g = gain((DEPTH, D_MIX))
    w_out = nrm((DEPTH, D_MIX, D_MODEL), DN_BETA * D_MIX ** -0.5)
    ln1_g = gain((DEPTH, D_MODEL))
    ln1_b = nrm((DEPTH, D_MODEL), 0.02)
    dense_w_gate = nrm((N_DENSE, D_MODEL, D_FF), D_MODEL ** -0.5)
    dense_w_up = nrm((N_DENSE, D_MODEL, D_FF), D_MODEL ** -0.5)
    dense_w_down = nrm((N_DENSE, D_FF, D_MODEL), DN_BETA * D_FF ** -0.5)
    moe_w_router = nrm((N_MOE, D_MODEL, N_EXPERTS), D_MODEL ** -0.5)
    moe_w_gate = nrm((N_MOE, N_EXPERTS, D_MODEL, D_FF), D_MODEL ** -0.5)
    moe_w_up = nrm((N_MOE, N_EXPERTS, D_MODEL, D_FF), D_MODEL ** -0.5)
    moe_w_down = nrm((N_MOE, N_EXPERTS, D_FF, D_MODEL), DN_BETA * D_FF ** -0.5)
    ln2_g = gain((DEPTH, D_MODEL))
    ln2_b = nrm((DEPTH, D_MODEL), 0.02)
    return {"x": x, "ln_in_g": ln_in_g, "ln_in_b": ln_in_b, "w_in": w_in, "conv_w": conv_w,
            "q_norm_g": q_norm_g, "w_uq": w_uq, "kv_norm_g": kv_norm_g, "w_ukv": w_ukv,
            "lru_conv_w": lru_conv_w, "lru_conv_b": lru_conv_b, "lru_wa": lru_wa, "lru_ba": lru_ba,
            "lru_wi": lru_wi, "lru_bi": lru_bi, "lru_lam": lru_lam, "mix_norm_g": mix_norm_g,
            "w_out": w_out, "ln1_g": ln1_g, "ln1_b": ln1_b, "dense_w_gate": dense_w_gate,
            "dense_w_up": dense_w_up, "dense_w_down": dense_w_down, "moe_w_router": moe_w_router,
            "moe_w_gate": moe_w_gate, "moe_w_up": moe_w_up, "moe_w_down": moe_w_down,
            "ln2_g": ln2_g, "ln2_b": ln2_b}


def reference(x, ln_in_g, ln_in_b, w_in, conv_w, q_norm_g, w_uq, kv_norm_g, w_ukv,
              lru_conv_w, lru_conv_b, lru_wa, lru_ba, lru_wi, lru_bi, lru_lam, mix_norm_g,
              w_out, ln1_g, ln1_b, dense_w_gate, dense_w_up, dense_w_down, moe_w_router,
              moe_w_gate, moe_w_up, moe_w_down, ln2_g, ln2_b):
    cos, sin = rope_tables(x.shape[1])
    x = layer_norm(x, ln_in_g, ln_in_b)
    for l in range(DEPTH):
        mix = hybrid_mixer(x, w_in[l], conv_w[l], q_norm_g[l], w_uq[l], kv_norm_g[l], w_ukv[l],
                           lru_conv_w[l], lru_conv_b[l], lru_wa[l], lru_ba[l], lru_wi[l], lru_bi[l],
                           lru_lam[l], mix_norm_g[l], w_out[l], cos, sin)
        x = layer_norm(DN_ALPHA * x + mix, ln1_g[l], ln1_b[l])
        if l % 2 == 0:
            j = l // 2
            f = swiglu(x, dense_w_gate[j], dense_w_up[j], dense_w_down[j])
        else:
            j = l // 2
            f = moe_swiglu(x, moe_w_router[j], moe_w_gate[j], moe_w_up[j], moe_w_down[j])
        x = layer_norm(DN_ALPHA * x + f, ln2_g[l], ln2_b[l])
    return x
```

```python
import functools
import math

import jax
import jax.numpy as jnp
from jax import lax
from jax.experimental import pallas as pl
from jax.experimental.pallas import tpu as pltpu

D_MODEL = 1024
DEPTH = 2
CONV_DIM = 256
MLA_HEADS = 8
MLA_NOPE = 64
MLA_ROPE = 32
MLA_V = 64
MLA_Q_RANK = 384
MLA_KV_RANK = 256
LRU_DIM = 256
LRU_C = 8.0
ROPE_THETA = 10000.0
D_FF = 3584
N_EXPERTS = 8
DN_ALPHA = (2.0 * DEPTH) ** 0.25
LN_EPS = 1e-5
RMS_EPS = 1e-6

D_IN_PAD = 2048
HALF_ROPE = MLA_ROPE // 2
HEAD_PAD = 128
COL_CB, COL_CC, COL_CH, COL_CKV, COL_LG, COL_LX = 0, 1, 2, 3, 4, 5
COL_CQ = 4
COL_KR = 15
KR_X1 = 64

LANES = 128
SUBLANES = 8
VMEM_LIMIT = 56 * 1024 * 1024
MXU_DTYPE = jnp.bfloat16

TM = 512
TS = 512
TQ = 512
TM_FFN = 1024
TF = 512
TM_CMB = 256
NEG_BIG = -1e30
LOG2E = 1.4426950408889634


def _cparams(sem, vmem=VMEM_LIMIT):
    return pltpu.CompilerParams(dimension_semantics=sem, vmem_limit_bytes=vmem)


def _layer_norm(x, g, b):
    mu = jnp.mean(x, axis=-1, keepdims=True)
    xc = x - mu
    var = jnp.mean(xc * xc, axis=-1, keepdims=True)
    return xc * lax.rsqrt(var + LN_EPS) * g + b


def _rms_norm(x, g):
    ms = jnp.mean(x * x, axis=-1, keepdims=True)
    return x * lax.rsqrt(ms + RMS_EPS) * g


def _dot(a, b):
    return jnp.dot(a.astype(MXU_DTYPE), b.astype(MXU_DTYPE), preferred_element_type=jnp.float32)


def _dot_nt(a, b):
    return lax.dot_general(a.astype(MXU_DTYPE), b.astype(MXU_DTYPE), (((1,), (1,)), ((), ())),
                           preferred_element_type=jnp.float32)


def _in_proj_ln_kernel(x_ref, g_ref, b_ref, w_ref, xn_ref, z_ref):
    xn = _layer_norm(x_ref[...], g_ref[...], b_ref[...])
    xn_ref[...] = xn
    z_ref[...] = _dot(xn, w_ref[...])


def _in_proj_kernel(x_ref, w_ref, z_ref):
    z_ref[...] = _dot(x_ref[...], w_ref[...])


def in_proj(x, w_pad, ln=None):
    n = x.shape[0]
    tm = min(TM, n)
    grid = (n // tm,)
    row = lambda i: (i, 0)
    fixed = lambda i: (0, 0)
    x_spec = pl.BlockSpec((tm, D_MODEL), row)
    w_spec = pl.BlockSpec((D_MODEL, D_IN_PAD), fixed)
    z_spec = pl.BlockSpec((tm, D_IN_PAD), row)
    z_shape = jax.ShapeDtypeStruct((n, D_IN_PAD), jnp.float32)
    if ln is None:
        z = pl.pallas_call(
            _in_proj_kernel, grid=grid, in_specs=[x_spec, w_spec], out_specs=z_spec, out_shape=z_shape,
            compiler_params=_cparams(("parallel",)), name="in_proj")(x, w_pad)
        return x, z
    g, b = ln
    vec = pl.BlockSpec((1, D_MODEL), fixed)
    xn, z = pl.pallas_call(
        _in_proj_ln_kernel, grid=grid, in_specs=[x_spec, vec, vec, w_spec],
        out_specs=[x_spec, z_spec],
        out_shape=[jax.ShapeDtypeStruct((n, D_MODEL), jnp.float32), z_shape],
        compiler_params=_cparams(("parallel",)), name="in_proj_ln")(x, g, b, w_pad)
    return xn, z


def _mla_proj_kernel(ckv_ref, cq_ref, kr_ref, gq_ref, gkv_ref, wq_ref, wk_ref, wv_ref,
                     cos_ref, sin_ref, kc_ref, ksa_ref, ksb_ref, q_out, k_out, v_out):
    cqn = _rms_norm(cq_ref[0], gq_ref[...])
    ckvn = _rms_norm(ckv_ref[0], gkv_ref[...])
    kr = kr_ref[0]

    krope = (kr * kc_ref[...]
             + pltpu.roll(kr, LANES - HALF_ROPE, axis=1) * ksa_ref[...]
             + pltpu.roll(kr, HALF_ROPE, axis=1) * ksb_ref[...])
    k_all = _dot(ckvn, wk_ref[...])
    for h in range(MLA_HEADS):
        k_out[0, h] = (k_all[:, h * HEAD_PAD:(h + 1) * HEAD_PAD] + krope).astype(k_out.dtype)

    v_all = _dot_nt(wv_ref[...], ckvn)
    for h in range(MLA_HEADS):
        v_out[0, h, 0] = v_all[h * MLA_V:(h + 1) * MLA_V].astype(v_out.dtype)

    q_all = _dot_nt(wq_ref[...], cqn)
    cos_t = cos_ref[...]
    sin_t = sin_ref[...]
    qscale = (MLA_NOPE + MLA_ROPE) ** -0.5 * LOG2E
    for h in range(MLA_HEADS):
        base = h * HEAD_PAD
        nope = q_all[base:base + MLA_NOPE]
        x1 = q_all[base + MLA_NOPE:base + MLA_NOPE + HALF_ROPE]
        x2 = q_all[base + MLA_NOPE + HALF_ROPE:base + MLA_NOPE + MLA_ROPE]
        zero = q_all[base + MLA_NOPE + MLA_ROPE:base + HEAD_PAD]
        qh = jnp.concatenate([nope, x1 * cos_t - x2 * sin_t, x2 * cos_t + x1 * sin_t, zero], axis=0)
        q_out[0, h] = (qh * qscale).astype(q_out.dtype)


def mla_proj(z3, gq, gkv, wq_t, wk_pad, wv_t, rope):
    b, s, _ = z3.shape
    ts = min(TS, s)
    nt = s // ts
    cos_t, sin_t, kc, ksa, ksb = rope
    fixed = lambda bi, i: (0, 0)
    in_specs = [
        pl.BlockSpec((1, ts, MLA_KV_RANK), lambda bi, i: (bi, i, COL_CKV)),
        pl.BlockSpec((1, ts, MLA_Q_RANK), lambda bi, i: (bi, i, COL_CQ)),
        pl.BlockSpec((1, ts, LANES), lambda bi, i: (bi, i, COL_KR)),
        pl.BlockSpec((1, MLA_Q_RANK), fixed),
        pl.BlockSpec((1, MLA_KV_RANK), fixed),
        pl.BlockSpec((MLA_HEADS * HEAD_PAD, MLA_Q_RANK), fixed),
        pl.BlockSpec((MLA_KV_RANK, MLA_HEADS * HEAD_PAD), fixed),
        pl.BlockSpec((MLA_HEADS * MLA_V, MLA_KV_RANK), fixed),
        pl.BlockSpec((HALF_ROPE, ts), lambda bi, i: (0, i)),
        pl.BlockSpec((HALF_ROPE, ts), lambda bi, i: (0, i)),
        pl.BlockSpec((ts, LANES), lambda bi, i: (i, 0)),
        pl.BlockSpec((ts, LANES), lambda bi, i: (i, 0)),
        pl.BlockSpec((ts, LANES), lambda bi, i: (i, 0)),
    ]
    out_specs = [
        pl.BlockSpec((1, MLA_HEADS, HEAD_PAD, ts), lambda bi, i: (bi, 0, 0, i)),
        pl.BlockSpec((1, MLA_HEADS, ts, HEAD_PAD), lambda bi, i: (bi, 0, i, 0)),
        pl.BlockSpec((1, MLA_HEADS, 1, MLA_V, ts), lambda bi, i: (bi, 0, i, 0, 0)),
    ]
    out_shape = [
        jax.ShapeDtypeStruct((b, MLA_HEADS, HEAD_PAD, s), MXU_DTYPE),
        jax.ShapeDtypeStruct((b, MLA_HEADS, s, HEAD_PAD), MXU_DTYPE),
        jax.ShapeDtypeStruct((b, MLA_HEADS, nt, MLA_V, ts), MXU_DTYPE),
    ]
    return pl.pallas_call(
        _mla_proj_kernel, grid=(b, nt), in_specs=in_specs, out_specs=out_specs, out_shape=out_shape,
        compiler_params=_cparams(("parallel", "parallel")), name="mla_proj",
    )(z3, z3, z3, gq, gkv, wq_t, wk_pad, wv_t, cos_t, sin_t, kc, ksa, ksb)


def _attn_kernel(k_ref, q_ref, v_ref, o_ref, *, n_chunks, tkc):
    q_t = q_ref[0, 0]
    tq = q_t.shape[1]

    def body(c, carry):
        m, l, acc = carry
        start = pl.multiple_of(c * tkc, tkc)
        kc = k_ref[0, 0, pl.ds(start, tkc), :]
        s = jnp.dot(kc, q_t, preferred_element_type=jnp.float32)
        m_new = jnp.maximum(m, jnp.max(s, axis=0, keepdims=True))
        alpha = jnp.exp2(m - m_new)
        p = jnp.exp2(s - m_new)
        l_new = alpha * l + jnp.sum(p, axis=0, keepdims=True)
        pv = jnp.dot(v_ref[0, 0, c], p.astype(v_ref.dtype), preferred_element_type=jnp.float32)
        return m_new, l_new, alpha * acc + pv

    init = (jnp.full((1, tq), NEG_BIG, jnp.float32), jnp.zeros((1, tq), jnp.float32),
            jnp.zeros((MLA_V, tq), jnp.float32))
    _, l, acc = lax.fori_loop(0, n_chunks, body, init)
    o_ref[0, 0] = acc / l


def attention(q_t, k, v_t):
    b, h, _, s = q_t.shape
    nc, tkc = v_t.shape[2], v_t.shape[4]
    tq = min(TQ, s)
    kern = functools.partial(_attn_kernel, n_chunks=nc, tkc=tkc)
    return pl.pallas_call(
        kern, grid=(b, h, s // tq),
        in_specs=[
            pl.BlockSpec((1, 1, s, HEAD_PAD), lambda bi, hi, qi: (bi, hi, 0, 0)),
            pl.BlockSpec((1, 1, HEAD_PAD, tq), lambda bi, hi, qi: (bi, hi, 0, qi)),
            pl.BlockSpec((1, 1, nc, MLA_V, tkc), lambda bi, hi, qi: (bi, hi, 0, 0, 0)),
        ],
        out_specs=pl.BlockSpec((1, 1, MLA_V, tq), lambda bi, hi, qi: (bi, hi, 0, qi)),
        out_shape=jax.ShapeDtypeStruct((b, h, MLA_V, s), jnp.float32),
        compiler_params=_cparams(("parallel", "parallel", "parallel")), name="attention",
    )(k, q_t, v_t)


def _shift_rows(x, d, edge_rows, row):
    ts = x.shape[0]
    y = pltpu.roll(x, (-d) % ts, axis=0)
    if d < 0:
        return jnp.where(row == 0, edge_rows[0], y)
    for j in range(d):
        y = jnp.where(row == ts - d + j, edge_rows[j], y)
    return y


def _scan_rows(a, u, reverse):
    ts = a.shape[0]
    row = lax.broadcasted_iota(jnp.int32, a.shape, 0)
    d = 1
    while d < ts:
        if reverse:
            valid = row < ts - d
            shift = ts - d
        else:
            valid = row >= d
            shift = d
        a_sh = jnp.where(valid, pltpu.roll(a, shift, axis=0), 1.0)
        u_sh = jnp.where(valid, pltpu.roll(u, shift, axis=0), 0.0)
        u = u + a * u_sh
        a = a * a_sh
        d *= 2
    return a, u


def _lru_core(x, xp, xn, first, last, cw_ref, cb_ref, wg_ref, bg_ref, lam_ref, carry_ref, h_ref, reverse):
    ts = x.shape[0]
    row = lax.broadcasted_iota(jnp.int32, x.shape, 0)
    keep_prev = jnp.where(first, 0.0, 1.0)
    keep_next = jnp.where(last, 0.0, 1.0)
    prev_row = xp[SUBLANES - 1:SUBLANES] * keep_prev
    next0 = xn[0:1] * keep_next
    next1 = xn[1:2] * keep_next
    cw = cw_ref[...]
    xc = (cw[0:1] * _shift_rows(x, -1, [prev_row], row) + cw[1:2] * x
          + cw[2:3] * _shift_rows(x, 1, [next0], row)
          + cw[3:4] * _shift_rows(x, 2, [next0, next1], row) + cb_ref[...])
    gates = _dot(xc, wg_ref[...]) + bg_ref[...]
    rec = jax.nn.sigmoid(gates[:, :LRU_DIM])
    inp = jax.nn.sigmoid(gates[:, LRU_DIM:])
    neg_lam = -lam_ref[...]
    softplus = jnp.maximum(neg_lam, 0.0) + jnp.log(1.0 + jnp.exp(-jnp.abs(neg_lam)))
    log_a = -LRU_C * rec * softplus
    a = jnp.exp(log_a)
    u = jnp.sqrt(1.0 - a * a) * (inp * xc)
    a_cum, h0 = _scan_rows(a, u, reverse)

    @pl.when(pl.program_id(1) == 0)
    def _():
        carry_ref[...] = jnp.zeros_like(carry_ref)

    h = h0 + a_cum * carry_ref[0:1]
    h_ref[0] = h
    edge = h[0:1] if reverse else h[ts - 1:ts]
    carry_ref[...] = jnp.broadcast_to(edge, carry_ref.shape)


def _lru_fwd_kernel(x_ref, xp_ref, xn_ref, cc_ref, ccp_ref, ccn_ref, ch_ref, chp_ref, chn_ref, cbg_ref,
                    cw_ref, cb_ref, wg_ref, bg_ref, lam_ref, sw_ref, h_ref, y_ref, carry_ref):
    i = pl.program_id(1)
    first = i == 0
    last = i == pl.num_programs(1) - 1
    _lru_core(x_ref[0], xp_ref[0], xn_ref[0], first, last, cw_ref, cb_ref, wg_ref, bg_ref, lam_ref,
              carry_ref, h_ref, reverse=False)
    g = cc_ref[0] * ch_ref[0]
    row = lax.broadcasted_iota(jnp.int32, g.shape, 0)
    keep_prev = jnp.where(first, 0.0, 1.0)
    keep_next = jnp.where(last, 0.0, 1.0)
    g_prev = ccp_ref[0, SUBLANES - 1:SUBLANES] * chp_ref[0, SUBLANES - 1:SUBLANES] * keep_prev
    g_next = ccn_ref[0, 0:1] * chn_ref[0, 0:1] * keep_next
    sw = sw_ref[...]
    conv = (sw[0:1] * _shift_rows(g, -1, [g_prev], row) + sw[1:2] * g
            + sw[2:3] * _shift_rows(g, 1, [g_next], row))
    y_ref[0] = cbg_ref[0] * conv


def _lru_bwd_kernel(x_ref, xp_ref, xn_ref, cw_ref, cb_ref, wg_ref, bg_ref, lam_ref, h_ref, carry_ref):
    i = pl.program_id(1)
    nt = pl.num_programs(1)
    first = i == nt - 1
    last = i == 0
    _lru_core(x_ref[0], xp_ref[0], xn_ref[0], first, last, cw_ref, cb_ref, wg_ref, bg_ref, lam_ref,
              carry_ref, h_ref, reverse=True)


def lru_scan(z3, conv_w, conv_b, wg, bg, lam, short_w=None, reverse=False):
    b, s, _ = z3.shape
    ts = min(TS, s)
    nt = s // ts
    rb = ts // SUBLANES
    nrb = s // SUBLANES
    tile = (lambda i: nt - 1 - i) if reverse else (lambda i: i)

    def main(col):
        return pl.BlockSpec((1, ts, LRU_DIM), lambda bi, i: (bi, tile(i), col))

    def prev(col):
        return pl.BlockSpec((1, SUBLANES, LRU_DIM),
                            lambda bi, i: (bi, jnp.maximum(tile(i) * rb - 1, 0), col))

    def nxt(col):
        return pl.BlockSpec((1, SUBLANES, LRU_DIM),
                            lambda bi, i: (bi, jnp.minimum((tile(i) + 1) * rb, nrb - 1), col))

    fixed = lambda bi, i: (0, 0)
    par = [pl.BlockSpec((4, LRU_DIM), fixed), pl.BlockSpec((1, LRU_DIM), fixed),
           pl.BlockSpec((LRU_DIM, 2 * LRU_DIM), fixed), pl.BlockSpec((1, 2 * LRU_DIM), fixed),
           pl.BlockSpec((1, LRU_DIM), fixed)]
    h_shape = jax.ShapeDtypeStruct((b, s, LRU_DIM), jnp.float32)
    scratch = [pltpu.VMEM((SUBLANES, LRU_DIM), jnp.float32)]
    if reverse:
        return pl.pallas_call(
            _lru_bwd_kernel, grid=(b, nt),
            in_specs=[main(COL_LX), prev(COL_LX), nxt(COL_LX)] + par,
            out_specs=main(0), out_shape=h_shape, scratch_shapes=scratch,
            compiler_params=_cparams(("parallel", "arbitrary")), name="lru_bwd",
        )(z3, z3, z3, conv_w, conv_b, wg, bg, lam)
    return pl.pallas_call(
        _lru_fwd_kernel, grid=(b, nt),
        in_specs=[main(COL_LX), prev(COL_LX), nxt(COL_LX), main(COL_CC), prev(COL_CC), nxt(COL_CC),
                  main(COL_CH), prev(COL_CH), nxt(COL_CH), main(COL_CB)] + par
                 + [pl.BlockSpec((3, CONV_DIM), fixed)],
        out_specs=[main(0), main(0)], out_shape=[h_shape, h_shape], scratch_shapes=scratch,
        compiler_params=_cparams(("parallel", "arbitrary")), name="lru_fwd",
    )(z3, z3, z3, z3, z3, z3, z3, z3, z3, z3, conv_w, conv_b, wg, bg, lam, short_w)


def _post_mixer_body(yc_ref, o_ref, hf_ref, hb_ref, lg_ref, xn_ref, gm_ref, wo_ref, g1_ref, b1_ref):
    gm = gm_ref[...]
    y_lru = jax.nn.gelu(lg_ref[...], approximate=True) * (hf_ref[...] + hb_ref[...])
    y = jnp.concatenate([
        _rms_norm(yc_ref[...], gm[:, :CONV_DIM]),
        _rms_norm(o_ref[...], gm[:, CONV_DIM:CONV_DIM + MLA_HEADS * MLA_V]),
        _rms_norm(y_lru, gm[:, CONV_DIM + MLA_HEADS * MLA_V:]),
    ], axis=1)
    mix = _dot(y, wo_ref[...])
    return _layer_norm(DN_ALPHA * xn_ref[...] + mix, g1_ref[...], b1_ref[...])


def _post_mixer_kernel(yc_ref, o_ref, hf_ref, hb_ref, lg_ref, xn_ref, gm_ref, wo_ref, g1_ref, b1_ref,
                       x1_ref, x1b_ref):
    x1 = _post_mixer_body(yc_ref, o_ref, hf_ref, hb_ref, lg_ref, xn_ref, gm_ref, wo_ref, g1_ref, b1_ref)
    x1_ref[...] = x1
    x1b_ref[...] = x1.astype(x1b_ref.dtype)


def _post_mixer_router_kernel(yc_ref, o_ref, hf_ref, hb_ref, lg_ref, xn_ref, gm_ref, wo_ref, g1_ref,
                              b1_ref, wr_ref, x1_ref, route_ref):
    x1 = _post_mixer_body(yc_ref, o_ref, hf_ref, hb_ref, lg_ref, xn_ref, gm_ref, wo_ref, g1_ref, b1_ref)
    x1_ref[...] = x1
    logits = jnp.dot(x1, wr_ref[...], preferred_element_type=jnp.float32, precision=lax.Precision.HIGHEST)
    lane = lax.broadcasted_iota(jnp.int32, logits.shape, 1)
    logits = jnp.where(lane < N_EXPERTS, logits, NEG_BIG)
    v1 = jnp.max(logits, axis=1, keepdims=True)
    i1 = jnp.min(jnp.where(logits == v1, lane, LANES), axis=1, keepdims=True)
    rest = jnp.where(lane == i1, NEG_BIG, logits)
    v2 = jnp.max(rest, axis=1, keepdims=True)
    i2 = jnp.min(jnp.where(rest == v2, lane, LANES), axis=1, keepdims=True)
    e = jnp.exp(v2 - v1)
    g_top = 1.0 / (1.0 + e)
    g_sec = e * g_top
    route_ref[...] = jnp.where(lane == 0, i1.astype(jnp.float32),
                               jnp.where(lane == 1, i2.astype(jnp.float32),
                                         jnp.where(lane == 2, g_top, jnp.where(lane == 3, g_sec, 0.0))))


def post_mixer(y_conv, o, h_f, h_b, z, xn, gm, wo, g1, b1, w_router=None):
    n = xn.shape[0]
    tm = min(TM, n)
    row = lambda i: (i, 0)
    fixed = lambda i: (0, 0)
    in_specs = [
        pl.BlockSpec((tm, CONV_DIM), row), pl.BlockSpec((tm, MLA_HEADS * MLA_V), row),
        pl.BlockSpec((tm, LRU_DIM), row), pl.BlockSpec((tm, LRU_DIM), row),
        pl.BlockSpec((tm, LRU_DIM), lambda i: (i, COL_LG)), pl.BlockSpec((tm, D_MODEL), row),
        pl.BlockSpec((1, D_MODEL), fixed), pl.BlockSpec((D_MODEL, D_MODEL), fixed),
        pl.BlockSpec((1, D_MODEL), fixed), pl.BlockSpec((1, D_MODEL), fixed),
    ]
    x_spec = pl.BlockSpec((tm, D_MODEL), row)
    x_shape = jax.ShapeDtypeStruct((n, D_MODEL), jnp.float32)
    args = (y_conv, o, h_f, h_b, z, xn, gm, wo, g1, b1)
    if w_router is None:
        return pl.pallas_call(
            _post_mixer_kernel, grid=(n // tm,), in_specs=in_specs, out_specs=[x_spec, x_spec],
            out_shape=[x_shape, jax.ShapeDtypeStruct((n, D_MODEL), MXU_DTYPE)],
            compiler_params=_cparams(("parallel",)), name="post_mixer")(*args)
    return pl.pallas_call(
        _post_mixer_router_kernel, grid=(n // tm,),
        in_specs=in_specs + [pl.BlockSpec((D_MODEL, LANES), fixed)],
        out_specs=[x_spec, pl.BlockSpec((tm, LANES), row)],
        out_shape=[x_shape, jax.ShapeDtypeStruct((n, LANES), jnp.float32)],
        compiler_params=_cparams(("parallel",)), name="post_mixer_router")(*args, w_router)


def _swiglu_chunk(xb, wg, wu, wd):
    gate = jnp.dot(xb, wg, preferred_element_type=jnp.float32)
    up = jnp.dot(xb, wu, preferred_element_type=jnp.float32)
    hidden = (jax.nn.silu(gate) * up).astype(wd.dtype)
    return jnp.dot(hidden, wd, preferred_element_type=jnp.float32)


def _ffn_dense_kernel(xb_ref, x1_ref, wg_ref, wu_ref, wd_ref, g2_ref, b2_ref, out_ref, acc_ref):
    f = pl.program_id(1)

    @pl.when(f == 0)
    def _():
        acc_ref[...] = jnp.zeros_like(acc_ref)

    acc_ref[...] += _swiglu_chunk(xb_ref[...], wg_ref[...], wu_ref[...], wd_ref[...])

    @pl.when(f == pl.num_programs(1) - 1)
    def _():
        out_ref[...] = _layer_norm(DN_ALPHA * x1_ref[...] + acc_ref[...], g2_ref[...], b2_ref[...])


def ffn_dense(x1b, x1, wg, wu, wd, g2, b2):
    n = x1.shape[0]
    tm = min(TM_FFN, n)
    tf = TF
    row = lambda i, f: (i, 0)
    fixed = lambda i, f: (0, 0)
    return pl.pallas_call(
        _ffn_dense_kernel, grid=(n // tm, D_FF // tf),
        in_specs=[pl.BlockSpec((tm, D_MODEL), row), pl.BlockSpec((tm, D_MODEL), row),
                  pl.BlockSpec((D_MODEL, tf), lambda i, f: (0, f)),
                  pl.BlockSpec((D_MODEL, tf), lambda i, f: (0, f)),
                  pl.BlockSpec((tf, D_MODEL), lambda i, f: (f, 0)),
                  pl.BlockSpec((1, D_MODEL), fixed), pl.BlockSpec((1, D_MODEL), fixed)],
        out_specs=pl.BlockSpec((tm, D_MODEL), row),
        out_shape=jax.ShapeDtypeStruct((n, D_MODEL), jnp.float32),
        scratch_shapes=[pltpu.VMEM((tm, D_MODEL), jnp.float32)],
        compiler_params=_cparams(("parallel", "arbitrary")), name="ffn_dense",
    )(x1b, x1, wg, wu, wd, g2, b2)


def _row_copy(src_hbm, dst_vmem, src_row, dst_row, sem):
    return pltpu.make_async_copy(src_hbm.at[pl.ds(src_row, 1), :], dst_vmem.at[pl.ds(dst_row, 1), :], sem)


def _gather_rows(src_hbm, dst_vmem, idx_of_row, n_rows, sem):
    def start(r, c):
        _row_copy(src_hbm, dst_vmem, idx_of_row(r), r, sem).start()
        return c

    def wait(r, c):
        _row_copy(src_hbm, dst_vmem, 0, r, sem).wait()
        return c

    lax.fori_loop(0, n_rows, start, 0)
    lax.fori_loop(0, n_rows, wait, 0)


def _moe_ffn_kernel(te_ref, tv_ref, tok_ref, x_hbm, wg_ref, wu_ref, wd_ref, y_ref, xg_ref, xb_ref, acc_ref,
                    sem):
    i = pl.program_id(0)
    f = pl.program_id(1)
    valid = tv_ref[i] == 1
    tm = xg_ref.shape[0]

    @pl.when(jnp.logical_and(valid, f == 0))
    def _():
        _gather_rows(x_hbm, xg_ref, lambda r: tok_ref[0, 0, r], tm, sem)
        xb_ref[...] = xg_ref[...].astype(xb_ref.dtype)
        acc_ref[...] = jnp.zeros_like(acc_ref)

    @pl.when(valid)
    def _():
        acc_ref[...] += _swiglu_chunk(xb_ref[...], wg_ref[...], wu_ref[...], wd_ref[...])

    @pl.when(f == pl.num_programs(1) - 1)
    def _():
        y_ref[...] = jnp.where(valid, acc_ref[...], 0.0)


def moe_ffn(x1, row_token, tile_expert, tile_valid, wg, wu, wd):
    n_tiles = tile_expert.shape[0]
    tm = row_token.shape[0] // n_tiles
    tf = TF
    tok3 = row_token.reshape(n_tiles, 1, tm)
    grid_spec = pltpu.PrefetchScalarGridSpec(
        num_scalar_prefetch=2, grid=(n_tiles, D_FF // tf),
        in_specs=[
            pl.BlockSpec((1, 1, tm), lambda i, f, te, tv: (i, 0, 0), memory_space=pltpu.SMEM),
            pl.BlockSpec(memory_space=pl.ANY),
            pl.BlockSpec((None, D_MODEL, tf), lambda i, f, te, tv: (te[i], 0, f)),
            pl.BlockSpec((None, D_MODEL, tf), lambda i, f, te, tv: (te[i], 0, f)),
            pl.BlockSpec((None, tf, D_MODEL), lambda i, f, te, tv: (te[i], f, 0)),
        ],
        out_specs=pl.BlockSpec((tm, D_MODEL), lambda i, f, te, tv: (i, 0)),
        scratch_shapes=[pltpu.VMEM((tm, D_MODEL), jnp.float32), pltpu.VMEM((tm, D_MODEL), MXU_DTYPE),
                        pltpu.VMEM((tm, D_MODEL), jnp.float32), pltpu.SemaphoreType.DMA(())],
    )
    return pl.pallas_call(
        _moe_ffn_kernel, grid_spec=grid_spec,
        out_shape=jax.ShapeDtypeStruct((n_tiles * tm, D_MODEL), jnp.float32),
        compiler_params=_cparams(("arbitrary", "arbitrary")), name="moe_ffn",
    )(tile_expert, tile_valid, tok3, x1, wg, wu, wd)


def _moe_combine_kernel(pos_ref, x1_ref, gate_ref, y_hbm, g2_ref, b2_ref, out_ref, ya_ref, yb_ref, sem):
    tm = ya_ref.shape[0]
    _gather_rows(y_hbm, ya_ref, lambda r: pos_ref[0, 0, r], tm, sem)
    _gather_rows(y_hbm, yb_ref, lambda r: pos_ref[0, 1, r], tm, sem)
    gate = gate_ref[...]
    f = gate[:, 2:3] * ya_ref[...] + gate[:, 3:4] * yb_ref[...]
    out_ref[...] = _layer_norm(DN_ALPHA * x1_ref[...] + f, g2_ref[...], b2_ref[...])


def moe_combine(pos, x1, route, y_sorted, g2, b2):
    n = x1.shape[0]
    tm = min(TM_CMB, n)
    nt = n // tm
    pos3 = pos.reshape(nt, tm, 2).transpose(0, 2, 1)
    row = lambda i: (i, 0)
    fixed = lambda i: (0, 0)
    return pl.pallas_call(
        _moe_combine_kernel, grid=(nt,),
        in_specs=[pl.BlockSpec((1, 2, tm), lambda i: (i, 0, 0), memory_space=pltpu.SMEM),
                  pl.BlockSpec((tm, D_MODEL), row), pl.BlockSpec((tm, LANES), row),
                  pl.BlockSpec(memory_space=pl.ANY),
                  pl.BlockSpec((1, D_MODEL), fixed), pl.BlockSpec((1, D_MODEL), fixed)],
        out_specs=pl.BlockSpec((tm, D_MODEL), row),
        out_shape=jax.ShapeDtypeStruct((n, D_MODEL), jnp.float32),
        scratch_shapes=[pltpu.VMEM((tm, D_MODEL), jnp.float32), pltpu.VMEM((tm, D_MODEL), jnp.float32),
                        pltpu.SemaphoreType.DMA(())],
        compiler_params=_cparams(("arbitrary",)), name="moe_combine",
    )(pos3, x1, route, y_sorted, g2, b2)


def _dispatch_plan(expert_idx, tm):
    n = expert_idx.shape[0]
    e_flat = expert_idx.reshape(-1)
    onehot = (e_flat[:, None] == jnp.arange(N_EXPERTS, dtype=jnp.int32)[None, :]).astype(jnp.int32)
    csum = jnp.cumsum(onehot, axis=0)
    rank = jnp.sum((csum - onehot) * onehot, axis=1)
    counts = csum[-1]
    tiles_per = (counts + tm - 1) // tm
    tile_end = jnp.cumsum(tiles_per)
    group_start = (tile_end - tiles_per) * tm
    pos_flat = group_start[e_flat] + rank
    n_tiles = (2 * n) // tm + N_EXPERTS
    row_token = jnp.zeros((n_tiles * tm,), jnp.int32).at[pos_flat].set(
        jnp.arange(2 * n, dtype=jnp.int32) // 2)
    t = jnp.arange(n_tiles, dtype=jnp.int32)
    tile_valid = (t < tile_end[-1]).astype(jnp.int32)
    last_valid = jnp.maximum(tile_end[-1] - 1, 0)
    t_eff = jnp.minimum(t, last_valid)
    tile_expert = jnp.minimum(jnp.sum((t_eff[:, None] >= tile_end[None, :]).astype(jnp.int32), axis=1),
                              N_EXPERTS - 1).astype(jnp.int32)
    return row_token, tile_expert, tile_valid, pos_flat.reshape(n, 2).astype(jnp.int32)


def _pad_w_in(w):
    z64 = jnp.zeros((D_MODEL, KR_X1), w.dtype)
    z32 = jnp.zeros((D_MODEL, LANES - KR_X1 - MLA_ROPE), w.dtype)
    return jnp.concatenate([w[:, 0:768], w[:, 1152:1408], w[:, 1440:1696], w[:, 1696:1952],
                            w[:, 768:1152], z64, w[:, 1408:1440], z32], axis=1).astype(MXU_DTYPE)


def _rope_tables(s):
    pos = jnp.arange(s, dtype=jnp.float32)
    inv = ROPE_THETA ** (-jnp.arange(0, MLA_ROPE, 2, dtype=jnp.float32) / MLA_ROPE)
    ang = pos[:, None] * inv[None, :]
    cos, sin = jnp.cos(ang), jnp.sin(ang)
    zl = jnp.zeros((s, KR_X1), jnp.float32)
    zh = jnp.zeros((s, HALF_ROPE), jnp.float32)
    zr = jnp.zeros((s, LANES - KR_X1 - MLA_ROPE), jnp.float32)
    kc = jnp.concatenate([zl, cos, cos, zr], axis=1)
    ksa = jnp.concatenate([zl, -sin, zh, zr], axis=1)
    ksb = jnp.concatenate([zl, zh, sin, zr], axis=1)
    return cos.T, sin.T, kc, ksa, ksb


def _block_diag(w):
    nb, bw, _ = w.shape
    out = jnp.zeros((nb * bw, nb * bw), w.dtype)
    for i in range(nb):
        out = out.at[i * bw:(i + 1) * bw, i * bw:(i + 1) * bw].set(w[i])
    return out


def kernel(x, ln_in_g, ln_in_b, w_in, conv_w, q_norm_g, w_uq, kv_norm_g, w_ukv, lru_conv_w, lru_conv_b,
           lru_wa, lru_ba, lru_wi, lru_bi, lru_lam, mix_norm_g, w_out, ln1_g, ln1_b, dense_w_gate,
           dense_w_up, dense_w_down, moe_w_router, moe_w_gate, moe_w_up, moe_w_down, ln2_g, ln2_b):
    b, s, d = x.shape
    n = b * s
    rope = _rope_tables(s)
    vec = lambda v: v.reshape(1, -1)
    cur = x.reshape(n, d)
    for l in range(DEPTH):
        w_in_pad = _pad_w_in(w_in[l])
        wq_t = jnp.pad(w_uq[l].T.reshape(MLA_HEADS, MLA_NOPE + MLA_ROPE, MLA_Q_RANK),
                       ((0, 0), (0, HEAD_PAD - MLA_NOPE - MLA_ROPE), (0, 0))
                       ).reshape(MLA_HEADS * HEAD_PAD, MLA_Q_RANK).astype(MXU_DTYPE)
        wkv = w_ukv[l].reshape(MLA_KV_RANK, MLA_HEADS, MLA_NOPE + MLA_V)
        wk_pad = jnp.pad(wkv[:, :, :MLA_NOPE], ((0, 0), (0, 0), (0, HEAD_PAD - MLA_NOPE))
                         ).reshape(MLA_KV_RANK, MLA_HEADS * HEAD_PAD).astype(MXU_DTYPE)
        wv_t = wkv[:, :, MLA_NOPE:].reshape(MLA_KV_RANK, MLA_HEADS * MLA_V).T.astype(MXU_DTYPE)
        gate_w = [jnp.concatenate([_block_diag(lru_wa[l, dr]), _block_diag(lru_wi[l, dr])], axis=1
                                  ).astype(MXU_DTYPE) for dr in range(2)]
        gate_b = [jnp.concatenate([lru_ba[l, dr], lru_bi[l, dr]]).reshape(1, -1) for dr in range(2)]

        if l == 0:
            xn, z = in_proj(cur, w_in_pad, ln=(vec(ln_in_g), vec(ln_in_b)))
        else:
            xn, z = in_proj(cur, w_in_pad)
        z3 = z.reshape(b, s, D_IN_PAD)
        q_t, k, v_t = mla_proj(z3, vec(q_norm_g[l]), vec(kv_norm_g[l]), wq_t, wk_pad, wv_t, rope)
        o_t = attention(q_t, k, v_t)
        o = o_t.transpose(0, 3, 1, 2).reshape(n, MLA_HEADS * MLA_V)
        h_f, y_conv = lru_scan(z3, lru_conv_w[l], vec(lru_conv_b[l]), gate_w[0], gate_b[0],
                               vec(lru_lam[l, 0]), short_w=conv_w[l])
        h_b = lru_scan(z3, lru_conv_w[l], vec(lru_conv_b[l]), gate_w[1], gate_b[1],
                       vec(lru_lam[l, 1]), reverse=True)
        mixer_args = (y_conv.reshape(n, CONV_DIM), o, h_f.reshape(n, LRU_DIM), h_b.reshape(n, LRU_DIM),
                      z, xn, vec(mix_norm_g[l]), w_out[l].astype(MXU_DTYPE), vec(ln1_g[l]), vec(ln1_b[l]))

        j = l // 2
        if l % 2 == 0:
            x1, x1b = post_mixer(*mixer_args)
            cur = ffn_dense(x1b, x1, dense_w_gate[j].astype(MXU_DTYPE), dense_w_up[j].astype(MXU_DTYPE),
                            dense_w_down[j].astype(MXU_DTYPE), vec(ln2_g[l]), vec(ln2_b[l]))
        else:
            w_router = jnp.pad(moe_w_router[j], ((0, 0), (0, LANES - N_EXPERTS)))
            x1, route = post_mixer(*mixer_args, w_router=w_router)
            expert_idx = route[:, 0:2].astype(jnp.int32)
            tm = min(TM_FFN, n)
            row_token, tile_expert, tile_valid, pos = _dispatch_plan(expert_idx, tm)
            y_sorted = moe_ffn(x1, row_token, tile_expert, tile_valid, moe_w_gate[j].astype(MXU_DTYPE),
                               moe_w_up[j].astype(MXU_DTYPE), moe_w_down[j].astype(MXU_DTYPE))
            cur = moe_combine(pos, x1, route, y_sorted, vec(ln2_g[l]), vec(ln2_b[l]))
    return cur.reshape(b, s, d)
```

```python
import functools
import math

import jax
import jax.numpy as jnp
from jax import lax
from jax.experimental import pallas as pl
from jax.experimental.pallas import tpu as pltpu

D_MODEL = 1024
DEPTH = 2
CONV_DIM = 256
MLA_HEADS = 8
MLA_NOPE = 64
MLA_ROPE = 32
MLA_V = 64
MLA_Q_RANK = 384
MLA_KV_RANK = 256
LRU_DIM = 256
LRU_C = 8.0
ROPE_THETA = 10000.0
D_FF = 3584
N_EXPERTS = 8
DN_ALPHA = (2.0 * DEPTH) ** 0.25
LN_EPS = 1e-5
RMS_EPS = 1e-6

D_IN_PAD = 2048
HALF_ROPE = MLA_ROPE // 2
HEAD_PAD = 128
COL_CB, COL_CC, COL_CH, COL_CKV, COL_LG, COL_LX = 0, 1, 2, 3, 4, 5
COL_CQ = 4
COL_KR = 15
KR_X1 = 64
V_ROWS = MLA_V + 16

LANES = 128
SUBLANES = 8
VMEM_LIMIT = 56 * 1024 * 1024
MXU_DTYPE = jnp.bfloat16

TM = 512
TS = 512
TQ = 512
ATTN_GROUP = 4
TM_FFN = 1024
TF = 512
TM_CMB = 256
NEG_BIG = -1e30
LOG2E = 1.4426950408889634


def _cparams(sem, vmem=VMEM_LIMIT, flags=None):
    return pltpu.CompilerParams(dimension_semantics=sem, vmem_limit_bytes=vmem, flags=flags)


def _layer_norm(x, g, b):
    mu = jnp.mean(x, axis=-1, keepdims=True)
    xc = x - mu
    var = jnp.mean(xc * xc, axis=-1, keepdims=True)
    return xc * lax.rsqrt(var + LN_EPS) * g + b


def _rms_norm(x, g):
    ms = jnp.mean(x * x, axis=-1, keepdims=True)
    return x * lax.rsqrt(ms + RMS_EPS) * g


def _dot(a, b):
    return jnp.dot(a.astype(MXU_DTYPE), b.astype(MXU_DTYPE), preferred_element_type=jnp.float32)


def _dot_nt(a, b):
    return lax.dot_general(a.astype(MXU_DTYPE), b.astype(MXU_DTYPE), (((1,), (1,)), ((), ())),
                           preferred_element_type=jnp.float32)


def _in_proj_ln_kernel(x_ref, g_ref, b_ref, w_ref, xn_ref, z_ref):
    xn = _layer_norm(x_ref[...], g_ref[...], b_ref[...])
    xn_ref[...] = xn
    z_ref[...] = _dot(xn, w_ref[...])


def _in_proj_kernel(x_ref, w_ref, z_ref):
    z_ref[...] = _dot(x_ref[...], w_ref[...])


def in_proj(x, w_pad, ln=None):
    n = x.shape[0]
    tm = min(TM, n)
    grid = (n // tm,)
    row = lambda i: (i, 0)
    fixed = lambda i: (0, 0)
    x_spec = pl.BlockSpec((tm, D_MODEL), row)
    w_spec = pl.BlockSpec((D_MODEL, D_IN_PAD), fixed)
    z_spec = pl.BlockSpec((tm, D_IN_PAD), row)
    z_shape = jax.ShapeDtypeStruct((n, D_IN_PAD), jnp.float32)
    if ln is None:
        z = pl.pallas_call(
            _in_proj_kernel, grid=grid, in_specs=[x_spec, w_spec], out_specs=z_spec, out_shape=z_shape,
            compiler_params=_cparams(("parallel",)), name="in_proj")(x, w_pad)
        return x, z
    g, b = ln
    vec = pl.BlockSpec((1, D_MODEL), fixed)
    xn, z = pl.pallas_call(
        _in_proj_ln_kernel, grid=grid, in_specs=[x_spec, vec, vec, w_spec],
        out_specs=[x_spec, z_spec],
        out_shape=[jax.ShapeDtypeStruct((n, D_MODEL), jnp.float32), z_shape],
        compiler_params=_cparams(("parallel",)), name="in_proj_ln")(x, g, b, w_pad)
    return xn, z


def _mla_proj_kernel(ckv_ref, cq_ref, kr_ref, gq_ref, gkv_ref, wq_ref, wk_ref, wv_ref,
                     cos_ref, sin_ref, kc_ref, ksa_ref, ksb_ref, q_out, k_out, v_out):
    cqn = _rms_norm(cq_ref[0], gq_ref[...])
    ckvn = _rms_norm(ckv_ref[0], gkv_ref[...])
    kr = kr_ref[0]

    krope = (kr * kc_ref[...]
             + pltpu.roll(kr, LANES - HALF_ROPE, axis=1) * ksa_ref[...]
             + pltpu.roll(kr, HALF_ROPE, axis=1) * ksb_ref[...])
    k_all = _dot(ckvn, wk_ref[...])
    for h in range(MLA_HEADS):
        k_out[0, h] = (k_all[:, h * HEAD_PAD:(h + 1) * HEAD_PAD] + krope).astype(k_out.dtype)

    v_all = _dot_nt(wv_ref[...], ckvn)
    ones = jnp.ones((V_ROWS - MLA_V, v_all.shape[1]), jnp.float32)
    for h in range(MLA_HEADS):
        v_out[0, h, 0] = jnp.concatenate([v_all[h * MLA_V:(h + 1) * MLA_V], ones], axis=0).astype(v_out.dtype)

    q_all = _dot_nt(wq_ref[...], cqn)
    cos_t = cos_ref[...]
    sin_t = sin_ref[...]
    qscale = (MLA_NOPE + MLA_ROPE) ** -0.5 * LOG2E
    for h in range(MLA_HEADS):
        base = h * HEAD_PAD
        nope = q_all[base:base + MLA_NOPE]
        x1 = q_all[base + MLA_NOPE:base + MLA_NOPE + HALF_ROPE]
        x2 = q_all[base + MLA_NOPE + HALF_ROPE:base + MLA_NOPE + MLA_ROPE]
        zero = q_all[base + MLA_NOPE + MLA_ROPE:base + HEAD_PAD]
        qh = jnp.concatenate([nope, x1 * cos_t - x2 * sin_t, x2 * cos_t + x1 * sin_t, zero], axis=0)
        q_out[0, h] = (qh * qscale).astype(q_out.dtype)


def mla_proj(z3, gq, gkv, wq_t, wk_pad, wv_t, rope):
    b, s, _ = z3.shape
    ts = min(TS, s)
    nt = s // ts
    cos_t, sin_t, kc, ksa, ksb = rope
    fixed = lambda bi, i: (0, 0)
    in_specs = [
        pl.BlockSpec((1, ts, MLA_KV_RANK), lambda bi, i: (bi, i, COL_CKV)),
        pl.BlockSpec((1, ts, MLA_Q_RANK), lambda bi, i: (bi, i, COL_CQ)),
        pl.BlockSpec((1, ts, LANES), lambda bi, i: (bi, i, COL_KR)),
        pl.BlockSpec((1, MLA_Q_RANK), fixed),
        pl.BlockSpec((1, MLA_KV_RANK), fixed),
        pl.BlockSpec((MLA_HEADS * HEAD_PAD, MLA_Q_RANK), fixed),
        pl.BlockSpec((MLA_KV_RANK, MLA_HEADS * HEAD_PAD), fixed),
        pl.BlockSpec((MLA_HEADS * MLA_V, MLA_KV_RANK), fixed),
        pl.BlockSpec((HALF_ROPE, ts), lambda bi, i: (0, i)),
        pl.BlockSpec((HALF_ROPE, ts), lambda bi, i: (0, i)),
        pl.BlockSpec((ts, LANES), lambda bi, i: (i, 0)),
        pl.BlockSpec((ts, LANES), lambda bi, i: (i, 0)),
        pl.BlockSpec((ts, LANES), lambda bi, i: (i, 0)),
    ]
    out_specs = [
        pl.BlockSpec((1, MLA_HEADS, HEAD_PAD, ts), lambda bi, i: (bi, 0, 0, i)),
        pl.BlockSpec((1, MLA_HEADS, ts, HEAD_PAD), lambda bi, i: (bi, 0, i, 0)),
        pl.BlockSpec((1, MLA_HEADS, 1, V_ROWS, ts), lambda bi, i: (bi, 0, i, 0, 0)),
    ]
    out_shape = [
        jax.ShapeDtypeStruct((b, MLA_HEADS, HEAD_PAD, s), MXU_DTYPE),
        jax.ShapeDtypeStruct((b, MLA_HEADS, s, HEAD_PAD), MXU_DTYPE),
        jax.ShapeDtypeStruct((b, MLA_HEADS, nt, V_ROWS, ts), MXU_DTYPE),
    ]
    return pl.pallas_call(
        _mla_proj_kernel, grid=(b, nt), in_specs=in_specs, out_specs=out_specs, out_shape=out_shape,
        compiler_params=_cparams(("parallel", "parallel")), name="mla_proj",
    )(z3, z3, z3, gq, gkv, wq_t, wk_pad, wv_t, cos_t, sin_t, kc, ksa, ksb)


def _attn_kernel(k_ref, q_ref, v_ref, o_ref, s0_ref, s1_ref, p0_ref, p1_ref, *, n_chunks, tkc, group_size):
    q_t = q_ref[0, 0]
    tq = q_t.shape[1]

    def scores(c, s_ref):
        start = pl.multiple_of(c * tkc, tkc)
        s = jnp.dot(k_ref[0, 0, pl.ds(start, tkc), :], q_t, preferred_element_type=jnp.float32)
        s_ref[...] = s
        return jnp.max(s, axis=0, keepdims=True)

    def accumulate(c, s_ref, p_ref, mx, m, acc):
        m_new = jnp.maximum(m, mx)
        alpha = jnp.exp2(m - m_new)
        p_ref[...] = jnp.exp2(s_ref[...] - m_new).astype(p_ref.dtype)
        pv = jnp.dot(v_ref[0, 0, c], p_ref[...], preferred_element_type=jnp.float32)
        return m_new, alpha * acc + pv

    bufs = (s0_ref, s1_ref)
    pbufs = (p0_ref, p1_ref)

    def group(c0, mx, m, acc, prefetch_last):
        for g in range(group_size):
            mx_next = None
            if g + 1 < group_size or prefetch_last:
                mx_next = scores(c0 + g + 1, bufs[(g + 1) % 2])
            m, acc = accumulate(c0 + g, bufs[g % 2], pbufs[g % 2], mx, m, acc)
            mx = mx_next
        return mx, m, acc

    def body(j, carry):
        return group(j * group_size, *carry, prefetch_last=True)

    carry = (scores(0, bufs[0]), jnp.full((1, tq), NEG_BIG, jnp.float32),
             jnp.zeros((v_ref.shape[3], tq), jnp.float32))
    n_groups = n_chunks // group_size
    carry = lax.fori_loop(0, n_groups - 1, body, carry)
    _, _, acc = group((n_groups - 1) * group_size, *carry, prefetch_last=False)
    o_ref[0, 0] = acc[:MLA_V] / acc[MLA_V:MLA_V + 1]


def attention(q_t, k, v_t):
    b, h, _, s = q_t.shape
    nc, tkc = v_t.shape[2], v_t.shape[4]
    tq = min(TQ, s)
    group_size = min(ATTN_GROUP, nc)
    assert group_size % 2 == 0 and nc % group_size == 0, "chunk groups alternate two buffers"
    kern = functools.partial(_attn_kernel, n_chunks=nc, tkc=tkc, group_size=group_size)
    return pl.pallas_call(
        kern, grid=(b, h, s // tq),
        in_specs=[
            pl.BlockSpec((1, 1, s, HEAD_PAD), lambda bi, hi, qi: (bi, hi, 0, 0)),
            pl.BlockSpec((1, 1, HEAD_PAD, tq), lambda bi, hi, qi: (bi, hi, 0, qi)),
            pl.BlockSpec((1, 1, nc, V_ROWS, tkc), lambda bi, hi, qi: (bi, hi, 0, 0, 0)),
        ],
        out_specs=pl.BlockSpec((1, 1, MLA_V, tq), lambda bi, hi, qi: (bi, hi, 0, qi)),
        out_shape=jax.ShapeDtypeStruct((b, h, MLA_V, s), jnp.float32),
        scratch_shapes=[pltpu.VMEM((tkc, tq), jnp.float32), pltpu.VMEM((tkc, tq), jnp.float32),
                        pltpu.VMEM((tkc, tq), MXU_DTYPE), pltpu.VMEM((tkc, tq), MXU_DTYPE)],
        compiler_params=_cparams(("parallel", "parallel", "parallel")), name="attention",
    )(k, q_t, v_t)


def _shift_rows(x, d, edge_rows, row):
    ts = x.shape[0]
    y = pltpu.roll(x, (-d) % ts, axis=0)
    if d < 0:
        return jnp.where(row == 0, edge_rows[0], y)
    for j in range(d):
        y = jnp.where(row == ts - d + j, edge_rows[j], y)
    return y


def _scan_rows(a, u, reverse):
    ts = a.shape[0]
    row = lax.broadcasted_iota(jnp.int32, a.shape, 0)
    d = 1
    while d < ts:
        if reverse:
            valid = row < ts - d
            shift = ts - d
        else:
            valid = row >= d
            shift = d
        a_sh = jnp.where(valid, pltpu.roll(a, shift, axis=0), 1.0)
        u_sh = jnp.where(valid, pltpu.roll(u, shift, axis=0), 0.0)
        u = u + a * u_sh
        a = a * a_sh
        d *= 2
    return a, u


def _lru_core(x, xp, xn, first, last, cw_ref, cb_ref, wg_ref, bg_ref, lam_ref, carry_ref, h_ref, reverse):
    ts = x.shape[0]
    row = lax.broadcasted_iota(jnp.int32, x.shape, 0)
    keep_prev = jnp.where(first, 0.0, 1.0)
    keep_next = jnp.where(last, 0.0, 1.0)
    prev_row = xp[SUBLANES - 1:SUBLANES] * keep_prev
    next0 = xn[0:1] * keep_next
    next1 = xn[1:2] * keep_next
    cw = cw_ref[...]
    xc = (cw[0:1] * _shift_rows(x, -1, [prev_row], row) + cw[1:2] * x
          + cw[2:3] * _shift_rows(x, 1, [next0], row)
          + cw[3:4] * _shift_rows(x, 2, [next0, next1], row) + cb_ref[...])
    gates = _dot(xc, wg_ref[...]) + bg_ref[...]
    rec = jax.nn.sigmoid(gates[:, :LRU_DIM])
    inp = jax.nn.sigmoid(gates[:, LRU_DIM:])
    neg_lam = -lam_ref[...]
    softplus = jnp.maximum(neg_lam, 0.0) + jnp.log(1.0 + jnp.exp(-jnp.abs(neg_lam)))
    log_a = -LRU_C * rec * softplus
    a = jnp.exp(log_a)
    u = jnp.sqrt(1.0 - a * a) * (inp * xc)
    a_cum, h0 = _scan_rows(a, u, reverse)

    @pl.when(pl.program_id(1) == 0)
    def _():
        carry_ref[...] = jnp.zeros_like(carry_ref)

    h = h0 + a_cum * carry_ref[0:1]
    h_ref[0] = h
    edge = h[0:1] if reverse else h[ts - 1:ts]
    carry_ref[...] = jnp.broadcast_to(edge, carry_ref.shape)


def _lru_fwd_kernel(x_ref, xp_ref, xn_ref, cc_ref, ccp_ref, ccn_ref, ch_ref, chp_ref, chn_ref, cbg_ref,
                    cw_ref, cb_ref, wg_ref, bg_ref, lam_ref, sw_ref, h_ref, y_ref, carry_ref):
    i = pl.program_id(1)
    first = i == 0
    last = i == pl.num_programs(1) - 1
    _lru_core(x_ref[0], xp_ref[0], xn_ref[0], first, last, cw_ref, cb_ref, wg_ref, bg_ref, lam_ref,
              carry_ref, h_ref, reverse=False)
    g = cc_ref[0] * ch_ref[0]
    row = lax.broadcasted_iota(jnp.int32, g.shape, 0)
    keep_prev = jnp.where(first, 0.0, 1.0)
    keep_next = jnp.where(last, 0.0, 1.0)
    g_prev = ccp_ref[0, SUBLANES - 1:SUBLANES] * chp_ref[0, SUBLANES - 1:SUBLANES] * keep_prev
    g_next = ccn_ref[0, 0:1] * chn_ref[0, 0:1] * keep_next
    sw = sw_ref[...]
    conv = (sw[0:1] * _shift_rows(g, -1, [g_prev], row) + sw[1:2] * g
            + sw[2:3] * _shift_rows(g, 1, [g_next], row))
    y_ref[0] = cbg_ref[0] * conv


def _lru_bwd_kernel(x_ref, xp_ref, xn_ref, cw_ref, cb_ref, wg_ref, bg_ref, lam_ref, h_ref, carry_ref):
    i = pl.program_id(1)
    nt = pl.num_programs(1)
    first = i == nt - 1
    last = i == 0
    _lru_core(x_ref[0], xp_ref[0], xn_ref[0], first, last, cw_ref, cb_ref, wg_ref, bg_ref, lam_ref,
              carry_ref, h_ref, reverse=True)


def lru_scan(z3, conv_w, conv_b, wg, bg, lam, short_w=None, reverse=False):
    b, s, _ = z3.shape
    ts = min(TS, s)
    nt = s // ts
    rb = ts // SUBLANES
    nrb = s // SUBLANES
    tile = (lambda i: nt - 1 - i) if reverse else (lambda i: i)

    def main(col):
        return pl.BlockSpec((1, ts, LRU_DIM), lambda bi, i: (bi, tile(i), col))

    def prev(col):
        return pl.BlockSpec((1, SUBLANES, LRU_DIM),
                            lambda bi, i: (bi, jnp.maximum(tile(i) * rb - 1, 0), col))

    def nxt(col):
        return pl.BlockSpec((1, SUBLANES, LRU_DIM),
                            lambda bi, i: (bi, jnp.minimum((tile(i) + 1) * rb, nrb - 1), col))

    fixed = lambda bi, i: (0, 0)
    par = [pl.BlockSpec((4, LRU_DIM), fixed), pl.BlockSpec((1, LRU_DIM), fixed),
           pl.BlockSpec((LRU_DIM, 2 * LRU_DIM), fixed), pl.BlockSpec((1, 2 * LRU_DIM), fixed),
           pl.BlockSpec((1, LRU_DIM), fixed)]
    h_shape = jax.ShapeDtypeStruct((b, s, LRU_DIM), jnp.float32)
    scratch = [pltpu.VMEM((SUBLANES, LRU_DIM), jnp.float32)]
    if reverse:
        return pl.pallas_call(
            _lru_bwd_kernel, grid=(b, nt),
            in_specs=[main(COL_LX), prev(COL_LX), nxt(COL_LX)] + par,
            out_specs=main(0), out_shape=h_shape, scratch_shapes=scratch,
            compiler_params=_cparams(("parallel", "arbitrary")), name="lru_bwd",
        )(z3, z3, z3, conv_w, conv_b, wg, bg, lam)
    return pl.pallas_call(
        _lru_fwd_kernel, grid=(b, nt),
        in_specs=[main(COL_LX), prev(COL_LX), nxt(COL_LX), main(COL_CC), prev(COL_CC), nxt(COL_CC),
                  main(COL_CH), prev(COL_CH), nxt(COL_CH), main(COL_CB)] + par
                 + [pl.BlockSpec((3, CONV_DIM), fixed)],
        out_specs=[main(0), main(0)], out_shape=[h_shape, h_shape], scratch_shapes=scratch,
        compiler_params=_cparams(("parallel", "arbitrary")), name="lru_fwd",
    )(z3, z3, z3, z3, z3, z3, z3, z3, z3, z3, conv_w, conv_b, wg, bg, lam, short_w)


def _post_mixer_body(yc_ref, o_ref, hf_ref, hb_ref, lg_ref, xn_ref, gm_ref, wo_ref, g1_ref, b1_ref):
    gm = gm_ref[...]
    y_lru = jax.nn.gelu(lg_ref[...], approximate=True) * (hf_ref[...] + hb_ref[...])
    y = jnp.concatenate([
        _rms_norm(yc_ref[...], gm[:, :CONV_DIM]),
        _rms_norm(o_ref[...], gm[:, CONV_DIM:CONV_DIM + MLA_HEADS * MLA_V]),
        _rms_norm(y_lru, gm[:, CONV_DIM + MLA_HEADS * MLA_V:]),
    ], axis=1)
    mix = _dot(y, wo_ref[...])
    return _layer_norm(DN_ALPHA * xn_ref[...] + mix, g1_ref[...], b1_ref[...])


def _post_mixer_kernel(yc_ref, o_ref, hf_ref, hb_ref, lg_ref, xn_ref, gm_ref, wo_ref, g1_ref, b1_ref,
                       x1_ref, x1b_ref):
    x1 = _post_mixer_body(yc_ref, o_ref, hf_ref, hb_ref, lg_ref, xn_ref, gm_ref, wo_ref, g1_ref, b1_ref)
    x1_ref[...] = x1
    x1b_ref[...] = x1.astype(x1b_ref.dtype)


def _post_mixer_router_kernel(yc_ref, o_ref, hf_ref, hb_ref, lg_ref, xn_ref, gm_ref, wo_ref, g1_ref,
                              b1_ref, wr_ref, x1_ref, route_ref):
    x1 = _post_mixer_body(yc_ref, o_ref, hf_ref, hb_ref, lg_ref, xn_ref, gm_ref, wo_ref, g1_ref, b1_ref)
    x1_ref[...] = x1
    logits = jnp.dot(x1, wr_ref[...], preferred_element_type=jnp.float32, precision=lax.Precision.HIGHEST)
    lane = lax.broadcasted_iota(jnp.int32, logits.shape, 1)
    logits = jnp.where(lane < N_EXPERTS, logits, NEG_BIG)
    v1 = jnp.max(logits, axis=1, keepdims=True)
    i1 = jnp.min(jnp.where(logits == v1, lane, LANES), axis=1, keepdims=True)
    rest = jnp.where(lane == i1, NEG_BIG, logits)
    v2 = jnp.max(rest, axis=1, keepdims=True)
    i2 = jnp.min(jnp.where(rest == v2, lane, LANES), axis=1, keepdims=True)
    e = jnp.exp(v2 - v1)
    g_top = 1.0 / (1.0 + e)
    g_sec = e * g_top
    route_ref[...] = jnp.where(lane == 0, i1.astype(jnp.float32),
                               jnp.where(lane == 1, i2.astype(jnp.float32),
                                         jnp.where(lane == 2, g_top, jnp.where(lane == 3, g_sec, 0.0))))


def post_mixer(y_conv, o, h_f, h_b, z, xn, gm, wo, g1, b1, w_router=None):
    n = xn.shape[0]
    tm = min(TM, n)
    row = lambda i: (i, 0)
    fixed = lambda i: (0, 0)
    in_specs = [
        pl.BlockSpec((tm, CONV_DIM), row), pl.BlockSpec((tm, MLA_HEADS * MLA_V), row),
        pl.BlockSpec((tm, LRU_DIM), row), pl.BlockSpec((tm, LRU_DIM), row),
        pl.BlockSpec((tm, LRU_DIM), lambda i: (i, COL_LG)), pl.BlockSpec((tm, D_MODEL), row),
        pl.BlockSpec((1, D_MODEL), fixed), pl.BlockSpec((D_MODEL, D_MODEL), fixed),
        pl.BlockSpec((1, D_MODEL), fixed), pl.BlockSpec((1, D_MODEL), fixed),
    ]
    x_spec = pl.BlockSpec((tm, D_MODEL), row)
    x_shape = jax.ShapeDtypeStruct((n, D_MODEL), jnp.float32)
    args = (y_conv, o, h_f, h_b, z, xn, gm, wo, g1, b1)
    if w_router is None:
        return pl.pallas_call(
            _post_mixer_kernel, grid=(n // tm,), in_specs=in_specs, out_specs=[x_spec, x_spec],
            out_shape=[x_shape, jax.ShapeDtypeStruct((n, D_MODEL), MXU_DTYPE)],
            compiler_params=_cparams(("parallel",)), name="post_mixer")(*args)
    return pl.pallas_call(
        _post_mixer_router_kernel, grid=(n // tm,),
        in_specs=in_specs + [pl.BlockSpec((D_MODEL, LANES), fixed)],
        out_specs=[x_spec, pl.BlockSpec((tm, LANES), row)],
        out_shape=[x_shape, jax.ShapeDtypeStruct((n, LANES), jnp.float32)],
        compiler_params=_cparams(("parallel",)), name="post_mixer_router")(*args, w_router)


def _swiglu_chunk(xb, wg, wu, wd):
    gate = jnp.dot(xb, wg, preferred_element_type=jnp.float32)
    up = jnp.dot(xb, wu, preferred_element_type=jnp.float32)
    hidden = (jax.nn.silu(gate) * up).astype(wd.dtype)
    return jnp.dot(hidden, wd, preferred_element_type=jnp.float32)


def _ffn_dense_kernel(xb_ref, x1_ref, wg_ref, wu_ref, wd_ref, g2_ref, b2_ref, out_ref, acc_ref):
    f = pl.program_id(1)

    @pl.when(f == 0)
    def _():
        acc_ref[...] = jnp.zeros_like(acc_ref)

    acc_ref[...] += _swiglu_chunk(xb_ref[...], wg_ref[...], wu_ref[...], wd_ref[...])

    @pl.when(f == pl.num_programs(1) - 1)
    def _():
        out_ref[...] = _layer_norm(DN_ALPHA * x1_ref[...] + acc_ref[...], g2_ref[...], b2_ref[...])


def ffn_dense(x1b, x1, wg, wu, wd, g2, b2):
    n = x1.shape[0]
    tm = min(TM_FFN, n)
    tf = TF
    row = lambda i, f: (i, 0)
    fixed = lambda i, f: (0, 0)
    return pl.pallas_call(
        _ffn_dense_kernel, grid=(n // tm, D_FF // tf),
        in_specs=[pl.BlockSpec((tm, D_MODEL), row), pl.BlockSpec((tm, D_MODEL), row),
                  pl.BlockSpec((D_MODEL, tf), lambda i, f: (0, f)),
                  pl.BlockSpec((D_MODEL, tf), lambda i, f: (0, f)),
                  pl.BlockSpec((tf, D_MODEL), lambda i, f: (f, 0)),
                  pl.BlockSpec((1, D_MODEL), fixed), pl.BlockSpec((1, D_MODEL), fixed)],
        out_specs=pl.BlockSpec((tm, D_MODEL), row),
        out_shape=jax.ShapeDtypeStruct((n, D_MODEL), jnp.float32),
        scratch_shapes=[pltpu.VMEM((tm, D_MODEL), jnp.float32)],
        compiler_params=_cparams(("parallel", "arbitrary")), name="ffn_dense",
    )(x1b, x1, wg, wu, wd, g2, b2)


def _row_copy(src_hbm, dst_vmem, src_row, dst_row, sem):
    return pltpu.make_async_copy(src_hbm.at[pl.ds(src_row, 1), :], dst_vmem.at[pl.ds(dst_row, 1), :], sem)


def _gather_rows(src_hbm, dst_vmem, idx_of_row, n_rows, sem):
    def start(r, c):
        _row_copy(src_hbm, dst_vmem, idx_of_row(r), r, sem).start()
        return c

    def wait(r, c):
        _row_copy(src_hbm, dst_vmem, 0, r, sem).wait()
        return c

    lax.fori_loop(0, n_rows, start, 0, unroll=8)
    lax.fori_loop(0, n_rows, wait, 0, unroll=8)


def _moe_ffn_kernel(te_ref, tv_ref, tok_ref, x_hbm, wg_ref, wu_ref, wd_ref, y_ref, xg_ref, xb_ref, acc_ref,
                    sem):
    i = pl.program_id(0)
    f = pl.program_id(1)
    valid = tv_ref[i] == 1
    tm = xg_ref.shape[0]

    @pl.when(jnp.logical_and(valid, f == 0))
    def _():
        _gather_rows(x_hbm, xg_ref, lambda r: tok_ref[0, 0, r], tm, sem)
        xb_ref[...] = xg_ref[...].astype(xb_ref.dtype)
        acc_ref[...] = jnp.zeros_like(acc_ref)

    @pl.when(valid)
    def _():
        acc_ref[...] += _swiglu_chunk(xb_ref[...], wg_ref[...], wu_ref[...], wd_ref[...])

    @pl.when(f == pl.num_programs(1) - 1)
    def _():
        y_ref[...] = jnp.where(valid, acc_ref[...], 0.0)


def moe_ffn(x1, row_token, tile_expert, tile_valid, wg, wu, wd):
    n_tiles = tile_expert.shape[0]
    tm = row_token.shape[0] // n_tiles
    tf = TF
    tok3 = row_token.reshape(n_tiles, 1, tm)
    grid_spec = pltpu.PrefetchScalarGridSpec(
        num_scalar_prefetch=2, grid=(n_tiles, D_FF // tf),
        in_specs=[
            pl.BlockSpec((1, 1, tm), lambda i, f, te, tv: (i, 0, 0), memory_space=pltpu.SMEM),
            pl.BlockSpec(memory_space=pl.ANY),
            pl.BlockSpec((None, D_MODEL, tf), lambda i, f, te, tv: (te[i], 0, f)),
            pl.BlockSpec((None, D_MODEL, tf), lambda i, f, te, tv: (te[i], 0, f)),
            pl.BlockSpec((None, tf, D_MODEL), lambda i, f, te, tv: (te[i], f, 0)),
        ],
        out_specs=pl.BlockSpec((tm, D_MODEL), lambda i, f, te, tv: (i, 0)),
        scratch_shapes=[pltpu.VMEM((tm, D_MODEL), jnp.float32), pltpu.VMEM((tm, D_MODEL), MXU_DTYPE),
                        pltpu.VMEM((tm, D_MODEL), jnp.float32), pltpu.SemaphoreType.DMA(())],
    )
    return pl.pallas_call(
        _moe_ffn_kernel, grid_spec=grid_spec,
        out_shape=jax.ShapeDtypeStruct((n_tiles * tm, D_MODEL), jnp.float32),
        compiler_params=_cparams(("arbitrary", "arbitrary")), name="moe_ffn",
    )(tile_expert, tile_valid, tok3, x1, wg, wu, wd)


def _moe_combine_kernel(pos_ref, x1_ref, gate_ref, y_hbm, g2_ref, b2_ref, out_ref, ya_ref, yb_ref, sem):
    tm = ya_ref.shape[0]
    _gather_rows(y_hbm, ya_ref, lambda r: pos_ref[0, 0, r], tm, sem)
    _gather_rows(y_hbm, yb_ref, lambda r: pos_ref[0, 1, r], tm, sem)
    gate = gate_ref[...]
    f = gate[:, 2:3] * ya_ref[...] + gate[:, 3:4] * yb_ref[...]
    out_ref[...] = _layer_norm(DN_ALPHA * x1_ref[...] + f, g2_ref[...], b2_ref[...])


def moe_combine(pos, x1, route, y_sorted, g2, b2):
    n = x1.shape[0]
    tm = min(TM_CMB, n)
    nt = n // tm
    pos3 = pos.reshape(nt, tm, 2).transpose(0, 2, 1)
    row = lambda i: (i, 0)
    fixed = lambda i: (0, 0)
    return pl.pallas_call(
        _moe_combine_kernel, grid=(nt,),
        in_specs=[pl.BlockSpec((1, 2, tm), lambda i: (i, 0, 0), memory_space=pltpu.SMEM),
                  pl.BlockSpec((tm, D_MODEL), row), pl.BlockSpec((tm, LANES), row),
                  pl.BlockSpec(memory_space=pl.ANY),
                  pl.BlockSpec((1, D_MODEL), fixed), pl.BlockSpec((1, D_MODEL), fixed)],
        out_specs=pl.BlockSpec((tm, D_MODEL), row),
        out_shape=jax.ShapeDtypeStruct((n, D_MODEL), jnp.float32),
        scratch_shapes=[pltpu.VMEM((tm, D_MODEL), jnp.float32), pltpu.VMEM((tm, D_MODEL), jnp.float32),
                        pltpu.SemaphoreType.DMA(())],
        compiler_params=_cparams(("arbitrary",)), name="moe_combine",
    )(pos3, x1, route, y_sorted, g2, b2)


def _dispatch_plan(expert_idx, tm):
    n = expert_idx.shape[0]
    e_flat = expert_idx.reshape(-1)
    onehot = (e_flat[:, None] == jnp.arange(N_EXPERTS, dtype=jnp.int32)[None, :]).astype(jnp.int32)
    csum = jnp.cumsum(onehot, axis=0)
    rank = jnp.sum((csum - onehot) * onehot, axis=1)
    counts = csum[-1]
    tiles_per = (counts + tm - 1) // tm
    tile_end = jnp.cumsum(tiles_per)
    group_start = (tile_end - tiles_per) * tm
    pos_flat = group_start[e_flat] + rank
    n_tiles = (2 * n) // tm + N_EXPERTS
    row_token = jnp.zeros((n_tiles * tm,), jnp.int32).at[pos_flat].set(
        jnp.arange(2 * n, dtype=jnp.int32) // 2)
    t = jnp.arange(n_tiles, dtype=jnp.int32)
    tile_valid = (t < tile_end[-1]).astype(jnp.int32)
    last_valid = jnp.maximum(tile_end[-1] - 1, 0)
    t_eff = jnp.minimum(t, last_valid)
    tile_expert = jnp.minimum(jnp.sum((t_eff[:, None] >= tile_end[None, :]).astype(jnp.int32), axis=1),
                              N_EXPERTS - 1).astype(jnp.int32)
    return row_token, tile_expert, tile_valid, pos_flat.reshape(n, 2).astype(jnp.int32)


def _pad_w_in(w):
    z64 = jnp.zeros((D_MODEL, KR_X1), w.dtype)
    z32 = jnp.zeros((D_MODEL, LANES - KR_X1 - MLA_ROPE), w.dtype)
    return jnp.concatenate([w[:, 0:768], w[:, 1152:1408], w[:, 1440:1696], w[:, 1696:1952],
                            w[:, 768:1152], z64, w[:, 1408:1440], z32], axis=1).astype(MXU_DTYPE)


def _rope_tables(s):
    pos = jnp.arange(s, dtype=jnp.float32)
    inv = ROPE_THETA ** (-jnp.arange(0, MLA_ROPE, 2, dtype=jnp.float32) / MLA_ROPE)
    ang = pos[:, None] * inv[None, :]
    cos, sin = jnp.cos(ang), jnp.sin(ang)
    zl = jnp.zeros((s, KR_X1), jnp.float32)
    zh = jnp.zeros((s, HALF_ROPE), jnp.float32)
    zr = jnp.zeros((s, LANES - KR_X1 - MLA_ROPE), jnp.float32)
    kc = jnp.concatenate([zl, cos, cos, zr], axis=1)
    ksa = jnp.concatenate([zl, -sin, zh, zr], axis=1)
    ksb = jnp.concatenate([zl, zh, sin, zr], axis=1)
    return cos.T, sin.T, kc, ksa, ksb


def _block_diag(w):
    nb, bw, _ = w.shape
    out = jnp.zeros((nb * bw, nb * bw), w.dtype)
    for i in range(nb):
        out = out.at[i * bw:(i + 1) * bw, i * bw:(i + 1) * bw].set(w[i])
    return out


def kernel(x, ln_in_g, ln_in_b, w_in, conv_w, q_norm_g, w_uq, kv_norm_g, w_ukv, lru_conv_w, lru_conv_b,
           lru_wa, lru_ba, lru_wi, lru_bi, lru_lam, mix_norm_g, w_out, ln1_g, ln1_b, dense_w_gate,
           dense_w_up, dense_w_down, moe_w_router, moe_w_gate, moe_w_up, moe_w_down, ln2_g, ln2_b):
    b, s, d = x.shape
    n = b * s
    rope = _rope_tables(s)
    vec = lambda v: v.reshape(1, -1)
    cur = x.reshape(n, d)
    for l in range(DEPTH):
        w_in_pad = _pad_w_in(w_in[l])
        wq_t = jnp.pad(w_uq[l].T.reshape(MLA_HEADS, MLA_NOPE + MLA_ROPE, MLA_Q_RANK),
                       ((0, 0), (0, HEAD_PAD - MLA_NOPE - MLA_ROPE), (0, 0))
                       ).reshape(MLA_HEADS * HEAD_PAD, MLA_Q_RANK).astype(MXU_DTYPE)
        wkv = w_ukv[l].reshape(MLA_KV_RANK, MLA_HEADS, MLA_NOPE + MLA_V)
        wk_pad = jnp.pad(wkv[:, :, :MLA_NOPE], ((0, 0), (0, 0), (0, HEAD_PAD - MLA_NOPE))
                         ).reshape(MLA_KV_RANK, MLA_HEADS * HEAD_PAD).astype(MXU_DTYPE)
        wv_t = wkv[:, :, MLA_NOPE:].reshape(MLA_KV_RANK, MLA_HEADS * MLA_V).T.astype(MXU_DTYPE)
        gate_w = [jnp.concatenate([_block_diag(lru_wa[l, dr]), _block_diag(lru_wi[l, dr])], axis=1
                                  ).astype(MXU_DTYPE) for dr in range(2)]
        gate_b = [jnp.concatenate([lru_ba[l, dr], lru_bi[l, dr]]).reshape(1, -1) for dr in range(2)]

        if l == 0:
            xn, z = in_proj(cur, w_in_pad, ln=(vec(ln_in_g), vec(ln_in_b)))
        else:
            xn, z = in_proj(cur, w_in_pad)
        z3 = z.reshape(b, s, D_IN_PAD)
        q_t, k, v_t = mla_proj(z3, vec(q_norm_g[l]), vec(kv_norm_g[l]), wq_t, wk_pad, wv_t, rope)
        o_t = attention(q_t, k, v_t)
        o = o_t.transpose(0, 3, 1, 2).reshape(n, MLA_HEADS * MLA_V)
        h_f, y_conv = lru_scan(z3, lru_conv_w[l], vec(lru_conv_b[l]), gate_w[0], gate_b[0],
                               vec(lru_lam[l, 0]), short_w=conv_w[l])
        h_b = lru_scan(z3, lru_conv_w[l], vec(lru_conv_b[l]), gate_w[1], gate_b[1],
                       vec(lru_lam[l, 1]), reverse=True)
        mixer_args = (y_conv.reshape(n, CONV_DIM), o, h_f.reshape(n, LRU_DIM), h_b.reshape(n, LRU_DIM),
                      z, xn, vec(mix_norm_g[l]), w_out[l].astype(MXU_DTYPE), vec(ln1_g[l]), vec(ln1_b[l]))

        j = l // 2
        if l % 2 == 0:
            x1, x1b = post_mixer(*mixer_args)
            cur = ffn_dense(x1b, x1, dense_w_gate[j].astype(MXU_DTYPE), dense_w_up[j].astype(MXU_DTYPE),
                            dense_w_down[j].astype(MXU_DTYPE), vec(ln2_g[l]), vec(ln2_b[l]))
        else:
            w_router = jnp.pad(moe_w_router[j], ((0, 0), (0, LANES - N_EXPERTS)))
            x1, route = post_mixer(*mixer_args, w_router=w_router)
            expert_idx = route[:, 0:2].astype(jnp.int32)
            tm = min(TM_FFN, n)
            row_token, tile_expert, tile_valid, pos = _dispatch_plan(expert_idx, tm)
            y_sorted = moe_ffn(x1, row_token, tile_expert, tile_valid, moe_w_gate[j].astype(MXU_DTYPE),
                               moe_w_up[j].astype(MXU_DTYPE), moe_w_down[j].astype(MXU_DTYPE))
            cur = moe_combine(pos, x1, route, y_sorted, vec(ln2_g[l]), vec(ln2_b[l]))
    return cur.reshape(b, s, d)
```

```python
import functools
import math

import jax
import jax.numpy as jnp
from jax import lax
from jax.experimental import pallas as pl
from jax.experimental.pallas import tpu as pltpu

D_MODEL = 1024
DEPTH = 2
CONV_DIM = 256
MLA_HEADS = 8
MLA_NOPE = 64
MLA_ROPE = 32
MLA_V = 64
MLA_Q_RANK = 384
MLA_KV_RANK = 256
LRU_DIM = 256
LRU_C = 8.0
ROPE_THETA = 10000.0
D_FF = 3584
N_EXPERTS = 8
DN_ALPHA = (2.0 * DEPTH) ** 0.25
LN_EPS = 1e-5
RMS_EPS = 1e-6

D_IN_PAD = 2048
HALF_ROPE = MLA_ROPE // 2
HEAD_PAD = 128
COL_CB, COL_CC, COL_CH, COL_CKV, COL_LG, COL_LX = 0, 1, 2, 3, 4, 5
COL_CQ = 4
COL_KR = 15
KR_X1 = 64
V_ROWS = MLA_V + 16

LANES = 128
SUBLANES = 8
VMEM_LIMIT = 56 * 1024 * 1024
MXU_DTYPE = jnp.bfloat16

TM = 512
TS = 512
TQ = 512
ATTN_GROUP = 4
TM_FFN = 1024
TF = 512
TM_CMB = 256
CMB_BLK = 16
NEG_BIG = -1e30
LOG2E = 1.4426950408889634


def _cparams(sem, vmem=VMEM_LIMIT, flags=None):
    return pltpu.CompilerParams(dimension_semantics=sem, vmem_limit_bytes=vmem, flags=flags)


def _layer_norm(x, g, b):
    mu = jnp.mean(x, axis=-1, keepdims=True)
    xc = x - mu
    var = jnp.mean(xc * xc, axis=-1, keepdims=True)
    return xc * lax.rsqrt(var + LN_EPS) * g + b


def _rms_norm(x, g):
    ms = jnp.mean(x * x, axis=-1, keepdims=True)
    return x * lax.rsqrt(ms + RMS_EPS) * g


def _dot(a, b):
    return jnp.dot(a.astype(MXU_DTYPE), b.astype(MXU_DTYPE), preferred_element_type=jnp.float32)


def _dot_nt(a, b):
    return lax.dot_general(a.astype(MXU_DTYPE), b.astype(MXU_DTYPE), (((1,), (1,)), ((), ())),
                           preferred_element_type=jnp.float32)


def _in_proj_ln_kernel(x_ref, g_ref, b_ref, w_ref, xn_ref, z_ref):
    xn = _layer_norm(x_ref[...], g_ref[...], b_ref[...])
    xn_ref[...] = xn
    z_ref[...] = _dot(xn, w_ref[...])


def _in_proj_kernel(x_ref, w_ref, z_ref):
    z_ref[...] = _dot(x_ref[...], w_ref[...])


def in_proj(x, w_pad, ln=None):
    n = x.shape[0]
    tm = min(TM, n)
    grid = (n // tm,)
    row = lambda i: (i, 0)
    fixed = lambda i: (0, 0)
    x_spec = pl.BlockSpec((tm, D_MODEL), row)
    w_spec = pl.BlockSpec((D_MODEL, D_IN_PAD), fixed)
    z_spec = pl.BlockSpec((tm, D_IN_PAD), row)
    z_shape = jax.ShapeDtypeStruct((n, D_IN_PAD), jnp.float32)
    if ln is None:
        z = pl.pallas_call(
            _in_proj_kernel, grid=grid, in_specs=[x_spec, w_spec], out_specs=z_spec, out_shape=z_shape,
            compiler_params=_cparams(("parallel",)), name="in_proj")(x, w_pad)
        return x, z
    g, b = ln
    vec = pl.BlockSpec((1, D_MODEL), fixed)
    xn, z = pl.pallas_call(
        _in_proj_ln_kernel, grid=grid, in_specs=[x_spec, vec, vec, w_spec],
        out_specs=[x_spec, z_spec],
        out_shape=[jax.ShapeDtypeStruct((n, D_MODEL), jnp.float32), z_shape],
        compiler_params=_cparams(("parallel",)), name="in_proj_ln")(x, g, b, w_pad)
    return xn, z


def _mla_proj_kernel(ckv_ref, cq_ref, kr_ref, gq_ref, gkv_ref, wq_ref, wk_ref, wv_ref,
                     cos_ref, sin_ref, kc_ref, ksa_ref, ksb_ref, q_out, k_out, v_out):
    cqn = _rms_norm(cq_ref[0], gq_ref[...])
    ckvn = _rms_norm(ckv_ref[0], gkv_ref[...])
    kr = kr_ref[0]

    krope = (kr * kc_ref[...]
             + pltpu.roll(kr, LANES - HALF_ROPE, axis=1) * ksa_ref[...]
             + pltpu.roll(kr, HALF_ROPE, axis=1) * ksb_ref[...])
    k_all = _dot(ckvn, wk_ref[...])
    for h in range(MLA_HEADS):
        k_out[0, h] = (k_all[:, h * HEAD_PAD:(h + 1) * HEAD_PAD] + krope).astype(k_out.dtype)

    v_all = _dot_nt(wv_ref[...], ckvn)
    ones = jnp.ones((V_ROWS - MLA_V, v_all.shape[1]), jnp.float32)
    for h in range(MLA_HEADS):
        v_out[0, h, 0] = jnp.concatenate([v_all[h * MLA_V:(h + 1) * MLA_V], ones], axis=0).astype(v_out.dtype)

    q_all = _dot_nt(wq_ref[...], cqn)
    cos_t = cos_ref[...]
    sin_t = sin_ref[...]
    qscale = (MLA_NOPE + MLA_ROPE) ** -0.5 * LOG2E
    for h in range(MLA_HEADS):
        base = h * HEAD_PAD
        nope = q_all[base:base + MLA_NOPE]
        x1 = q_all[base + MLA_NOPE:base + MLA_NOPE + HALF_ROPE]
        x2 = q_all[base + MLA_NOPE + HALF_ROPE:base + MLA_NOPE + MLA_ROPE]
        zero = q_all[base + MLA_NOPE + MLA_ROPE:base + HEAD_PAD]
        qh = jnp.concatenate([nope, x1 * cos_t - x2 * sin_t, x2 * cos_t + x1 * sin_t, zero], axis=0)
        q_out[0, h] = (qh * qscale).astype(q_out.dtype)


def mla_proj(z3, gq, gkv, wq_t, wk_pad, wv_t, rope):
    b, s, _ = z3.shape
    ts = min(TS, s)
    nt = s // ts
    cos_t, sin_t, kc, ksa, ksb = rope
    fixed = lambda bi, i: (0, 0)
    in_specs = [
        pl.BlockSpec((1, ts, MLA_KV_RANK), lambda bi, i: (bi, i, COL_CKV)),
        pl.BlockSpec((1, ts, MLA_Q_RANK), lambda bi, i: (bi, i, COL_CQ)),
        pl.BlockSpec((1, ts, LANES), lambda bi, i: (bi, i, COL_KR)),
        pl.BlockSpec((1, MLA_Q_RANK), fixed),
        pl.BlockSpec((1, MLA_KV_RANK), fixed),
        pl.BlockSpec((MLA_HEADS * HEAD_PAD, MLA_Q_RANK), fixed),
        pl.BlockSpec((MLA_KV_RANK, MLA_HEADS * HEAD_PAD), fixed),
        pl.BlockSpec((MLA_HEADS * MLA_V, MLA_KV_RANK), fixed),
        pl.BlockSpec((HALF_ROPE, ts), lambda bi, i: (0, i)),
        pl.BlockSpec((HALF_ROPE, ts), lambda bi, i: (0, i)),
        pl.BlockSpec((ts, LANES), lambda bi, i: (i, 0)),
        pl.BlockSpec((ts, LANES), lambda bi, i: (i, 0)),
        pl.BlockSpec((ts, LANES), lambda bi, i: (i, 0)),
    ]
    out_specs = [
        pl.BlockSpec((1, MLA_HEADS, HEAD_PAD, ts), lambda bi, i: (bi, 0, 0, i)),
        pl.BlockSpec((1, MLA_HEADS, ts, HEAD_PAD), lambda bi, i: (bi, 0, i, 0)),
        pl.BlockSpec((1, MLA_HEADS, 1, V_ROWS, ts), lambda bi, i: (bi, 0, i, 0, 0)),
    ]
    out_shape = [
        jax.ShapeDtypeStruct((b, MLA_HEADS, HEAD_PAD, s), MXU_DTYPE),
        jax.ShapeDtypeStruct((b, MLA_HEADS, s, HEAD_PAD), MXU_DTYPE),
        jax.ShapeDtypeStruct((b, MLA_HEADS, nt, V_ROWS, ts), MXU_DTYPE),
    ]
    return pl.pallas_call(
        _mla_proj_kernel, grid=(b, nt), in_specs=in_specs, out_specs=out_specs, out_shape=out_shape,
        compiler_params=_cparams(("parallel", "parallel")), name="mla_proj",
    )(z3, z3, z3, gq, gkv, wq_t, wk_pad, wv_t, cos_t, sin_t, kc, ksa, ksb)


def _attn_kernel(k_ref, q_ref, v_ref, o_ref, s0_ref, s1_ref, p0_ref, p1_ref, *, n_chunks, tkc, group_size):
    q_t = q_ref[0, 0]
    tq = q_t.shape[1]

    def scores(c, s_ref):
        start = pl.multiple_of(c * tkc, tkc)
        s = jnp.dot(k_ref[0, 0, pl.ds(start, tkc), :], q_t, preferred_element_type=jnp.float32)
        s_ref[...] = s
        return jnp.max(s, axis=0, keepdims=True)

    def accumulate(c, s_ref, p_ref, mx, m, acc):
        m_new = jnp.maximum(m, mx)
        alpha = jnp.exp2(m - m_new)
        p_ref[...] = jnp.exp2(s_ref[...] - m_new).astype(p_ref.dtype)
        pv = jnp.dot(v_ref[0, 0, c], p_ref[...], preferred_element_type=jnp.float32)
        return m_new, alpha * acc + pv

    bufs = (s0_ref, s1_ref)
    pbufs = (p0_ref, p1_ref)

    def group(c0, mx, m, acc, prefetch_last):
        for g in range(group_size):
            mx_next = None
            if g + 1 < group_size or prefetch_last:
                mx_next = scores(c0 + g + 1, bufs[(g + 1) % 2])
            m, acc = accumulate(c0 + g, bufs[g % 2], pbufs[g % 2], mx, m, acc)
            mx = mx_next
        return mx, m, acc

    def body(j, carry):
        return group(j * group_size, *carry, prefetch_last=True)

    carry = (scores(0, bufs[0]), jnp.full((1, tq), NEG_BIG, jnp.float32),
             jnp.zeros((v_ref.shape[3], tq), jnp.float32))
    n_groups = n_chunks // group_size
    carry = lax.fori_loop(0, n_groups - 1, body, carry)
    _, _, acc = group((n_groups - 1) * group_size, *carry, prefetch_last=False)
    o_ref[0, 0] = acc[:MLA_V] / acc[MLA_V:MLA_V + 1]


def attention(q_t, k, v_t):
    b, h, _, s = q_t.shape
    nc, tkc = v_t.shape[2], v_t.shape[4]
    tq = min(TQ, s)
    group_size = min(ATTN_GROUP, nc)
    assert group_size % 2 == 0 and nc % group_size == 0, "chunk groups alternate two buffers"
    kern = functools.partial(_attn_kernel, n_chunks=nc, tkc=tkc, group_size=group_size)
    return pl.pallas_call(
        kern, grid=(b, h, s // tq),
        in_specs=[
            pl.BlockSpec((1, 1, s, HEAD_PAD), lambda bi, hi, qi: (bi, hi, 0, 0)),
            pl.BlockSpec((1, 1, HEAD_PAD, tq), lambda bi, hi, qi: (bi, hi, 0, qi)),
            pl.BlockSpec((1, 1, nc, V_ROWS, tkc), lambda bi, hi, qi: (bi, hi, 0, 0, 0)),
        ],
        out_specs=pl.BlockSpec((1, 1, MLA_V, tq), lambda bi, hi, qi: (bi, hi, 0, qi)),
        out_shape=jax.ShapeDtypeStruct((b, h, MLA_V, s), jnp.float32),
        scratch_shapes=[pltpu.VMEM((tkc, tq), jnp.float32), pltpu.VMEM((tkc, tq), jnp.float32),
                        pltpu.VMEM((tkc, tq), MXU_DTYPE), pltpu.VMEM((tkc, tq), MXU_DTYPE)],
        compiler_params=_cparams(("parallel", "parallel", "parallel")), name="attention",
    )(k, q_t, v_t)


def _shift_rows(x, d, edge_rows, row):
    ts = x.shape[0]
    y = pltpu.roll(x, (-d) % ts, axis=0)
    if d < 0:
        return jnp.where(row == 0, edge_rows[0], y)
    for j in range(d):
        y = jnp.where(row == ts - d + j, edge_rows[j], y)
    return y


def _scan_rows(a, u, reverse):
    ts = a.shape[0]
    row = lax.broadcasted_iota(jnp.int32, a.shape, 0)
    d = 1
    while d < ts:
        if reverse:
            valid = row < ts - d
            shift = ts - d
        else:
            valid = row >= d
            shift = d
        a_sh = jnp.where(valid, pltpu.roll(a, shift, axis=0), 1.0)
        u_sh = jnp.where(valid, pltpu.roll(u, shift, axis=0), 0.0)
        u = u + a * u_sh
        a = a * a_sh
        d *= 2
    return a, u


def _lru_core(x, xp, xn, first, last, cw_ref, cb_ref, wg_ref, bg_ref, lam_ref, carry_ref, h_ref, reverse):
    ts = x.shape[0]
    row = lax.broadcasted_iota(jnp.int32, x.shape, 0)
    keep_prev = jnp.where(first, 0.0, 1.0)
    keep_next = jnp.where(last, 0.0, 1.0)
    prev_row = xp[SUBLANES - 1:SUBLANES] * keep_prev
    next0 = xn[0:1] * keep_next
    next1 = xn[1:2] * keep_next
    cw = cw_ref[...]
    xc = (cw[0:1] * _shift_rows(x, -1, [prev_row], row) + cw[1:2] * x
          + cw[2:3] * _shift_rows(x, 1, [next0], row)
          + cw[3:4] * _shift_rows(x, 2, [next0, next1], row) + cb_ref[...])
    gates = _dot(xc, wg_ref[...]) + bg_ref[...]
    rec = jax.nn.sigmoid(gates[:, :LRU_DIM])
    inp = jax.nn.sigmoid(gates[:, LRU_DIM:])
    neg_lam = -lam_ref[...]
    softplus = jnp.maximum(neg_lam, 0.0) + jnp.log(1.0 + jnp.exp(-jnp.abs(neg_lam)))
    log_a = -LRU_C * rec * softplus
    a = jnp.exp(log_a)
    u = jnp.sqrt(1.0 - a * a) * (inp * xc)
    a_cum, h0 = _scan_rows(a, u, reverse)

    @pl.when(pl.program_id(1) == 0)
    def _():
        carry_ref[...] = jnp.zeros_like(carry_ref)

    h = h0 + a_cum * carry_ref[0:1]
    h_ref[0] = h
    edge = h[0:1] if reverse else h[ts - 1:ts]
    carry_ref[...] = jnp.broadcast_to(edge, carry_ref.shape)


def _lru_fwd_kernel(x_ref, xp_ref, xn_ref, cc_ref, ccp_ref, ccn_ref, ch_ref, chp_ref, chn_ref, cbg_ref,
                    cw_ref, cb_ref, wg_ref, bg_ref, lam_ref, sw_ref, h_ref, y_ref, carry_ref):
    i = pl.program_id(1)
    first = i == 0
    last = i == pl.num_programs(1) - 1
    _lru_core(x_ref[0], xp_ref[0], xn_ref[0], first, last, cw_ref, cb_ref, wg_ref, bg_ref, lam_ref,
              carry_ref, h_ref, reverse=False)
    g = cc_ref[0] * ch_ref[0]
    row = lax.broadcasted_iota(jnp.int32, g.shape, 0)
    keep_prev = jnp.where(first, 0.0, 1.0)
    keep_next = jnp.where(last, 0.0, 1.0)
    g_prev = ccp_ref[0, SUBLANES - 1:SUBLANES] * chp_ref[0, SUBLANES - 1:SUBLANES] * keep_prev
    g_next = ccn_ref[0, 0:1] * chn_ref[0, 0:1] * keep_next
    sw = sw_ref[...]
    conv = (sw[0:1] * _shift_rows(g, -1, [g_prev], row) + sw[1:2] * g
            + sw[2:3] * _shift_rows(g, 1, [g_next], row))
    y_ref[0] = cbg_ref[0] * conv


def _lru_bwd_kernel(x_ref, xp_ref, xn_ref, cw_ref, cb_ref, wg_ref, bg_ref, lam_ref, h_ref, carry_ref):
    i = pl.program_id(1)
    nt = pl.num_programs(1)
    first = i == nt - 1
    last = i == 0
    _lru_core(x_ref[0], xp_ref[0], xn_ref[0], first, last, cw_ref, cb_ref, wg_ref, bg_ref, lam_ref,
              carry_ref, h_ref, reverse=True)


def lru_scan(z3, conv_w, conv_b, wg, bg, lam, short_w=None, reverse=False):
    b, s, _ = z3.shape
    ts = min(TS, s)
    nt = s // ts
    rb = ts // SUBLANES
    nrb = s // SUBLANES
    tile = (lambda i: nt - 1 - i) if reverse else (lambda i: i)

    def main(col):
        return pl.BlockSpec((1, ts, LRU_DIM), lambda bi, i: (bi, tile(i), col))

    def prev(col):
        return pl.BlockSpec((1, SUBLANES, LRU_DIM),
                            lambda bi, i: (bi, jnp.maximum(tile(i) * rb - 1, 0), col))

    def nxt(col):
        return pl.BlockSpec((1, SUBLANES, LRU_DIM),
                            lambda bi, i: (bi, jnp.minimum((tile(i) + 1) * rb, nrb - 1), col))

    fixed = lambda bi, i: (0, 0)
    par = [pl.BlockSpec((4, LRU_DIM), fixed), pl.BlockSpec((1, LRU_DIM), fixed),
           pl.BlockSpec((LRU_DIM, 2 * LRU_DIM), fixed), pl.BlockSpec((1, 2 * LRU_DIM), fixed),
           pl.BlockSpec((1, LRU_DIM), fixed)]
    h_shape = jax.ShapeDtypeStruct((b, s, LRU_DIM), jnp.float32)
    scratch = [pltpu.VMEM((SUBLANES, LRU_DIM), jnp.float32)]
    if reverse:
        return pl.pallas_call(
            _lru_bwd_kernel, grid=(b, nt),
            in_specs=[main(COL_LX), prev(COL_LX), nxt(COL_LX)] + par,
            out_specs=main(0), out_shape=h_shape, scratch_shapes=scratch,
            compiler_params=_cparams(("parallel", "arbitrary")), name="lru_bwd",
        )(z3, z3, z3, conv_w, conv_b, wg, bg, lam)
    return pl.pallas_call(
        _lru_fwd_kernel, grid=(b, nt),
        in_specs=[main(COL_LX), prev(COL_LX), nxt(COL_LX), main(COL_CC), prev(COL_CC), nxt(COL_CC),
                  main(COL_CH), prev(COL_CH), nxt(COL_CH), main(COL_CB)] + par
                 + [pl.BlockSpec((3, CONV_DIM), fixed)],
        out_specs=[main(0), main(0)], out_shape=[h_shape, h_shape], scratch_shapes=scratch,
        compiler_params=_cparams(("parallel", "arbitrary")), name="lru_fwd",
    )(z3, z3, z3, z3, z3, z3, z3, z3, z3, z3, conv_w, conv_b, wg, bg, lam, short_w)


def _post_mixer_body(yc_ref, o_ref, hf_ref, hb_ref, lg_ref, xn_ref, gm_ref, wo_ref, g1_ref, b1_ref):
    gm = gm_ref[...]
    y_lru = jax.nn.gelu(lg_ref[...], approximate=True) * (hf_ref[...] + hb_ref[...])
    y = jnp.concatenate([
        _rms_norm(yc_ref[...], gm[:, :CONV_DIM]),
        _rms_norm(o_ref[...], gm[:, CONV_DIM:CONV_DIM + MLA_HEADS * MLA_V]),
        _rms_norm(y_lru, gm[:, CONV_DIM + MLA_HEADS * MLA_V:]),
    ], axis=1)
    mix = _dot(y, wo_ref[...])
    return _layer_norm(DN_ALPHA * xn_ref[...] + mix, g1_ref[...], b1_ref[...])


def _post_mixer_kernel(yc_ref, o_ref, hf_ref, hb_ref, lg_ref, xn_ref, gm_ref, wo_ref, g1_ref, b1_ref,
                       x1_ref, x1b_ref):
    x1 = _post_mixer_body(yc_ref, o_ref, hf_ref, hb_ref, lg_ref, xn_ref, gm_ref, wo_ref, g1_ref, b1_ref)
    x1_ref[...] = x1
    x1b_ref[...] = x1.astype(x1b_ref.dtype)


def _post_mixer_router_kernel(yc_ref, o_ref, hf_ref, hb_ref, lg_ref, xn_ref, gm_ref, wo_ref, g1_ref,
                              b1_ref, wr_ref, x1_ref, route_ref):
    x1 = _post_mixer_body(yc_ref, o_ref, hf_ref, hb_ref, lg_ref, xn_ref, gm_ref, wo_ref, g1_ref, b1_ref)
    x1_ref[...] = x1
    wr = wr_ref[...]
    x_hi = x1.astype(jnp.bfloat16)
    x_lo = (x1 - x_hi.astype(jnp.float32)).astype(jnp.bfloat16)
    w_hi = wr.astype(jnp.bfloat16)
    w_lo = (wr - w_hi.astype(jnp.float32)).astype(jnp.bfloat16)
    logits = (jnp.dot(x_hi, w_hi, preferred_element_type=jnp.float32)
              + jnp.dot(x_lo, w_hi, preferred_element_type=jnp.float32)
              + jnp.dot(x_hi, w_lo, preferred_element_type=jnp.float32))
    lane = lax.broadcasted_iota(jnp.int32, logits.shape, 1)
    logits = jnp.where(lane < N_EXPERTS, logits, NEG_BIG)
    v1 = jnp.max(logits, axis=1, keepdims=True)
    i1 = jnp.min(jnp.where(logits == v1, lane, LANES), axis=1, keepdims=True)
    rest = jnp.where(lane == i1, NEG_BIG, logits)
    v2 = jnp.max(rest, axis=1, keepdims=True)
    i2 = jnp.min(jnp.where(rest == v2, lane, LANES), axis=1, keepdims=True)
    e = jnp.exp(v2 - v1)
    g_top = 1.0 / (1.0 + e)
    g_sec = e * g_top
    route_ref[...] = jnp.where(lane == 0, i1.astype(jnp.float32),
                               jnp.where(lane == 1, i2.astype(jnp.float32),
                                         jnp.where(lane == 2, g_top, jnp.where(lane == 3, g_sec, 0.0))))


def post_mixer(y_conv, o, h_f, h_b, z, xn, gm, wo, g1, b1, w_router=None):
    n = xn.shape[0]
    tm = min(TM, n)
    row = lambda i: (i, 0)
    fixed = lambda i: (0, 0)
    in_specs = [
        pl.BlockSpec((tm, CONV_DIM), row), pl.BlockSpec((tm, MLA_HEADS * MLA_V), row),
        pl.BlockSpec((tm, LRU_DIM), row), pl.BlockSpec((tm, LRU_DIM), row),
        pl.BlockSpec((tm, LRU_DIM), lambda i: (i, COL_LG)), pl.BlockSpec((tm, D_MODEL), row),
        pl.BlockSpec((1, D_MODEL), fixed), pl.BlockSpec((D_MODEL, D_MODEL), fixed),
        pl.BlockSpec((1, D_MODEL), fixed), pl.BlockSpec((1, D_MODEL), fixed),
    ]
    x_spec = pl.BlockSpec((tm, D_MODEL), row)
    x_shape = jax.ShapeDtypeStruct((n, D_MODEL), jnp.float32)
    args = (y_conv, o, h_f, h_b, z, xn, gm, wo, g1, b1)
    if w_router is None:
        return pl.pallas_call(
            _post_mixer_kernel, grid=(n // tm,), in_specs=in_specs, out_specs=[x_spec, x_spec],
            out_shape=[x_shape, jax.ShapeDtypeStruct((n, D_MODEL), MXU_DTYPE)],
            compiler_params=_cparams(("parallel",)), name="post_mixer")(*args)
    return pl.pallas_call(
        _post_mixer_router_kernel, grid=(n // tm,),
        in_specs=in_specs + [pl.BlockSpec((D_MODEL, LANES), fixed)],
        out_specs=[x_spec, pl.BlockSpec((tm, LANES), row)],
        out_shape=[x_shape, jax.ShapeDtypeStruct((n, LANES), jnp.float32)],
        compiler_params=_cparams(("parallel",)), name="post_mixer_router")(*args, w_router)


def _swiglu_chunk(xb, wg, wu, wd):
    gate = jnp.dot(xb, wg, preferred_element_type=jnp.float32)
    up = jnp.dot(xb, wu, preferred_element_type=jnp.float32)
    hidden = (jax.nn.silu(gate) * up).astype(wd.dtype)
    return jnp.dot(hidden, wd, preferred_element_type=jnp.float32)


def _ffn_dense_kernel(xb_ref, x1_ref, wg_ref, wu_ref, wd_ref, g2_ref, b2_ref, out_ref, acc_ref):
    f = pl.program_id(1)

    @pl.when(f == 0)
    def _():
        acc_ref[...] = jnp.zeros_like(acc_ref)

    acc_ref[...] += _swiglu_chunk(xb_ref[...], wg_ref[...], wu_ref[...], wd_ref[...])

    @pl.when(f == pl.num_programs(1) - 1)
    def _():
        out_ref[...] = _layer_norm(DN_ALPHA * x1_ref[...] + acc_ref[...], g2_ref[...], b2_ref[...])


def ffn_dense(x1b, x1, wg, wu, wd, g2, b2):
    n = x1.shape[0]
    tm = min(TM_FFN, n)
    tf = TF
    row = lambda i, f: (i, 0)
    fixed = lambda i, f: (0, 0)
    return pl.pallas_call(
        _ffn_dense_kernel, grid=(n // tm, D_FF // tf),
        in_specs=[pl.BlockSpec((tm, D_MODEL), row), pl.BlockSpec((tm, D_MODEL), row),
                  pl.BlockSpec((D_MODEL, tf), lambda i, f: (0, f)),
                  pl.BlockSpec((D_MODEL, tf), lambda i, f: (0, f)),
                  pl.BlockSpec((tf, D_MODEL), lambda i, f: (f, 0)),
                  pl.BlockSpec((1, D_MODEL), fixed), pl.BlockSpec((1, D_MODEL), fixed)],
        out_specs=pl.BlockSpec((tm, D_MODEL), row),
        out_shape=jax.ShapeDtypeStruct((n, D_MODEL), jnp.float32),
        scratch_shapes=[pltpu.VMEM((tm, D_MODEL), jnp.float32)],
        compiler_params=_cparams(("parallel", "arbitrary")), name="ffn_dense",
    )(x1b, x1, wg, wu, wd, g2, b2)


def _moe_ffn_kernel(te_ref, tv_ref, tok_ref, tok_next_ref, x_hbm, wg_ref, wu_ref, wd_ref, y_ref, xg_ref, xb_ref,
                    acc_ref, sem):
    i = pl.program_id(0)
    f = pl.program_id(1)
    valid = tv_ref[i] == 1
    tm = xb_ref.shape[0]
    slot = i % 2

    def tile_copy(tok, r, sl):
        dst = pl.multiple_of(r * SUBLANES, SUBLANES)
        return pltpu.make_async_copy(x_hbm.at[tok], xg_ref.at[sl, pl.ds(dst, SUBLANES), :], sem.at[sl])

    def start_gather(tokens_ref, sl):
        def start(r, c):
            tile_copy(tokens_ref[0, 0, r], r, sl).start()
            return c

        lax.fori_loop(0, tm, start, 0, unroll=8)

    @pl.when(jnp.logical_and(valid, f == 0))
    def _():
        @pl.when(i == 0)
        def _():
            start_gather(tok_ref, slot)

        def wait(r, c):
            tile_copy(0, r, slot).wait()
            return c

        lax.fori_loop(0, tm, wait, 0, unroll=8)
        for s in range(SUBLANES):
            xb_ref[:, s * LANES:(s + 1) * LANES] = (
                xg_ref[slot, pl.ds(s, tm, stride=SUBLANES), :].astype(xb_ref.dtype))
        acc_ref[...] = jnp.zeros_like(acc_ref)

        nxt = jnp.minimum(i + 1, pl.num_programs(0) - 1)

        @pl.when(jnp.logical_and(i + 1 < pl.num_programs(0), tv_ref[nxt] == 1))
        def _():
            start_gather(tok_next_ref, 1 - slot)

    @pl.when(valid)
    def _():
        acc_ref[...] += _swiglu_chunk(xb_ref[...], wg_ref[...], wu_ref[...], wd_ref[...])

    @pl.when(f == pl.num_programs(1) - 1)
    def _():
        y_ref[...] = jnp.where(valid, acc_ref[...], 0.0).astype(y_ref.dtype)


def moe_ffn(x_tiles, row_token, tile_expert, tile_valid, wg, wu, wd):
    n_tiles = tile_expert.shape[0]
    tm = row_token.shape[0] // n_tiles
    tf = TF
    tok3 = row_token.reshape(n_tiles, 1, tm)
    grid_spec = pltpu.PrefetchScalarGridSpec(
        num_scalar_prefetch=2, grid=(n_tiles, D_FF // tf),
        in_specs=[
            pl.BlockSpec((1, 1, tm), lambda i, f, te, tv: (i, 0, 0), memory_space=pltpu.SMEM),
            pl.BlockSpec((1, 1, tm), lambda i, f, te, tv: (jnp.minimum(i + 1, n_tiles - 1), 0, 0),
                         memory_space=pltpu.SMEM),
            pl.BlockSpec(memory_space=pl.ANY),
            pl.BlockSpec((None, D_MODEL, tf), lambda i, f, te, tv: (te[i], 0, f)),
            pl.BlockSpec((None, D_MODEL, tf), lambda i, f, te, tv: (te[i], 0, f)),
            pl.BlockSpec((None, tf, D_MODEL), lambda i, f, te, tv: (te[i], f, 0)),
        ],
        out_specs=pl.BlockSpec((tm, D_MODEL), lambda i, f, te, tv: (i, 0)),
        scratch_shapes=[pltpu.VMEM((2, tm * SUBLANES, LANES), jnp.float32), pltpu.VMEM((tm, D_MODEL), MXU_DTYPE),
                        pltpu.VMEM((tm, D_MODEL), jnp.float32), pltpu.SemaphoreType.DMA((2,))],
    )
    return pl.pallas_call(
        _moe_ffn_kernel, grid_spec=grid_spec,
        out_shape=jax.ShapeDtypeStruct((n_tiles * tm, D_MODEL), MXU_DTYPE),
        compiler_params=_cparams(("arbitrary", "arbitrary")), name="moe_ffn",
    )(tile_expert, tile_valid, tok3, tok3, x_tiles, wg, wu, wd)


def _moe_combine_kernel(a_ref, nb_ref, x1_ref, route_ref, y_hbm, g2_ref, b2_ref, out_ref, ybuf_ref, sem):
    i = pl.program_id(0)

    @pl.when(i == 0)
    def _():
        ybuf_ref[...] = jnp.zeros_like(ybuf_ref)

    def block_copy(src, dst):
        return pltpu.make_async_copy(y_hbm.at[pl.ds(src, CMB_BLK), :], ybuf_ref.at[pl.ds(dst, CMB_BLK), :], sem)

    off = jnp.int32(0)
    for e in range(N_EXPERTS):
        a = a_ref[i * N_EXPERTS + e]
        nb = nb_ref[i * N_EXPERTS + e]

        def start(k, c, a=a, off=off):
            block_copy(pl.multiple_of(a + k * CMB_BLK, CMB_BLK), pl.multiple_of(off + k * CMB_BLK, CMB_BLK)).start()
            return c

        lax.fori_loop(0, nb, start, 0)
        off = off + nb * CMB_BLK

    def wait(k, c):
        block_copy(0, 0).wait()
        return c

    lax.fori_loop(0, off // CMB_BLK, wait, 0)

    route = route_ref[...]
    col = lax.broadcasted_iota(jnp.int32, (route.shape[0], ybuf_ref.shape[0]), 1)
    sel = (jnp.where(col == route[:, 4:5].astype(jnp.int32), route[:, 2:3], 0.0)
           + jnp.where(col == route[:, 5:6].astype(jnp.int32), route[:, 3:4], 0.0))
    f = jnp.dot(sel.astype(ybuf_ref.dtype), ybuf_ref[...], preferred_element_type=jnp.float32)
    out_ref[...] = _layer_norm(DN_ALPHA * x1_ref[...] + f, g2_ref[...], b2_ref[...])


def moe_combine(blk_start, blk_count, x1, route, y_sorted, g2, b2):
    n = x1.shape[0]
    tm = min(TM_CMB, n)
    buf_rows = -(-(2 * tm + N_EXPERTS * 2 * (CMB_BLK - 1)) // 256) * 256
    row = lambda i, a, nb: (i, 0)
    fixed = lambda i, a, nb: (0, 0)
    grid_spec = pltpu.PrefetchScalarGridSpec(
        num_scalar_prefetch=2, grid=(n // tm,),
        in_specs=[pl.BlockSpec((tm, D_MODEL), row), pl.BlockSpec((tm, LANES), row),
                  pl.BlockSpec(memory_space=pl.ANY),
                  pl.BlockSpec((1, D_MODEL), fixed), pl.BlockSpec((1, D_MODEL), fixed)],
        out_specs=pl.BlockSpec((tm, D_MODEL), row),
        scratch_shapes=[pltpu.VMEM((buf_rows, D_MODEL), y_sorted.dtype), pltpu.SemaphoreType.DMA(())],
    )
    return pl.pallas_call(
        _moe_combine_kernel, grid_spec=grid_spec,
        out_shape=jax.ShapeDtypeStruct((n, D_MODEL), jnp.float32),
        compiler_params=_cparams(("arbitrary",)), name="moe_combine",
    )(blk_start, blk_count, x1, route, y_sorted, g2, b2)


def _dispatch_plan(expert_idx, tm, tm_cmb):
    n = expert_idx.shape[0]
    e_flat = expert_idx.reshape(-1)
    onehot = (e_flat[:, None] == jnp.arange(N_EXPERTS, dtype=jnp.int32)[None, :]).astype(jnp.int32)
    csum = jnp.cumsum(onehot, axis=0)
    before = csum - onehot
    rank = jnp.sum(before * onehot, axis=1)
    counts = csum[-1]
    tiles_per = (counts + tm - 1) // tm
    tile_end = jnp.cumsum(tiles_per)
    group_start = (tile_end - tiles_per) * tm
    pos_flat = group_start[e_flat] + rank
    n_tiles = (2 * n) // tm + N_EXPERTS
    row_token = jnp.zeros((n_tiles * tm,), jnp.int32).at[pos_flat].set(
        jnp.arange(2 * n, dtype=jnp.int32) // 2)
    t = jnp.arange(n_tiles, dtype=jnp.int32)
    tile_valid = (t < tile_end[-1]).astype(jnp.int32)
    last_valid = jnp.maximum(tile_end[-1] - 1, 0)
    t_eff = jnp.minimum(t, last_valid)
    tile_expert = jnp.minimum(jnp.sum((t_eff[:, None] >= tile_end[None, :]).astype(jnp.int32), axis=1),
                              N_EXPERTS - 1).astype(jnp.int32)

    first = before[::2 * tm_cmb]
    cnt = jnp.concatenate([first[1:], counts[None, :]], axis=0) - first
    start = group_start[None, :] + first
    blk_start = (start // CMB_BLK) * CMB_BLK
    blk_count = jnp.where(cnt > 0, (start - blk_start + cnt + CMB_BLK - 1) // CMB_BLK, 0)
    buf_off = (jnp.cumsum(blk_count, axis=1) - blk_count) * CMB_BLK
    tile_of_pair = jnp.arange(2 * n, dtype=jnp.int32) // (2 * tm_cmb)
    local = pos_flat - blk_start[tile_of_pair, e_flat] + buf_off[tile_of_pair, e_flat]
    return (row_token, tile_expert, tile_valid, blk_start.reshape(-1).astype(jnp.int32),
            blk_count.reshape(-1).astype(jnp.int32), local.reshape(n, 2))


def _pad_w_in(w):
    z64 = jnp.zeros((D_MODEL, KR_X1), w.dtype)
    z32 = jnp.zeros((D_MODEL, LANES - KR_X1 - MLA_ROPE), w.dtype)
    return jnp.concatenate([w[:, 0:768], w[:, 1152:1408], w[:, 1440:1696], w[:, 1696:1952],
                            w[:, 768:1152], z64, w[:, 1408:1440], z32], axis=1).astype(MXU_DTYPE)


def _rope_tables(s):
    pos = jnp.arange(s, dtype=jnp.float32)
    inv = ROPE_THETA ** (-jnp.arange(0, MLA_ROPE, 2, dtype=jnp.float32) / MLA_ROPE)
    ang = pos[:, None] * inv[None, :]
    cos, sin = jnp.cos(ang), jnp.sin(ang)
    zl = jnp.zeros((s, KR_X1), jnp.float32)
    zh = jnp.zeros((s, HALF_ROPE), jnp.float32)
    zr = jnp.zeros((s, LANES - KR_X1 - MLA_ROPE), jnp.float32)
    kc = jnp.concatenate([zl, cos, cos, zr], axis=1)
    ksa = jnp.concatenate([zl, -sin, zh, zr], axis=1)
    ksb = jnp.concatenate([zl, zh, sin, zr], axis=1)
    return cos.T, sin.T, kc, ksa, ksb


def _block_diag(w):
    nb, bw, _ = w.shape
    out = jnp.zeros((nb * bw, nb * bw), w.dtype)
    for i in range(nb):
        out = out.at[i * bw:(i + 1) * bw, i * bw:(i + 1) * bw].set(w[i])
    return out


def kernel(x, ln_in_g, ln_in_b, w_in, conv_w, q_norm_g, w_uq, kv_norm_g, w_ukv, lru_conv_w, lru_conv_b,
           lru_wa, lru_ba, lru_wi, lru_bi, lru_lam, mix_norm_g, w_out, ln1_g, ln1_b, dense_w_gate,
           dense_w_up, dense_w_down, moe_w_router, moe_w_gate, moe_w_up, moe_w_down, ln2_g, ln2_b):
    b, s, d = x.shape
    n = b * s
    rope = _rope_tables(s)
    vec = lambda v: v.reshape(1, -1)
    cur = x.reshape(n, d)
    for l in range(DEPTH):
        w_in_pad = _pad_w_in(w_in[l])
        wq_t = jnp.pad(w_uq[l].T.reshape(MLA_HEADS, MLA_NOPE + MLA_ROPE, MLA_Q_RANK),
                       ((0, 0), (0, HEAD_PAD - MLA_NOPE - MLA_ROPE), (0, 0))
                       ).reshape(MLA_HEADS * HEAD_PAD, MLA_Q_RANK).astype(MXU_DTYPE)
        wkv = w_ukv[l].reshape(MLA_KV_RANK, MLA_HEADS, MLA_NOPE + MLA_V)
        wk_pad = jnp.pad(wkv[:, :, :MLA_NOPE], ((0, 0), (0, 0), (0, HEAD_PAD - MLA_NOPE))
                         ).reshape(MLA_KV_RANK, MLA_HEADS * HEAD_PAD).astype(MXU_DTYPE)
        wv_t = wkv[:, :, MLA_NOPE:].reshape(MLA_KV_RANK, MLA_HEADS * MLA_V).T.astype(MXU_DTYPE)
        gate_w = [jnp.concatenate([_block_diag(lru_wa[l, dr]), _block_diag(lru_wi[l, dr])], axis=1
                                  ).astype(MXU_DTYPE) for dr in range(2)]
        gate_b = [jnp.concatenate([lru_ba[l, dr], lru_bi[l, dr]]).reshape(1, -1) for dr in range(2)]

        if l == 0:
            xn, z = in_proj(cur, w_in_pad, ln=(vec(ln_in_g), vec(ln_in_b)))
        else:
            xn, z = in_proj(cur, w_in_pad)
        z3 = z.reshape(b, s, D_IN_PAD)
        q_t, k, v_t = mla_proj(z3, vec(q_norm_g[l]), vec(kv_norm_g[l]), wq_t, wk_pad, wv_t, rope)
        o_t = attention(q_t, k, v_t)
        o = o_t.transpose(0, 3, 1, 2).reshape(n, MLA_HEADS * MLA_V)
        h_f, y_conv = lru_scan(z3, lru_conv_w[l], vec(lru_conv_b[l]), gate_w[0], gate_b[0],
                               vec(lru_lam[l, 0]), short_w=conv_w[l])
        h_b = lru_scan(z3, lru_conv_w[l], vec(lru_conv_b[l]), gate_w[1], gate_b[1],
                       vec(lru_lam[l, 1]), reverse=True)
        mixer_args = (y_conv.reshape(n, CONV_DIM), o, h_f.reshape(n, LRU_DIM), h_b.reshape(n, LRU_DIM),
                      z, xn, vec(mix_norm_g[l]), w_out[l].astype(MXU_DTYPE), vec(ln1_g[l]), vec(ln1_b[l]))

        j = l // 2
        if l % 2 == 0:
            x1, x1b = post_mixer(*mixer_args)
            cur = ffn_dense(x1b, x1, dense_w_gate[j].astype(MXU_DTYPE), dense_w_up[j].astype(MXU_DTYPE),
                            dense_w_down[j].astype(MXU_DTYPE), vec(ln2_g[l]), vec(ln2_b[l]))
        else:
            w_router = jnp.pad(moe_w_router[j], ((0, 0), (0, LANES - N_EXPERTS)))
            x1, route = post_mixer(*mixer_args, w_router=w_router)
            expert_idx = route[:, 0:2].astype(jnp.int32)
            row_token, tile_expert, tile_valid, blk_start, blk_count, local = _dispatch_plan(
                expert_idx, min(TM_FFN, n), min(TM_CMB, n))
            y_sorted = moe_ffn(x1.reshape(n, SUBLANES, LANES), row_token, tile_expert, tile_valid,
                               moe_w_gate[j].astype(MXU_DTYPE), moe_w_up[j].astype(MXU_DTYPE),
                               moe_w_down[j].astype(MXU_DTYPE))
            route = jnp.concatenate([route[:, :4], local.astype(jnp.float32), route[:, 6:]], axis=1)
            cur = moe_combine(blk_start, blk_count, x1, route, y_sorted, vec(ln2_g[l]), vec(ln2_b[l]))
    return cur.reshape(b, s, d)
```

```python
import functools
import math

import jax
import jax.numpy as jnp
from jax import lax
from jax.experimental import pallas as pl
from jax.experimental.pallas import tpu as pltpu

D_MODEL = 1024
DEPTH = 2
CONV_DIM = 256
MLA_HEADS = 8
MLA_NOPE = 64
MLA_ROPE = 32
MLA_V = 64
MLA_Q_RANK = 384
MLA_KV_RANK = 256
LRU_DIM = 256
LRU_C = 8.0
ROPE_THETA = 10000.0
D_FF = 3584
N_EXPERTS = 8
DN_ALPHA = (2.0 * DEPTH) ** 0.25
LN_EPS = 1e-5
RMS_EPS = 1e-6

D_IN_PAD = 2048
HALF_ROPE = MLA_ROPE // 2
HEAD_PAD = 128
COL_CB, COL_CC, COL_CH, COL_CKV, COL_LG, COL_LX = 0, 1, 2, 3, 4, 5
COL_CQ = 4
COL_KR = 15
KR_X1 = 64
V_ROWS = MLA_V + 16

LANES = 128
SUBLANES = 8
VMEM_LIMIT = 56 * 1024 * 1024
MXU_DTYPE = jnp.bfloat16

TM = 512
TS = 512
TQ = 512
ATTN_GROUP = 4
ATTN_HEADS = 2
TM_FFN = 1024
TF = 512
TM_CMB = 256
CMB_BLK = 16
NEG_BIG = -1e30
LOG2E = 1.4426950408889634


def _cparams(sem, vmem=VMEM_LIMIT, flags=None):
    return pltpu.CompilerParams(dimension_semantics=sem, vmem_limit_bytes=vmem, flags=flags)


def _layer_norm(x, g, b):
    mu = jnp.mean(x, axis=-1, keepdims=True)
    xc = x - mu
    var = jnp.mean(xc * xc, axis=-1, keepdims=True)
    return xc * lax.rsqrt(var + LN_EPS) * g + b


def _rms_norm(x, g):
    ms = jnp.mean(x * x, axis=-1, keepdims=True)
    return x * lax.rsqrt(ms + RMS_EPS) * g


def _dot(a, b):
    return jnp.dot(a.astype(MXU_DTYPE), b.astype(MXU_DTYPE), preferred_element_type=jnp.float32)


def _dot_nt(a, b):
    return lax.dot_general(a.astype(MXU_DTYPE), b.astype(MXU_DTYPE), (((1,), (1,)), ((), ())),
                           preferred_element_type=jnp.float32)


def _in_proj_ln_kernel(x_ref, g_ref, b_ref, w_ref, xn_ref, z_ref):
    xn = _layer_norm(x_ref[...], g_ref[...], b_ref[...])
    xn_ref[...] = xn
    z_ref[...] = _dot(xn, w_ref[...])


def _in_proj_kernel(x_ref, w_ref, z_ref):
    z_ref[...] = _dot(x_ref[...], w_ref[...])


def in_proj(x, w_pad, ln=None):
    n = x.shape[0]
    tm = min(TM, n)
    grid = (n // tm,)
    row = lambda i: (i, 0)
    fixed = lambda i: (0, 0)
    x_spec = pl.BlockSpec((tm, D_MODEL), row)
    w_spec = pl.BlockSpec((D_MODEL, D_IN_PAD), fixed)
    z_spec = pl.BlockSpec((tm, D_IN_PAD), row)
    z_shape = jax.ShapeDtypeStruct((n, D_IN_PAD), jnp.float32)
    if ln is None:
        z = pl.pallas_call(
            _in_proj_kernel, grid=grid, in_specs=[x_spec, w_spec], out_specs=z_spec, out_shape=z_shape,
            compiler_params=_cparams(("parallel",)), name="in_proj")(x, w_pad)
        return x, z
    g, b = ln
    vec = pl.BlockSpec((1, D_MODEL), fixed)
    xn, z = pl.pallas_call(
        _in_proj_ln_kernel, grid=grid, in_specs=[x_spec, vec, vec, w_spec],
        out_specs=[x_spec, z_spec],
        out_shape=[jax.ShapeDtypeStruct((n, D_MODEL), jnp.float32), z_shape],
        compiler_params=_cparams(("parallel",)), name="in_proj_ln")(x, g, b, w_pad)
    return xn, z


def _mla_proj_kernel(ckv_ref, cq_ref, kr_ref, gq_ref, gkv_ref, wq_ref, wk_ref, wv_ref,
                     cos_ref, sin_ref, kc_ref, ksa_ref, ksb_ref, q_out, k_out, v_out):
    cqn = _rms_norm(cq_ref[0], gq_ref[...])
    ckvn = _rms_norm(ckv_ref[0], gkv_ref[...])
    kr = kr_ref[0]

    krope = (kr * kc_ref[...]
             + pltpu.roll(kr, LANES - HALF_ROPE, axis=1) * ksa_ref[...]
             + pltpu.roll(kr, HALF_ROPE, axis=1) * ksb_ref[...])
    k_all = _dot(ckvn, wk_ref[...])
    for h in range(MLA_HEADS):
        k_out[0, h] = (k_all[:, h * HEAD_PAD:(h + 1) * HEAD_PAD] + krope).astype(k_out.dtype)

    v_all = _dot_nt(wv_ref[...], ckvn)
    ones = jnp.ones((V_ROWS - MLA_V, v_all.shape[1]), jnp.float32)
    for h in range(MLA_HEADS):
        v_out[0, h, 0] = jnp.concatenate([v_all[h * MLA_V:(h + 1) * MLA_V], ones], axis=0).astype(v_out.dtype)

    q_all = _dot_nt(wq_ref[...], cqn)
    cos_t = cos_ref[...]
    sin_t = sin_ref[...]
    qscale = (MLA_NOPE + MLA_ROPE) ** -0.5 * LOG2E
    for h in range(MLA_HEADS):
        base = h * HEAD_PAD
        nope = q_all[base:base + MLA_NOPE]
        x1 = q_all[base + MLA_NOPE:base + MLA_NOPE + HALF_ROPE]
        x2 = q_all[base + MLA_NOPE + HALF_ROPE:base + MLA_NOPE + MLA_ROPE]
        zero = q_all[base + MLA_NOPE + MLA_ROPE:base + HEAD_PAD]
        qh = jnp.concatenate([nope, x1 * cos_t - x2 * sin_t, x2 * cos_t + x1 * sin_t, zero], axis=0)
        q_out[0, h] = (qh * qscale).astype(q_out.dtype)


def mla_proj(z3, gq, gkv, wq_t, wk_pad, wv_t, rope):
    b, s, _ = z3.shape
    ts = min(TS, s)
    nt = s // ts
    cos_t, sin_t, kc, ksa, ksb = rope
    fixed = lambda bi, i: (0, 0)
    in_specs = [
        pl.BlockSpec((1, ts, MLA_KV_RANK), lambda bi, i: (bi, i, COL_CKV)),
        pl.BlockSpec((1, ts, MLA_Q_RANK), lambda bi, i: (bi, i, COL_CQ)),
        pl.BlockSpec((1, ts, LANES), lambda bi, i: (bi, i, COL_KR)),
        pl.BlockSpec((1, MLA_Q_RANK), fixed),
        pl.BlockSpec((1, MLA_KV_RANK), fixed),
        pl.BlockSpec((MLA_HEADS * HEAD_PAD, MLA_Q_RANK), fixed),
        pl.BlockSpec((MLA_KV_RANK, MLA_HEADS * HEAD_PAD), fixed),
        pl.BlockSpec((MLA_HEADS * MLA_V, MLA_KV_RANK), fixed),
        pl.BlockSpec((HALF_ROPE, ts), lambda bi, i: (0, i)),
        pl.BlockSpec((HALF_ROPE, ts), lambda bi, i: (0, i)),
        pl.BlockSpec((ts, LANES), lambda bi, i: (i, 0)),
        pl.BlockSpec((ts, LANES), lambda bi, i: (i, 0)),
        pl.BlockSpec((ts, LANES), lambda bi, i: (i, 0)),
    ]
    out_specs = [
        pl.BlockSpec((1, MLA_HEADS, HEAD_PAD, ts), lambda bi, i: (bi, 0, 0, i)),
        pl.BlockSpec((1, MLA_HEADS, ts, HEAD_PAD), lambda bi, i: (bi, 0, i, 0)),
        pl.BlockSpec((1, MLA_HEADS, 1, V_ROWS, ts), lambda bi, i: (bi, 0, i, 0, 0)),
    ]
    out_shape = [
        jax.ShapeDtypeStruct((b, MLA_HEADS, HEAD_PAD, s), MXU_DTYPE),
        jax.ShapeDtypeStruct((b, MLA_HEADS, s, HEAD_PAD), MXU_DTYPE),
        jax.ShapeDtypeStruct((b, MLA_HEADS, nt, V_ROWS, ts), MXU_DTYPE),
    ]
    return pl.pallas_call(
        _mla_proj_kernel, grid=(b, nt), in_specs=in_specs, out_specs=out_specs, out_shape=out_shape,
        compiler_params=_cparams(("parallel", "parallel")), name="mla_proj",
    )(z3, z3, z3, gq, gkv, wq_t, wk_pad, wv_t, cos_t, sin_t, kc, ksa, ksb)


def _attn_kernel(k_ref, q_ref, v_ref, o_ref, *scratch, n_chunks, tkc, group_size):
    n_heads = q_ref.shape[1]
    tq = q_ref.shape[3]
    s_bufs = [scratch[4 * hd:4 * hd + 2] for hd in range(n_heads)]
    p_bufs = [scratch[4 * hd + 2:4 * hd + 4] for hd in range(n_heads)]

    def scores(hd, c, s_ref):
        start = pl.multiple_of(c * tkc, tkc)
        s = jnp.dot(k_ref[0, hd, pl.ds(start, tkc), :], q_ref[0, hd], preferred_element_type=jnp.float32)
        s_ref[...] = s
        return jnp.max(s, axis=0, keepdims=True)

    def accumulate(hd, c, s_ref, p_ref, mx, m, acc):
        m_new = jnp.maximum(m, mx)
        alpha = jnp.exp2(m - m_new)
        p_ref[...] = jnp.exp2(s_ref[...] - m_new).astype(p_ref.dtype)
        pv = jnp.dot(v_ref[0, hd, c], p_ref[...], preferred_element_type=jnp.float32)
        return m_new, alpha * acc + pv

    def group(c0, state, prefetch_last):
        state = list(state)
        for g in range(group_size):
            for hd in range(n_heads):
                mx, m, acc = state[hd]
                mx_next = None
                if g + 1 < group_size or prefetch_last:
                    mx_next = scores(hd, c0 + g + 1, s_bufs[hd][(g + 1) % 2])
                m, acc = accumulate(hd, c0 + g, s_bufs[hd][g % 2], p_bufs[hd][g % 2], mx, m, acc)
                state[hd] = (mx_next, m, acc)
        return tuple(state)

    def body(j, state):
        return group(j * group_size, state, prefetch_last=True)

    state = tuple((scores(hd, 0, s_bufs[hd][0]), jnp.full((1, tq), NEG_BIG, jnp.float32),
                   jnp.zeros((v_ref.shape[3], tq), jnp.float32)) for hd in range(n_heads))
    n_groups = n_chunks // group_size
    state = lax.fori_loop(0, n_groups - 1, body, state)
    state = group((n_groups - 1) * group_size, state, prefetch_last=False)
    o_t = jnp.concatenate([acc[:MLA_V] / acc[MLA_V:MLA_V + 1] for _, _, acc in state], axis=0)
    o_ref[0] = o_t.T


def attention(q_t, k, v_t):
    b, h, _, s = q_t.shape
    nc, tkc = v_t.shape[2], v_t.shape[4]
    tq = min(TQ, s)
    nh = ATTN_HEADS
    group_size = min(ATTN_GROUP, nc)
    assert group_size % 2 == 0 and nc % group_size == 0, "chunk groups alternate two buffers"
    kern = functools.partial(_attn_kernel, n_chunks=nc, tkc=tkc, group_size=group_size)
    per_head = [pltpu.VMEM((tkc, tq), jnp.float32), pltpu.VMEM((tkc, tq), jnp.float32),
                pltpu.VMEM((tkc, tq), MXU_DTYPE), pltpu.VMEM((tkc, tq), MXU_DTYPE)]
    return pl.pallas_call(
        kern, grid=(b, h // nh, s // tq),
        in_specs=[
            pl.BlockSpec((1, nh, s, HEAD_PAD), lambda bi, hi, qi: (bi, hi, 0, 0)),
            pl.BlockSpec((1, nh, HEAD_PAD, tq), lambda bi, hi, qi: (bi, hi, 0, qi)),
            pl.BlockSpec((1, nh, nc, V_ROWS, tkc), lambda bi, hi, qi: (bi, hi, 0, 0, 0)),
        ],
        out_specs=pl.BlockSpec((1, tq, nh * MLA_V), lambda bi, hi, qi: (bi, qi, hi)),
        out_shape=jax.ShapeDtypeStruct((b, s, h * MLA_V), jnp.float32),
        scratch_shapes=per_head * nh,
        compiler_params=_cparams(("parallel", "parallel", "parallel")), name="attention",
    )(k, q_t, v_t)


def _shift_rows(x, d, edge_rows, row):
    ts = x.shape[0]
    y = pltpu.roll(x, (-d) % ts, axis=0)
    if d < 0:
        return jnp.where(row == 0, edge_rows[0], y)
    for j in range(d):
        y = jnp.where(row == ts - d + j, edge_rows[j], y)
    return y


def _scan_rows(a, u, reverse):
    ts = a.shape[0]
    row = lax.broadcasted_iota(jnp.int32, a.shape, 0)
    d = 1
    while d < ts:
        if reverse:
            valid = row < ts - d
            shift = ts - d
        else:
            valid = row >= d
            shift = d
        a_sh = jnp.where(valid, pltpu.roll(a, shift, axis=0), 1.0)
        u_sh = jnp.where(valid, pltpu.roll(u, shift, axis=0), 0.0)
        u = u + a * u_sh
        a = a * a_sh
        d *= 2
    return a, u


def _lru_core(x, xp, xn, first, last, cw_ref, cb_ref, wg_ref, bg_ref, lam_ref, carry_ref, h_ref, reverse):
    ts = x.shape[0]
    row = lax.broadcasted_iota(jnp.int32, x.shape, 0)
    keep_prev = jnp.where(first, 0.0, 1.0)
    keep_next = jnp.where(last, 0.0, 1.0)
    prev_row = xp[SUBLANES - 1:SUBLANES] * keep_prev
    next0 = xn[0:1] * keep_next
    next1 = xn[1:2] * keep_next
    cw = cw_ref[...]
    xc = (cw[0:1] * _shift_rows(x, -1, [prev_row], row) + cw[1:2] * x
          + cw[2:3] * _shift_rows(x, 1, [next0], row)
          + cw[3:4] * _shift_rows(x, 2, [next0, next1], row) + cb_ref[...])
    gates = _dot(xc, wg_ref[...]) + bg_ref[...]
    rec = jax.nn.sigmoid(gates[:, :LRU_DIM])
    inp = jax.nn.sigmoid(gates[:, LRU_DIM:])
    neg_lam = -lam_ref[...]
    softplus = jnp.maximum(neg_lam, 0.0) + jnp.log(1.0 + jnp.exp(-jnp.abs(neg_lam)))
    log_a = -LRU_C * rec * softplus
    a = jnp.exp(log_a)
    u = jnp.sqrt(1.0 - a * a) * (inp * xc)
    a_cum, h0 = _scan_rows(a, u, reverse)

    @pl.when(pl.program_id(1) == 0)
    def _():
        carry_ref[...] = jnp.zeros_like(carry_ref)

    h = h0 + a_cum * carry_ref[0:1]
    h_ref[0] = h
    edge = h[0:1] if reverse else h[ts - 1:ts]
    carry_ref[...] = jnp.broadcast_to(edge, carry_ref.shape)


def _lru_fwd_kernel(x_ref, xp_ref, xn_ref, cc_ref, ccp_ref, ccn_ref, ch_ref, chp_ref, chn_ref, cbg_ref,
                    cw_ref, cb_ref, wg_ref, bg_ref, lam_ref, sw_ref, h_ref, y_ref, carry_ref):
    i = pl.program_id(1)
    first = i == 0
    last = i == pl.num_programs(1) - 1
    _lru_core(x_ref[0], xp_ref[0], xn_ref[0], first, last, cw_ref, cb_ref, wg_ref, bg_ref, lam_ref,
              carry_ref, h_ref, reverse=False)
    g = cc_ref[0] * ch_ref[0]
    row = lax.broadcasted_iota(jnp.int32, g.shape, 0)
    keep_prev = jnp.where(first, 0.0, 1.0)
    keep_next = jnp.where(last, 0.0, 1.0)
    g_prev = ccp_ref[0, SUBLANES - 1:SUBLANES] * chp_ref[0, SUBLANES - 1:SUBLANES] * keep_prev
    g_next = ccn_ref[0, 0:1] * chn_ref[0, 0:1] * keep_next
    sw = sw_ref[...]
    conv = (sw[0:1] * _shift_rows(g, -1, [g_prev], row) + sw[1:2] * g
            + sw[2:3] * _shift_rows(g, 1, [g_next], row))
    y_ref[0] = cbg_ref[0] * conv


def _lru_bwd_kernel(x_ref, xp_ref, xn_ref, cw_ref, cb_ref, wg_ref, bg_ref, lam_ref, h_ref, carry_ref):
    i = pl.program_id(1)
    nt = pl.num_programs(1)
    first = i == nt - 1
    last = i == 0
    _lru_core(x_ref[0], xp_ref[0], xn_ref[0], first, last, cw_ref, cb_ref, wg_ref, bg_ref, lam_ref,
              carry_ref, h_ref, reverse=True)


def lru_scan(z3, conv_w, conv_b, wg, bg, lam, short_w=None, reverse=False):
    b, s, _ = z3.shape
    ts = min(TS, s)
    nt = s // ts
    rb = ts // SUBLANES
    nrb = s // SUBLANES
    tile = (lambda i: nt - 1 - i) if reverse else (lambda i: i)

    def main(col):
        return pl.BlockSpec((1, ts, LRU_DIM), lambda bi, i: (bi, tile(i), col))

    def prev(col):
        return pl.BlockSpec((1, SUBLANES, LRU_DIM),
                            lambda bi, i: (bi, jnp.maximum(tile(i) * rb - 1, 0), col))

    def nxt(col):
        return pl.BlockSpec((1, SUBLANES, LRU_DIM),
                            lambda bi, i: (bi, jnp.minimum((tile(i) + 1) * rb, nrb - 1), col))

    fixed = lambda bi, i: (0, 0)
    par = [pl.BlockSpec((4, LRU_DIM), fixed), pl.BlockSpec((1, LRU_DIM), fixed),
           pl.BlockSpec((LRU_DIM, 2 * LRU_DIM), fixed), pl.BlockSpec((1, 2 * LRU_DIM), fixed),
           pl.BlockSpec((1, LRU_DIM), fixed)]
    h_shape = jax.ShapeDtypeStruct((b, s, LRU_DIM), jnp.float32)
    scratch = [pltpu.VMEM((SUBLANES, LRU_DIM), jnp.float32)]
    if reverse:
        return pl.pallas_call(
            _lru_bwd_kernel, grid=(b, nt),
            in_specs=[main(COL_LX), prev(COL_LX), nxt(COL_LX)] + par,
            out_specs=main(0), out_shape=h_shape, scratch_shapes=scratch,
            compiler_params=_cparams(("parallel", "arbitrary")), name="lru_bwd",
        )(z3, z3, z3, conv_w, conv_b, wg, bg, lam)
    return pl.pallas_call(
        _lru_fwd_kernel, grid=(b, nt),
        in_specs=[main(COL_LX), prev(COL_LX), nxt(COL_LX), main(COL_CC), prev(COL_CC), nxt(COL_CC),
                  main(COL_CH), prev(COL_CH), nxt(COL_CH), main(COL_CB)] + par
                 + [pl.BlockSpec((3, CONV_DIM), fixed)],
        out_specs=[main(0), main(0)], out_shape=[h_shape, h_shape], scratch_shapes=scratch,
        compiler_params=_cparams(("parallel", "arbitrary")), name="lru_fwd",
    )(z3, z3, z3, z3, z3, z3, z3, z3, z3, z3, conv_w, conv_b, wg, bg, lam, short_w)


def _post_mixer_body(yc_ref, o_ref, hf_ref, hb_ref, lg_ref, xn_ref, gm_ref, wo_ref, g1_ref, b1_ref):
    gm = gm_ref[...]
    y_lru = jax.nn.gelu(lg_ref[...], approximate=True) * (hf_ref[...] + hb_ref[...])
    y = jnp.concatenate([
        _rms_norm(yc_ref[...], gm[:, :CONV_DIM]),
        _rms_norm(o_ref[...], gm[:, CONV_DIM:CONV_DIM + MLA_HEADS * MLA_V]),
        _rms_norm(y_lru, gm[:, CONV_DIM + MLA_HEADS * MLA_V:]),
    ], axis=1)
    mix = _dot(y, wo_ref[...])
    return _layer_norm(DN_ALPHA * xn_ref[...] + mix, g1_ref[...], b1_ref[...])


def _post_mixer_kernel(yc_ref, o_ref, hf_ref, hb_ref, lg_ref, xn_ref, gm_ref, wo_ref, g1_ref, b1_ref,
                       x1_ref, x1b_ref):
    x1 = _post_mixer_body(yc_ref, o_ref, hf_ref, hb_ref, lg_ref, xn_ref, gm_ref, wo_ref, g1_ref, b1_ref)
    x1_ref[...] = x1
    x1b_ref[...] = x1.astype(x1b_ref.dtype)


def _post_mixer_router_kernel(yc_ref, o_ref, hf_ref, hb_ref, lg_ref, xn_ref, gm_ref, wo_ref, g1_ref,
                              b1_ref, wr_ref, x1_ref, route_ref):
    x1 = _post_mixer_body(yc_ref, o_ref, hf_ref, hb_ref, lg_ref, xn_ref, gm_ref, wo_ref, g1_ref, b1_ref)
    x1_ref[...] = x1
    wr = wr_ref[...]
    x_hi = x1.astype(jnp.bfloat16)
    x_lo = (x1 - x_hi.astype(jnp.float32)).astype(jnp.bfloat16)
    w_hi = wr.astype(jnp.bfloat16)
    w_lo = (wr - w_hi.astype(jnp.float32)).astype(jnp.bfloat16)
    logits = (jnp.dot(x_hi, w_hi, preferred_element_type=jnp.float32)
              + jnp.dot(x_lo, w_hi, preferred_element_type=jnp.float32)
              + jnp.dot(x_hi, w_lo, preferred_element_type=jnp.float32))
    lane = lax.broadcasted_iota(jnp.int32, logits.shape, 1)
    logits = jnp.where(lane < N_EXPERTS, logits, NEG_BIG)
    v1 = jnp.max(logits, axis=1, keepdims=True)
    i1 = jnp.min(jnp.where(logits == v1, lane, LANES), axis=1, keepdims=True)
    rest = jnp.where(lane == i1, NEG_BIG, logits)
    v2 = jnp.max(rest, axis=1, keepdims=True)
    i2 = jnp.min(jnp.where(rest == v2, lane, LANES), axis=1, keepdims=True)
    e = jnp.exp(v2 - v1)
    g_top = 1.0 / (1.0 + e)
    g_sec = e * g_top
    route_ref[...] = jnp.where(lane == 0, i1.astype(jnp.float32),
                               jnp.where(lane == 1, i2.astype(jnp.float32),
                                         jnp.where(lane == 2, g_top, jnp.where(lane == 3, g_sec, 0.0))))


def post_mixer(y_conv, o, h_f, h_b, z, xn, gm, wo, g1, b1, w_router=None):
    n = xn.shape[0]
    tm = min(TM, n)
    row = lambda i: (i, 0)
    fixed = lambda i: (0, 0)
    in_specs = [
        pl.BlockSpec((tm, CONV_DIM), row), pl.BlockSpec((tm, MLA_HEADS * MLA_V), row),
        pl.BlockSpec((tm, LRU_DIM), row), pl.BlockSpec((tm, LRU_DIM), row),
        pl.BlockSpec((tm, LRU_DIM), lambda i: (i, COL_LG)), pl.BlockSpec((tm, D_MODEL), row),
        pl.BlockSpec((1, D_MODEL), fixed), pl.BlockSpec((D_MODEL, D_MODEL), fixed),
        pl.BlockSpec((1, D_MODEL), fixed), pl.BlockSpec((1, D_MODEL), fixed),
    ]
    x_spec = pl.BlockSpec((tm, D_MODEL), row)
    x_shape = jax.ShapeDtypeStruct((n, D_MODEL), jnp.float32)
    args = (y_conv, o, h_f, h_b, z, xn, gm, wo, g1, b1)
    if w_router is None:
        return pl.pallas_call(
            _post_mixer_kernel, grid=(n // tm,), in_specs=in_specs, out_specs=[x_spec, x_spec],
            out_shape=[x_shape, jax.ShapeDtypeStruct((n, D_MODEL), MXU_DTYPE)],
            compiler_params=_cparams(("parallel",)), name="post_mixer")(*args)
    return pl.pallas_call(
        _post_mixer_router_kernel, grid=(n // tm,),
        in_specs=in_specs + [pl.BlockSpec((D_MODEL, LANES), fixed)],
        out_specs=[x_spec, pl.BlockSpec((tm, LANES), row)],
        out_shape=[x_shape, jax.ShapeDtypeStruct((n, LANES), jnp.float32)],
        compiler_params=_cparams(("parallel",)), name="post_mixer_router")(*args, w_router)


def _swiglu_chunk(xb, wg, wu, wd):
    gate = jnp.dot(xb, wg, preferred_element_type=jnp.float32)
    up = jnp.dot(xb, wu, preferred_element_type=jnp.float32)
    hidden = (jax.nn.silu(gate) * up).astype(wd.dtype)
    return jnp.dot(hidden, wd, preferred_element_type=jnp.float32)


def _ffn_dense_kernel(xb_ref, x1_ref, wg_ref, wu_ref, wd_ref, g2_ref, b2_ref, out_ref, acc_ref):
    f = pl.program_id(1)

    @pl.when(f == 0)
    def _():
        acc_ref[...] = jnp.zeros_like(acc_ref)

    acc_ref[...] += _swiglu_chunk(xb_ref[...], wg_ref[...], wu_ref[...], wd_ref[...])

    @pl.when(f == pl.num_programs(1) - 1)
    def _():
        out_ref[...] = _layer_norm(DN_ALPHA * x1_ref[...] + acc_ref[...], g2_ref[...], b2_ref[...])


def ffn_dense(x1b, x1, wg, wu, wd, g2, b2):
    n = x1.shape[0]
    tm = min(TM_FFN, n)
    tf = TF
    row = lambda i, f: (i, 0)
    fixed = lambda i, f: (0, 0)
    return pl.pallas_call(
        _ffn_dense_kernel, grid=(n // tm, D_FF // tf),
        in_specs=[pl.BlockSpec((tm, D_MODEL), row), pl.BlockSpec((tm, D_MODEL), row),
                  pl.BlockSpec((D_MODEL, tf), lambda i, f: (0, f)),
                  pl.BlockSpec((D_MODEL, tf), lambda i, f: (0, f)),
                  pl.BlockSpec((tf, D_MODEL), lambda i, f: (f, 0)),
                  pl.BlockSpec((1, D_MODEL), fixed), pl.BlockSpec((1, D_MODEL), fixed)],
        out_specs=pl.BlockSpec((tm, D_MODEL), row),
        out_shape=jax.ShapeDtypeStruct((n, D_MODEL), jnp.float32),
        scratch_shapes=[pltpu.VMEM((tm, D_MODEL), jnp.float32)],
        compiler_params=_cparams(("parallel", "arbitrary")), name="ffn_dense",
    )(x1b, x1, wg, wu, wd, g2, b2)


def _moe_ffn_kernel(te_ref, tv_ref, tok_ref, tok_next_ref, x_hbm, wg_ref, wu_ref, wd_ref, y_ref, xg_ref, xb_ref,
                    acc_ref, sem):
    i = pl.program_id(0)
    f = pl.program_id(1)
    valid = tv_ref[i] == 1
    tm = xb_ref.shape[0]
    slot = i % 2

    def tile_copy(tok, r, sl):
        dst = pl.multiple_of(r * SUBLANES, SUBLANES)
        return pltpu.make_async_copy(x_hbm.at[tok], xg_ref.at[sl, pl.ds(dst, SUBLANES), :], sem.at[sl])

    def start_gather(tokens_ref, sl):
        def start(r, c):
            tile_copy(tokens_ref[0, 0, r], r, sl).start()
            return c

        lax.fori_loop(0, tm, start, 0, unroll=8)

    @pl.when(jnp.logical_and(valid, f == 0))
    def _():
        @pl.when(i == 0)
        def _():
            start_gather(tok_ref, slot)

        def wait(r, c):
            tile_copy(0, r, slot).wait()
            return c

        lax.fori_loop(0, tm, wait, 0, unroll=8)
        for s in range(SUBLANES):
            xb_ref[:, s * LANES:(s + 1) * LANES] = (
                xg_ref[slot, pl.ds(s, tm, stride=SUBLANES), :].astype(xb_ref.dtype))
        acc_ref[...] = jnp.zeros_like(acc_ref)

        nxt = jnp.minimum(i + 1, pl.num_programs(0) - 1)

        @pl.when(jnp.logical_and(i + 1 < pl.num_programs(0), tv_ref[nxt] == 1))
        def _():
            start_gather(tok_next_ref, 1 - slot)

    @pl.when(valid)
    def _():
        acc_ref[...] += _swiglu_chunk(xb_ref[...], wg_ref[...], wu_ref[...], wd_ref[...])

    @pl.when(f == pl.num_programs(1) - 1)
    def _():
        y_ref[...] = jnp.where(valid, acc_ref[...], 0.0).astype(y_ref.dtype)


def moe_ffn(x_tiles, row_token, tile_expert, tile_valid, wg, wu, wd):
    n_tiles = tile_expert.shape[0]
    tm = row_token.shape[0] // n_tiles
    tf = TF
    tok3 = row_token.reshape(n_tiles, 1, tm)
    grid_spec = pltpu.PrefetchScalarGridSpec(
        num_scalar_prefetch=2, grid=(n_tiles, D_FF // tf),
        in_specs=[
            pl.BlockSpec((1, 1, tm), lambda i, f, te, tv: (i, 0, 0), memory_space=pltpu.SMEM),
            pl.BlockSpec((1, 1, tm), lambda i, f, te, tv: (jnp.minimum(i + 1, n_tiles - 1), 0, 0),
                         memory_space=pltpu.SMEM),
            pl.BlockSpec(memory_space=pl.ANY),
            pl.BlockSpec((None, D_MODEL, tf), lambda i, f, te, tv: (te[i], 0, f)),
            pl.BlockSpec((None, D_MODEL, tf), lambda i, f, te, tv: (te[i], 0, f)),
            pl.BlockSpec((None, tf, D_MODEL), lambda i, f, te, tv: (te[i], f, 0)),
        ],
        out_specs=pl.BlockSpec((tm, D_MODEL), lambda i, f, te, tv: (i, 0)),
        scratch_shapes=[pltpu.VMEM((2, tm * SUBLANES, LANES), jnp.float32), pltpu.VMEM((tm, D_MODEL), MXU_DTYPE),
                        pltpu.VMEM((tm, D_MODEL), jnp.float32), pltpu.SemaphoreType.DMA((2,))],
    )
    return pl.pallas_call(
        _moe_ffn_kernel, grid_spec=grid_spec,
        out_shape=jax.ShapeDtypeStruct((n_tiles * tm, D_MODEL), MXU_DTYPE),
        compiler_params=_cparams(("arbitrary", "arbitrary")), name="moe_ffn",
    )(tile_expert, tile_valid, tok3, tok3, x_tiles, wg, wu, wd)


def _moe_combine_kernel(a_ref, nb_ref, x1_ref, route_ref, y_hbm, g2_ref, b2_ref, out_ref, ybuf_ref, sem):
    i = pl.program_id(0)

    @pl.when(i == 0)
    def _():
        ybuf_ref[...] = jnp.zeros_like(ybuf_ref)

    def block_copy(src, dst):
        return pltpu.make_async_copy(y_hbm.at[pl.ds(src, CMB_BLK), :], ybuf_ref.at[pl.ds(dst, CMB_BLK), :], sem)

    off = jnp.int32(0)
    for e in range(N_EXPERTS):
        a = a_ref[i * N_EXPERTS + e]
        nb = nb_ref[i * N_EXPERTS + e]

        def start(k, c, a=a, off=off):
            block_copy(pl.multiple_of(a + k * CMB_BLK, CMB_BLK), pl.multiple_of(off + k * CMB_BLK, CMB_BLK)).start()
            return c

        lax.fori_loop(0, nb, start, 0)
        off = off + nb * CMB_BLK

    def wait(k, c):
        block_copy(0, 0).wait()
        return c

    lax.fori_loop(0, off // CMB_BLK, wait, 0)

    route = route_ref[...]
    col = lax.broadcasted_iota(jnp.int32, (route.shape[0], ybuf_ref.shape[0]), 1)
    sel = (jnp.where(col == route[:, 4:5].astype(jnp.int32), route[:, 2:3], 0.0)
           + jnp.where(col == route[:, 5:6].astype(jnp.int32), route[:, 3:4], 0.0))
    f = jnp.dot(sel.astype(ybuf_ref.dtype), ybuf_ref[...], preferred_element_type=jnp.float32)
    out_ref[...] = _layer_norm(DN_ALPHA * x1_ref[...] + f, g2_ref[...], b2_ref[...])


def moe_combine(blk_start, blk_count, x1, route, y_sorted, g2, b2):
    n = x1.shape[0]
    tm = min(TM_CMB, n)
    buf_rows = -(-(2 * tm + N_EXPERTS * 2 * (CMB_BLK - 1)) // 256) * 256
    row = lambda i, a, nb: (i, 0)
    fixed = lambda i, a, nb: (0, 0)
    grid_spec = pltpu.PrefetchScalarGridSpec(
        num_scalar_prefetch=2, grid=(n // tm,),
        in_specs=[pl.BlockSpec((tm, D_MODEL), row), pl.BlockSpec((tm, LANES), row),
                  pl.BlockSpec(memory_space=pl.ANY),
                  pl.BlockSpec((1, D_MODEL), fixed), pl.BlockSpec((1, D_MODEL), fixed)],
        out_specs=pl.BlockSpec((tm, D_MODEL), row),
        scratch_shapes=[pltpu.VMEM((buf_rows, D_MODEL), y_sorted.dtype), pltpu.SemaphoreType.DMA(())],
    )
    return pl.pallas_call(
        _moe_combine_kernel, grid_spec=grid_spec,
        out_shape=jax.ShapeDtypeStruct((n, D_MODEL), jnp.float32),
        compiler_params=_cparams(("arbitrary",)), name="moe_combine",
    )(blk_start, blk_count, x1, route, y_sorted, g2, b2)


def _dispatch_plan(expert_idx, tm, tm_cmb):
    n = expert_idx.shape[0]
    e_flat = expert_idx.reshape(-1)
    onehot = (e_flat[:, None] == jnp.arange(N_EXPERTS, dtype=jnp.int32)[None, :]).astype(jnp.int32)
    csum = jnp.cumsum(onehot, axis=0)
    before = csum - onehot
    rank = jnp.sum(before * onehot, axis=1)
    counts = csum[-1]
    tiles_per = (counts + tm - 1) // tm
    tile_end = jnp.cumsum(tiles_per)
    group_start = (tile_end - tiles_per) * tm
    pos_flat = jnp.sum(onehot * group_start[None, :], axis=1) + rank
    n_tiles = (2 * n) // tm + N_EXPERTS
    row_token = jnp.zeros((n_tiles * tm,), jnp.int32).at[pos_flat].set(
        jnp.arange(2 * n, dtype=jnp.int32) // 2)
    t = jnp.arange(n_tiles, dtype=jnp.int32)
    tile_valid = (t < tile_end[-1]).astype(jnp.int32)
    last_valid = jnp.maximum(tile_end[-1] - 1, 0)
    t_eff = jnp.minimum(t, last_valid)
    tile_expert = jnp.minimum(jnp.sum((t_eff[:, None] >= tile_end[None, :]).astype(jnp.int32), axis=1),
                              N_EXPERTS - 1).astype(jnp.int32)

    first = before[::2 * tm_cmb]
    cnt = jnp.concatenate([first[1:], counts[None, :]], axis=0) - first
    start = group_start[None, :] + first
    blk_start = (start // CMB_BLK) * CMB_BLK
    blk_count = jnp.where(cnt > 0, (start - blk_start + cnt + CMB_BLK - 1) // CMB_BLK, 0)
    buf_off = (jnp.cumsum(blk_count, axis=1) - blk_count) * CMB_BLK
    shift = (buf_off - blk_start)[:, None, :]
    local = pos_flat + jnp.sum(onehot.reshape(-1, 2 * tm_cmb, N_EXPERTS) * shift, axis=2).reshape(-1)
    return (row_token, tile_expert, tile_valid, blk_start.reshape(-1).astype(jnp.int32),
            blk_count.reshape(-1).astype(jnp.int32), local.reshape(n, 2))


def _pad_w_in(w):
    z64 = jnp.zeros((D_MODEL, KR_X1), w.dtype)
    z32 = jnp.zeros((D_MODEL, LANES - KR_X1 - MLA_ROPE), w.dtype)
    return jnp.concatenate([w[:, 0:768], w[:, 1152:1408], w[:, 1440:1696], w[:, 1696:1952],
                            w[:, 768:1152], z64, w[:, 1408:1440], z32], axis=1).astype(MXU_DTYPE)


def _rope_tables(s):
    pos = jnp.arange(s, dtype=jnp.float32)
    inv = ROPE_THETA ** (-jnp.arange(0, MLA_ROPE, 2, dtype=jnp.float32) / MLA_ROPE)
    ang = pos[:, None] * inv[None, :]
    cos, sin = jnp.cos(ang), jnp.sin(ang)
    zl = jnp.zeros((s, KR_X1), jnp.float32)
    zh = jnp.zeros((s, HALF_ROPE), jnp.float32)
    zr = jnp.zeros((s, LANES - KR_X1 - MLA_ROPE), jnp.float32)
    kc = jnp.concatenate([zl, cos, cos, zr], axis=1)
    ksa = jnp.concatenate([zl, -sin, zh, zr], axis=1)
    ksb = jnp.concatenate([zl, zh, sin, zr], axis=1)
    return cos.T, sin.T, kc, ksa, ksb


def _block_diag(w):
    nb, bw, _ = w.shape
    out = jnp.zeros((nb * bw, nb * bw), w.dtype)
    for i in range(nb):
        out = out.at[i * bw:(i + 1) * bw, i * bw:(i + 1) * bw].set(w[i])
    return out


def kernel(x, ln_in_g, ln_in_b, w_in, conv_w, q_norm_g, w_uq, kv_norm_g, w_ukv, lru_conv_w, lru_conv_b,
           lru_wa, lru_ba, lru_wi, lru_bi, lru_lam, mix_norm_g, w_out, ln1_g, ln1_b, dense_w_gate,
           dense_w_up, dense_w_down, moe_w_router, moe_w_gate, moe_w_up, moe_w_down, ln2_g, ln2_b):
    b, s, d = x.shape
    n = b * s
    rope = _rope_tables(s)
    vec = lambda v: v.reshape(1, -1)
    cur = x.reshape(n, d)
    for l in range(DEPTH):
        w_in_pad = _pad_w_in(w_in[l])
        wq_t = jnp.pad(w_uq[l].T.reshape(MLA_HEADS, MLA_NOPE + MLA_ROPE, MLA_Q_RANK),
                       ((0, 0), (0, HEAD_PAD - MLA_NOPE - MLA_ROPE), (0, 0))
                       ).reshape(MLA_HEADS * HEAD_PAD, MLA_Q_RANK).astype(MXU_DTYPE)
        wkv = w_ukv[l].reshape(MLA_KV_RANK, MLA_HEADS, MLA_NOPE + MLA_V)
        wk_pad = jnp.pad(wkv[:, :, :MLA_NOPE], ((0, 0), (0, 0), (0, HEAD_PAD - MLA_NOPE))
                         ).reshape(MLA_KV_RANK, MLA_HEADS * HEAD_PAD).astype(MXU_DTYPE)
        wv_t = wkv[:, :, MLA_NOPE:].reshape(MLA_KV_RANK, MLA_HEADS * MLA_V).T.astype(MXU_DTYPE)
        gate_w = [jnp.concatenate([_block_diag(lru_wa[l, dr]), _block_diag(lru_wi[l, dr])], axis=1
                                  ).astype(MXU_DTYPE) for dr in range(2)]
        gate_b = [jnp.concatenate([lru_ba[l, dr], lru_bi[l, dr]]).reshape(1, -1) for dr in range(2)]

        if l == 0:
            xn, z = in_proj(cur, w_in_pad, ln=(vec(ln_in_g), vec(ln_in_b)))
        else:
            xn, z = in_proj(cur, w_in_pad)
        z3 = z.reshape(b, s, D_IN_PAD)
        q_t, k, v_t = mla_proj(z3, vec(q_norm_g[l]), vec(kv_norm_g[l]), wq_t, wk_pad, wv_t, rope)
        o = attention(q_t, k, v_t).reshape(n, MLA_HEADS * MLA_V)
        h_f, y_conv = lru_scan(z3, lru_conv_w[l], vec(lru_conv_b[l]), gate_w[0], gate_b[0],
                               vec(lru_lam[l, 0]), short_w=conv_w[l])
        h_b = lru_scan(z3, lru_conv_w[l], vec(lru_conv_b[l]), gate_w[1], gate_b[1],
                       vec(lru_lam[l, 1]), reverse=True)
        mixer_args = (y_conv.reshape(n, CONV_DIM), o, h_f.reshape(n, LRU_DIM), h_b.reshape(n, LRU_DIM),
                      z, xn, vec(mix_norm_g[l]), w_out[l].astype(MXU_DTYPE), vec(ln1_g[l]), vec(ln1_b[l]))

        j = l // 2
        if l % 2 == 0:
            x1, x1b = post_mixer(*mixer_args)
            cur = ffn_dense(x1b, x1, dense_w_gate[j].astype(MXU_DTYPE), dense_w_up[j].astype(MXU_DTYPE),
                            dense_w_down[j].astype(MXU_DTYPE), vec(ln2_g[l]), vec(ln2_b[l]))
        else:
            w_router = jnp.pad(moe_w_router[j], ((0, 0), (0, LANES - N_EXPERTS)))
            x1, route = post_mixer(*mixer_args, w_router=w_router)
            expert_idx = route[:, 0:2].astype(jnp.int32)
            row_token, tile_expert, tile_valid, blk_start, blk_count, local = _dispatch_plan(
                expert_idx, min(TM_FFN, n), min(TM_CMB, n))
            y_sorted = moe_ffn(x1.reshape(n, SUBLANES, LANES), row_token, tile_expert, tile_valid,
                               moe_w_gate[j].astype(MXU_DTYPE), moe_w_up[j].astype(MXU_DTYPE),
                               moe_w_down[j].astype(MXU_DTYPE))
            route = jnp.concatenate([route[:, :4], local.astype(jnp.float32), route[:, 6:]], axis=1)
            cur = moe_combine(blk_start, blk_count, x1, route, y_sorted, vec(ln2_g[l]), vec(ln2_b[l]))
    return cur.reshape(b, s, d)
```

```python
import functools
import math

import jax
import jax.numpy as jnp
from jax import lax
from jax.experimental import pallas as pl
from jax.experimental.pallas import tpu as pltpu

D_MODEL = 1024
DEPTH = 2
CONV_DIM = 256
MLA_HEADS = 8
MLA_NOPE = 64
MLA_ROPE = 32
MLA_V = 64
MLA_Q_RANK = 384
MLA_KV_RANK = 256
LRU_DIM = 256
LRU_C = 8.0
ROPE_THETA = 10000.0
D_FF = 3584
N_EXPERTS = 8
DN_ALPHA = (2.0 * DEPTH) ** 0.25
LN_EPS = 1e-5
RMS_EPS = 1e-6

D_IN_PAD = 2048
HALF_ROPE = MLA_ROPE // 2
HEAD_PAD = 128
COL_CB, COL_CC, COL_CH, COL_CKV, COL_LG, COL_LX = 0, 1, 2, 3, 4, 5
COL_CQ = 4
COL_KR = 15
KR_X1 = 64
V_ROWS = MLA_V + 16

LANES = 128
SUBLANES = 8
VMEM_LIMIT = 56 * 1024 * 1024
MXU_DTYPE = jnp.bfloat16

TM = 512
TS = 512
TQ = 512
ATTN_GROUP = 4
ATTN_HEADS = 2
TM_FFN = 1024
TF = 512
TM_CMB = 256
CMB_BLK = 16
DISP_BLK = 8
NEG_BIG = -1e30
LOG2E = 1.4426950408889634


def _cparams(sem, vmem=VMEM_LIMIT, flags=None):
    return pltpu.CompilerParams(dimension_semantics=sem, vmem_limit_bytes=vmem, flags=flags)


def _layer_norm(x, g, b):
    mu = jnp.mean(x, axis=-1, keepdims=True)
    xc = x - mu
    var = jnp.mean(xc * xc, axis=-1, keepdims=True)
    return xc * lax.rsqrt(var + LN_EPS) * g + b


def _rms_norm(x, g):
    ms = jnp.mean(x * x, axis=-1, keepdims=True)
    return x * lax.rsqrt(ms + RMS_EPS) * g


def _dot(a, b):
    return jnp.dot(a.astype(MXU_DTYPE), b.astype(MXU_DTYPE), preferred_element_type=jnp.float32)


def _dot_nt(a, b):
    return lax.dot_general(a.astype(MXU_DTYPE), b.astype(MXU_DTYPE), (((1,), (1,)), ((), ())),
                           preferred_element_type=jnp.float32)


def _in_proj_ln_kernel(x_ref, g_ref, b_ref, w_ref, xn_ref, z_ref):
    xn = _layer_norm(x_ref[...], g_ref[...], b_ref[...])
    xn_ref[...] = xn
    z_ref[...] = _dot(xn, w_ref[...])


def _in_proj_kernel(x_ref, w_ref, z_ref):
    z_ref[...] = _dot(x_ref[...], w_ref[...])


def in_proj(x, w_pad, ln=None):
    n = x.shape[0]
    tm = min(TM, n)
    grid = (n // tm,)
    row = lambda i: (i, 0)
    fixed = lambda i: (0, 0)
    x_spec = pl.BlockSpec((tm, D_MODEL), row)
    w_spec = pl.BlockSpec((D_MODEL, D_IN_PAD), fixed)
    z_spec = pl.BlockSpec((tm, D_IN_PAD), row)
    z_shape = jax.ShapeDtypeStruct((n, D_IN_PAD), jnp.float32)
    if ln is None:
        z = pl.pallas_call(
            _in_proj_kernel, grid=grid, in_specs=[x_spec, w_spec], out_specs=z_spec, out_shape=z_shape,
            compiler_params=_cparams(("parallel",)), name="in_proj")(x, w_pad)
        return x, z
    g, b = ln
    vec = pl.BlockSpec((1, D_MODEL), fixed)
    xn, z = pl.pallas_call(
        _in_proj_ln_kernel, grid=grid, in_specs=[x_spec, vec, vec, w_spec],
        out_specs=[x_spec, z_spec],
        out_shape=[jax.ShapeDtypeStruct((n, D_MODEL), jnp.float32), z_shape],
        compiler_params=_cparams(("parallel",)), name="in_proj_ln")(x, g, b, w_pad)
    return xn, z


def _mla_proj_kernel(ckv_ref, cq_ref, kr_ref, gq_ref, gkv_ref, wq_ref, wk_ref, wv_ref,
                     cos_ref, sin_ref, kc_ref, ksa_ref, ksb_ref, q_out, k_out, v_out):
    cqn = _rms_norm(cq_ref[0], gq_ref[...])
    ckvn = _rms_norm(ckv_ref[0], gkv_ref[...])
    kr = kr_ref[0]

    krope = (kr * kc_ref[...]
             + pltpu.roll(kr, LANES - HALF_ROPE, axis=1) * ksa_ref[...]
             + pltpu.roll(kr, HALF_ROPE, axis=1) * ksb_ref[...])
    k_all = _dot(ckvn, wk_ref[...])
    for h in range(MLA_HEADS):
        k_out[0, h] = (k_all[:, h * HEAD_PAD:(h + 1) * HEAD_PAD] + krope).astype(k_out.dtype)

    v_all = _dot_nt(wv_ref[...], ckvn)
    ones = jnp.ones((V_ROWS - MLA_V, v_all.shape[1]), jnp.float32)
    for h in range(MLA_HEADS):
        v_out[0, h, 0] = jnp.concatenate([v_all[h * MLA_V:(h + 1) * MLA_V], ones], axis=0).astype(v_out.dtype)

    q_all = _dot_nt(wq_ref[...], cqn)
    cos_t = cos_ref[...]
    sin_t = sin_ref[...]
    qscale = (MLA_NOPE + MLA_ROPE) ** -0.5 * LOG2E
    for h in range(MLA_HEADS):
        base = h * HEAD_PAD
        nope = q_all[base:base + MLA_NOPE]
        x1 = q_all[base + MLA_NOPE:base + MLA_NOPE + HALF_ROPE]
        x2 = q_all[base + MLA_NOPE + HALF_ROPE:base + MLA_NOPE + MLA_ROPE]
        zero = q_all[base + MLA_NOPE + MLA_ROPE:base + HEAD_PAD]
        qh = jnp.concatenate([nope, x1 * cos_t - x2 * sin_t, x2 * cos_t + x1 * sin_t, zero], axis=0)
        q_out[0, h] = (qh * qscale).astype(q_out.dtype)


def mla_proj(z3, gq, gkv, wq_t, wk_pad, wv_t, rope):
    b, s, _ = z3.shape
    ts = min(TS, s)
    nt = s // ts
    cos_t, sin_t, kc, ksa, ksb = rope
    fixed = lambda bi, i: (0, 0)
    in_specs = [
        pl.BlockSpec((1, ts, MLA_KV_RANK), lambda bi, i: (bi, i, COL_CKV)),
        pl.BlockSpec((1, ts, MLA_Q_RANK), lambda bi, i: (bi, i, COL_CQ)),
        pl.BlockSpec((1, ts, LANES), lambda bi, i: (bi, i, COL_KR)),
        pl.BlockSpec((1, MLA_Q_RANK), fixed),
        pl.BlockSpec((1, MLA_KV_RANK), fixed),
        pl.BlockSpec((MLA_HEADS * HEAD_PAD, MLA_Q_RANK), fixed),
        pl.BlockSpec((MLA_KV_RANK, MLA_HEADS * HEAD_PAD), fixed),
        pl.BlockSpec((MLA_HEADS * MLA_V, MLA_KV_RANK), fixed),
        pl.BlockSpec((HALF_ROPE, ts), lambda bi, i: (0, i)),
        pl.BlockSpec((HALF_ROPE, ts), lambda bi, i: (0, i)),
        pl.BlockSpec((ts, LANES), lambda bi, i: (i, 0)),
        pl.BlockSpec((ts, LANES), lambda bi, i: (i, 0)),
        pl.BlockSpec((ts, LANES), lambda bi, i: (i, 0)),
    ]
    out_specs = [
        pl.BlockSpec((1, MLA_HEADS, HEAD_PAD, ts), lambda bi, i: (bi, 0, 0, i)),
        pl.BlockSpec((1, MLA_HEADS, ts, HEAD_PAD), lambda bi, i: (bi, 0, i, 0)),
        pl.BlockSpec((1, MLA_HEADS, 1, V_ROWS, ts), lambda bi, i: (bi, 0, i, 0, 0)),
    ]
    out_shape = [
        jax.ShapeDtypeStruct((b, MLA_HEADS, HEAD_PAD, s), MXU_DTYPE),
        jax.ShapeDtypeStruct((b, MLA_HEADS, s, HEAD_PAD), MXU_DTYPE),
        jax.ShapeDtypeStruct((b, MLA_HEADS, nt, V_ROWS, ts), MXU_DTYPE),
    ]
    return pl.pallas_call(
        _mla_proj_kernel, grid=(b, nt), in_specs=in_specs, out_specs=out_specs, out_shape=out_shape,
        compiler_params=_cparams(("parallel", "parallel")), name="mla_proj",
    )(z3, z3, z3, gq, gkv, wq_t, wk_pad, wv_t, cos_t, sin_t, kc, ksa, ksb)


def _attn_kernel(k_ref, q_ref, v_ref, o_ref, *scratch, n_chunks, tkc, group_size):
    n_heads = q_ref.shape[1]
    tq = q_ref.shape[3]
    s_bufs = [scratch[4 * hd:4 * hd + 2] for hd in range(n_heads)]
    p_bufs = [scratch[4 * hd + 2:4 * hd + 4] for hd in range(n_heads)]

    def scores(hd, c, s_ref):
        start = pl.multiple_of(c * tkc, tkc)
        s = jnp.dot(k_ref[0, hd, pl.ds(start, tkc), :], q_ref[0, hd], preferred_element_type=jnp.float32)
        s_ref[...] = s
        return jnp.max(s, axis=0, keepdims=True)

    def accumulate(hd, c, s_ref, p_ref, mx, m, acc):
        m_new = jnp.maximum(m, mx)
        alpha = jnp.exp2(m - m_new)
        p_ref[...] = jnp.exp2(s_ref[...] - m_new).astype(p_ref.dtype)
        pv = jnp.dot(v_ref[0, hd, c], p_ref[...], preferred_element_type=jnp.float32)
        return m_new, alpha * acc + pv

    def group(c0, state, prefetch_last):
        state = list(state)
        for g in range(group_size):
            for hd in range(n_heads):
                mx, m, acc = state[hd]
                mx_next = None
                if g + 1 < group_size or prefetch_last:
                    mx_next = scores(hd, c0 + g + 1, s_bufs[hd][(g + 1) % 2])
                m, acc = accumulate(hd, c0 + g, s_bufs[hd][g % 2], p_bufs[hd][g % 2], mx, m, acc)
                state[hd] = (mx_next, m, acc)
        return tuple(state)

    def body(j, state):
        return group(j * group_size, state, prefetch_last=True)

    state = tuple((scores(hd, 0, s_bufs[hd][0]), jnp.full((1, tq), NEG_BIG, jnp.float32),
                   jnp.zeros((v_ref.shape[3], tq), jnp.float32)) for hd in range(n_heads))
    n_groups = n_chunks // group_size
    state = lax.fori_loop(0, n_groups - 1, body, state)
    state = group((n_groups - 1) * group_size, state, prefetch_last=False)
    o_t = jnp.concatenate([acc[:MLA_V] / acc[MLA_V:MLA_V + 1] for _, _, acc in state], axis=0)
    o_ref[0] = o_t.T


def attention(q_t, k, v_t):
    b, h, _, s = q_t.shape
    nc, tkc = v_t.shape[2], v_t.shape[4]
    tq = min(TQ, s)
    nh = ATTN_HEADS
    group_size = min(ATTN_GROUP, nc)
    assert group_size % 2 == 0 and nc % group_size == 0, "chunk groups alternate two buffers"
    kern = functools.partial(_attn_kernel, n_chunks=nc, tkc=tkc, group_size=group_size)
    per_head = [pltpu.VMEM((tkc, tq), jnp.float32), pltpu.VMEM((tkc, tq), jnp.float32),
                pltpu.VMEM((tkc, tq), MXU_DTYPE), pltpu.VMEM((tkc, tq), MXU_DTYPE)]
    return pl.pallas_call(
        kern, grid=(b, h // nh, s // tq),
        in_specs=[
            pl.BlockSpec((1, nh, s, HEAD_PAD), lambda bi, hi, qi: (bi, hi, 0, 0)),
            pl.BlockSpec((1, nh, HEAD_PAD, tq), lambda bi, hi, qi: (bi, hi, 0, qi)),
            pl.BlockSpec((1, nh, nc, V_ROWS, tkc), lambda bi, hi, qi: (bi, hi, 0, 0, 0)),
        ],
        out_specs=pl.BlockSpec((1, tq, nh * MLA_V), lambda bi, hi, qi: (bi, qi, hi)),
        out_shape=jax.ShapeDtypeStruct((b, s, h * MLA_V), jnp.float32),
        scratch_shapes=per_head * nh,
        compiler_params=_cparams(("parallel", "parallel", "parallel")), name="attention",
    )(k, q_t, v_t)


def _shift_rows(x, d, edge_rows, row):
    ts = x.shape[0]
    y = pltpu.roll(x, (-d) % ts, axis=0)
    if d < 0:
        return jnp.where(row == 0, edge_rows[0], y)
    for j in range(d):
        y = jnp.where(row == ts - d + j, edge_rows[j], y)
    return y


def _scan_rows(a, u, reverse):
    ts = a.shape[0]
    row = lax.broadcasted_iota(jnp.int32, a.shape, 0)
    d = 1
    while d < ts:
        if d % SUBLANES == 0:
            one = jnp.ones((d, a.shape[1]), a.dtype)
            zero = jnp.zeros((d, a.shape[1]), a.dtype)
            if reverse:
                a_sh = jnp.concatenate([a[d:], one], axis=0)
                u_sh = jnp.concatenate([u[d:], zero], axis=0)
            else:
                a_sh = jnp.concatenate([one, a[:ts - d]], axis=0)
                u_sh = jnp.concatenate([zero, u[:ts - d]], axis=0)
        else:
            if reverse:
                valid = row < ts - d
                shift = ts - d
            else:
                valid = row >= d
                shift = d
            a_sh = jnp.where(valid, pltpu.roll(a, shift, axis=0), 1.0)
            u_sh = jnp.where(valid, pltpu.roll(u, shift, axis=0), 0.0)
        u = u + a * u_sh
        a = a * a_sh
        d *= 2
    return a, u


def _lru_core(x, xp, xn, first, last, cw_ref, cb_ref, wg_ref, bg_ref, lam_ref, carry_ref, h_ref, reverse):
    ts = x.shape[0]
    row = lax.broadcasted_iota(jnp.int32, x.shape, 0)
    keep_prev = jnp.where(first, 0.0, 1.0)
    keep_next = jnp.where(last, 0.0, 1.0)
    prev_row = xp[SUBLANES - 1:SUBLANES] * keep_prev
    next0 = xn[0:1] * keep_next
    next1 = xn[1:2] * keep_next
    cw = cw_ref[...]
    xc = (cw[0:1] * _shift_rows(x, -1, [prev_row], row) + cw[1:2] * x
          + cw[2:3] * _shift_rows(x, 1, [next0], row)
          + cw[3:4] * _shift_rows(x, 2, [next0, next1], row) + cb_ref[...])
    gates = _dot(xc, wg_ref[...]) + bg_ref[...]
    rec = jax.nn.sigmoid(gates[:, :LRU_DIM])
    inp = jax.nn.sigmoid(gates[:, LRU_DIM:])
    neg_lam = -lam_ref[...]
    softplus = jnp.maximum(neg_lam, 0.0) + jnp.log(1.0 + jnp.exp(-jnp.abs(neg_lam)))
    log_a = -LRU_C * rec * softplus
    a = jnp.exp(log_a)
    u = jnp.sqrt(1.0 - a * a) * (inp * xc)
    a_cum, h0 = _scan_rows(a, u, reverse)

    @pl.when(pl.program_id(1) == 0)
    def _():
        carry_ref[...] = jnp.zeros_like(carry_ref)

    h = h0 + a_cum * carry_ref[0:1]
    h_ref[0] = h
    edge = h[0:1] if reverse else h[ts - 1:ts]
    carry_ref[...] = jnp.broadcast_to(edge, carry_ref.shape)


def _lru_fwd_kernel(x_ref, xp_ref, xn_ref, cc_ref, ccp_ref, ccn_ref, ch_ref, chp_ref, chn_ref, cbg_ref,
                    cw_ref, cb_ref, wg_ref, bg_ref, lam_ref, sw_ref, h_ref, y_ref, carry_ref):
    i = pl.program_id(1)
    first = i == 0
    last = i == pl.num_programs(1) - 1
    _lru_core(x_ref[0], xp_ref[0], xn_ref[0], first, last, cw_ref, cb_ref, wg_ref, bg_ref, lam_ref,
              carry_ref, h_ref, reverse=False)
    g = cc_ref[0] * ch_ref[0]
    row = lax.broadcasted_iota(jnp.int32, g.shape, 0)
    keep_prev = jnp.where(first, 0.0, 1.0)
    keep_next = jnp.where(last, 0.0, 1.0)
    g_prev = ccp_ref[0, SUBLANES - 1:SUBLANES] * chp_ref[0, SUBLANES - 1:SUBLANES] * keep_prev
    g_next = ccn_ref[0, 0:1] * chn_ref[0, 0:1] * keep_next
    sw = sw_ref[...]
    conv = (sw[0:1] * _shift_rows(g, -1, [g_prev], row) + sw[1:2] * g
            + sw[2:3] * _shift_rows(g, 1, [g_next], row))
    y_ref[0] = cbg_ref[0] * conv


def _lru_bwd_kernel(x_ref, xp_ref, xn_ref, cw_ref, cb_ref, wg_ref, bg_ref, lam_ref, h_ref, carry_ref):
    i = pl.program_id(1)
    nt = pl.num_programs(1)
    first = i == nt - 1
    last = i == 0
    _lru_core(x_ref[0], xp_ref[0], xn_ref[0], first, last, cw_ref, cb_ref, wg_ref, bg_ref, lam_ref,
              carry_ref, h_ref, reverse=True)


def lru_scan(z3, conv_w, conv_b, wg, bg, lam, short_w=None, reverse=False):
    b, s, _ = z3.shape
    ts = min(TS, s)
    nt = s // ts
    rb = ts // SUBLANES
    nrb = s // SUBLANES
    tile = (lambda i: nt - 1 - i) if reverse else (lambda i: i)

    def main(col):
        return pl.BlockSpec((1, ts, LRU_DIM), lambda bi, i: (bi, tile(i), col))

    def prev(col):
        return pl.BlockSpec((1, SUBLANES, LRU_DIM),
                            lambda bi, i: (bi, jnp.maximum(tile(i) * rb - 1, 0), col))

    def nxt(col):
        return pl.BlockSpec((1, SUBLANES, LRU_DIM),
                            lambda bi, i: (bi, jnp.minimum((tile(i) + 1) * rb, nrb - 1), col))

    fixed = lambda bi, i: (0, 0)
    par = [pl.BlockSpec((4, LRU_DIM), fixed), pl.BlockSpec((1, LRU_DIM), fixed),
           pl.BlockSpec((LRU_DIM, 2 * LRU_DIM), fixed), pl.BlockSpec((1, 2 * LRU_DIM), fixed),
           pl.BlockSpec((1, LRU_DIM), fixed)]
    h_shape = jax.ShapeDtypeStruct((b, s, LRU_DIM), jnp.float32)
    scratch = [pltpu.VMEM((SUBLANES, LRU_DIM), jnp.float32)]
    if reverse:
        return pl.pallas_call(
            _lru_bwd_kernel, grid=(b, nt),
            in_specs=[main(COL_LX), prev(COL_LX), nxt(COL_LX)] + par,
            out_specs=main(0), out_shape=h_shape, scratch_shapes=scratch,
            compiler_params=_cparams(("parallel", "arbitrary")), name="lru_bwd",
        )(z3, z3, z3, conv_w, conv_b, wg, bg, lam)
    return pl.pallas_call(
        _lru_fwd_kernel, grid=(b, nt),
        in_specs=[main(COL_LX), prev(COL_LX), nxt(COL_LX), main(COL_CC), prev(COL_CC), nxt(COL_CC),
                  main(COL_CH), prev(COL_CH), nxt(COL_CH), main(COL_CB)] + par
                 + [pl.BlockSpec((3, CONV_DIM), fixed)],
        out_specs=[main(0), main(0)], out_shape=[h_shape, h_shape], scratch_shapes=scratch,
        compiler_params=_cparams(("parallel", "arbitrary")), name="lru_fwd",
    )(z3, z3, z3, z3, z3, z3, z3, z3, z3, z3, conv_w, conv_b, wg, bg, lam, short_w)


def _post_mixer_body(yc_ref, o_ref, hf_ref, hb_ref, lg_ref, xn_ref, gm_ref, wo_ref, g1_ref, b1_ref):
    gm = gm_ref[...]
    y_lru = jax.nn.gelu(lg_ref[...], approximate=True) * (hf_ref[...] + hb_ref[...])
    y = jnp.concatenate([
        _rms_norm(yc_ref[...], gm[:, :CONV_DIM]),
        _rms_norm(o_ref[...], gm[:, CONV_DIM:CONV_DIM + MLA_HEADS * MLA_V]),
        _rms_norm(y_lru, gm[:, CONV_DIM + MLA_HEADS * MLA_V:]),
    ], axis=1)
    mix = _dot(y, wo_ref[...])
    return _layer_norm(DN_ALPHA * xn_ref[...] + mix, g1_ref[...], b1_ref[...])


def _post_mixer_kernel(yc_ref, o_ref, hf_ref, hb_ref, lg_ref, xn_ref, gm_ref, wo_ref, g1_ref, b1_ref,
                       x1_ref, x1b_ref):
    x1 = _post_mixer_body(yc_ref, o_ref, hf_ref, hb_ref, lg_ref, xn_ref, gm_ref, wo_ref, g1_ref, b1_ref)
    x1_ref[...] = x1
    x1b_ref[...] = x1.astype(x1b_ref.dtype)


def _post_mixer_router_kernel(yc_ref, o_ref, hf_ref, hb_ref, lg_ref, xn_ref, gm_ref, wo_ref, g1_ref,
                              b1_ref, wr_ref, x1_ref, route_ref):
    x1 = _post_mixer_body(yc_ref, o_ref, hf_ref, hb_ref, lg_ref, xn_ref, gm_ref, wo_ref, g1_ref, b1_ref)
    x1_ref[...] = x1
    wr = wr_ref[...].astype(jnp.bfloat16)
    x_hi = x1.astype(jnp.bfloat16)
    x_lo = (x1 - x_hi.astype(jnp.float32)).astype(jnp.bfloat16)
    t_hi = jnp.dot(x_hi, wr, preferred_element_type=jnp.float32)
    t_lo = jnp.dot(x_lo, wr, preferred_element_type=jnp.float32)
    logits = t_hi + t_lo + pltpu.roll(t_hi, LANES - N_EXPERTS, axis=1)
    lane = lax.broadcasted_iota(jnp.int32, logits.shape, 1)
    logits = jnp.where(lane < N_EXPERTS, logits, NEG_BIG)
    v1 = jnp.max(logits, axis=1, keepdims=True)
    i1 = jnp.min(jnp.where(logits == v1, lane, LANES), axis=1, keepdims=True)
    rest = jnp.where(lane == i1, NEG_BIG, logits)
    v2 = jnp.max(rest, axis=1, keepdims=True)
    i2 = jnp.min(jnp.where(rest == v2, lane, LANES), axis=1, keepdims=True)
    e = jnp.exp(v2 - v1)
    g_top = 1.0 / (1.0 + e)
    g_sec = e * g_top
    route_ref[...] = jnp.where(lane == 0, i1.astype(jnp.float32),
                               jnp.where(lane == 1, i2.astype(jnp.float32),
                                         jnp.where(lane == 2, g_top, jnp.where(lane == 3, g_sec, 0.0))))


def post_mixer(y_conv, o, h_f, h_b, z, xn, gm, wo, g1, b1, w_router=None):
    n = xn.shape[0]
    tm = min(TM, n)
    row = lambda i: (i, 0)
    fixed = lambda i: (0, 0)
    in_specs = [
        pl.BlockSpec((tm, CONV_DIM), row), pl.BlockSpec((tm, MLA_HEADS * MLA_V), row),
        pl.BlockSpec((tm, LRU_DIM), row), pl.BlockSpec((tm, LRU_DIM), row),
        pl.BlockSpec((tm, LRU_DIM), lambda i: (i, COL_LG)), pl.BlockSpec((tm, D_MODEL), row),
        pl.BlockSpec((1, D_MODEL), fixed), pl.BlockSpec((D_MODEL, D_MODEL), fixed),
        pl.BlockSpec((1, D_MODEL), fixed), pl.BlockSpec((1, D_MODEL), fixed),
    ]
    x_spec = pl.BlockSpec((tm, D_MODEL), row)
    x_shape = jax.ShapeDtypeStruct((n, D_MODEL), jnp.float32)
    args = (y_conv, o, h_f, h_b, z, xn, gm, wo, g1, b1)
    if w_router is None:
        return pl.pallas_call(
            _post_mixer_kernel, grid=(n // tm,), in_specs=in_specs, out_specs=[x_spec, x_spec],
            out_shape=[x_shape, jax.ShapeDtypeStruct((n, D_MODEL), MXU_DTYPE)],
            compiler_params=_cparams(("parallel",)), name="post_mixer")(*args)
    return pl.pallas_call(
        _post_mixer_router_kernel, grid=(n // tm,),
        in_specs=in_specs + [pl.BlockSpec((D_MODEL, LANES), fixed)],
        out_specs=[x_spec, pl.BlockSpec((tm, LANES), row)],
        out_shape=[x_shape, jax.ShapeDtypeStruct((n, LANES), jnp.float32)],
        compiler_params=_cparams(("parallel",)), name="post_mixer_router")(*args, w_router)


def _swiglu_chunk(xb, wg, wu, wd):
    gate = jnp.dot(xb, wg, preferred_element_type=jnp.float32)
    up = jnp.dot(xb, wu, preferred_element_type=jnp.float32)
    hidden = (jax.nn.silu(gate) * up).astype(wd.dtype)
    return jnp.dot(hidden, wd, preferred_element_type=jnp.float32)


def _ffn_dense_kernel(xb_ref, x1_ref, wg_ref, wu_ref, wd_ref, g2_ref, b2_ref, out_ref, acc_ref):
    f = pl.program_id(1)

    @pl.when(f == 0)
    def _():
        acc_ref[...] = jnp.zeros_like(acc_ref)

    acc_ref[...] += _swiglu_chunk(xb_ref[...], wg_ref[...], wu_ref[...], wd_ref[...])

    @pl.when(f == pl.num_programs(1) - 1)
    def _():
        out_ref[...] = _layer_norm(DN_ALPHA * x1_ref[...] + acc_ref[...], g2_ref[...], b2_ref[...])


def ffn_dense(x1b, x1, wg, wu, wd, g2, b2):
    n = x1.shape[0]
    tm = min(TM_FFN, n)
    tf = TF
    row = lambda i, f: (i, 0)
    fixed = lambda i, f: (0, 0)
    return pl.pallas_call(
        _ffn_dense_kernel, grid=(n // tm, D_FF // tf),
        in_specs=[pl.BlockSpec((tm, D_MODEL), row), pl.BlockSpec((tm, D_MODEL), row),
                  pl.BlockSpec((D_MODEL, tf), lambda i, f: (0, f)),
                  pl.BlockSpec((D_MODEL, tf), lambda i, f: (0, f)),
                  pl.BlockSpec((tf, D_MODEL), lambda i, f: (f, 0)),
                  pl.BlockSpec((1, D_MODEL), fixed), pl.BlockSpec((1, D_MODEL), fixed)],
        out_specs=pl.BlockSpec((tm, D_MODEL), row),
        out_shape=jax.ShapeDtypeStruct((n, D_MODEL), jnp.float32),
        scratch_shapes=[pltpu.VMEM((tm, D_MODEL), jnp.float32)],
        compiler_params=_cparams(("parallel", "arbitrary")), name="ffn_dense",
    )(x1b, x1, wg, wu, wd, g2, b2)


def _moe_dispatch_kernel(st_ref, nb_ref, x_ref, lr_ref, xs_hbm, buf_ref, sem):
    i = pl.program_id(0)
    n_rows = buf_ref.shape[0] // SUBLANES
    lr = lr_ref[0]
    rows = lax.broadcasted_iota(jnp.int32, (n_rows, lr.shape[1]), 0)
    sel = jnp.where(jnp.logical_or(rows == lr[0:1], rows == lr[1:2]), 1.0, 0.0)
    xc = _dot(sel, x_ref[...])
    for s in range(SUBLANES):
        buf_ref[pl.ds(s, n_rows, stride=SUBLANES), :] = xc[:, s * LANES:(s + 1) * LANES]

    blk = DISP_BLK * SUBLANES

    def block_copy(src, dst):
        return pltpu.make_async_copy(buf_ref.at[pl.ds(src, blk), :], xs_hbm.at[pl.ds(dst, blk), :], sem)

    off = jnp.int32(0)
    for e in range(N_EXPERTS):
        a = st_ref[i * N_EXPERTS + e]
        nb = nb_ref[i * N_EXPERTS + e]

        def start(k, c, a=a, off=off):
            block_copy(pl.multiple_of((off + k * DISP_BLK) * SUBLANES, blk),
                       pl.multiple_of((a + k * DISP_BLK) * SUBLANES, SUBLANES)).start()
            return c

        lax.fori_loop(0, nb, start, 0)
        off = off + nb * DISP_BLK

    def wait(k, c):
        block_copy(0, 0).wait()
        return c

    lax.fori_loop(0, off // DISP_BLK, wait, 0)


def moe_dispatch(x1, row_start, blk_count, local_row, n_sorted_rows):
    n = x1.shape[0]
    tm = local_row.shape[2]
    buf_rows = -(-(2 * tm + N_EXPERTS * (DISP_BLK - 1)) // 16) * 16
    grid_spec = pltpu.PrefetchScalarGridSpec(
        num_scalar_prefetch=2, grid=(n // tm,),
        in_specs=[pl.BlockSpec((tm, D_MODEL), lambda i, st, nb: (i, 0)),
                  pl.BlockSpec((1, 2, tm), lambda i, st, nb: (i, 0, 0))],
        out_specs=pl.BlockSpec(memory_space=pl.ANY),
        scratch_shapes=[pltpu.VMEM((buf_rows * SUBLANES, LANES), jnp.float32), pltpu.SemaphoreType.DMA(())],
    )
    return pl.pallas_call(
        _moe_dispatch_kernel, grid_spec=grid_spec,
        out_shape=jax.ShapeDtypeStruct((n_sorted_rows * SUBLANES, LANES), jnp.float32),
        compiler_params=_cparams(("arbitrary",)), name="moe_dispatch",
    )(row_start, blk_count, x1, local_row)


def _moe_ffn_kernel(te_ref, tr_ref, xs_ref, wg_ref, wu_ref, wd_ref, y_ref, xb_ref, acc_ref):
    i = pl.program_id(0)
    f = pl.program_id(1)
    n_valid = tr_ref[i]
    tm = xb_ref.shape[0]

    @pl.when(jnp.logical_and(n_valid > 0, f == 0))
    def _():
        live = lax.broadcasted_iota(jnp.int32, (tm, LANES), 0) < n_valid
        for s in range(SUBLANES):
            piece = xs_ref[pl.ds(s, tm, stride=SUBLANES), :]
            xb_ref[:, s * LANES:(s + 1) * LANES] = jnp.where(live, piece, 0.0).astype(xb_ref.dtype)
        acc_ref[...] = jnp.zeros_like(acc_ref)

    @pl.when(n_valid > 0)
    def _():
        acc_ref[...] += _swiglu_chunk(xb_ref[...], wg_ref[...], wu_ref[...], wd_ref[...])

    @pl.when(f == pl.num_programs(1) - 1)
    def _():
        y_ref[...] = jnp.where(n_valid > 0, acc_ref[...], 0.0).astype(y_ref.dtype)


def moe_ffn(x_sorted, tile_expert, tile_rows, wg, wu, wd):
    n_tiles = tile_expert.shape[0]
    tm = x_sorted.shape[0] // (n_tiles * SUBLANES)
    tf = TF
    grid_spec = pltpu.PrefetchScalarGridSpec(
        num_scalar_prefetch=2, grid=(n_tiles, D_FF // tf),
        in_specs=[
            pl.BlockSpec((tm * SUBLANES, LANES), lambda i, f, te, tr: (i, 0)),
            pl.BlockSpec((None, D_MODEL, tf), lambda i, f, te, tr: (te[i], 0, f)),
            pl.BlockSpec((None, D_MODEL, tf), lambda i, f, te, tr: (te[i], 0, f)),
            pl.BlockSpec((None, tf, D_MODEL), lambda i, f, te, tr: (te[i], f, 0)),
        ],
        out_specs=pl.BlockSpec((tm, D_MODEL), lambda i, f, te, tr: (i, 0)),
        scratch_shapes=[pltpu.VMEM((tm, D_MODEL), MXU_DTYPE), pltpu.VMEM((tm, D_MODEL), jnp.float32)],
    )
    return pl.pallas_call(
        _moe_ffn_kernel, grid_spec=grid_spec,
        out_shape=jax.ShapeDtypeStruct((n_tiles * tm, D_MODEL), MXU_DTYPE),
        compiler_params=_cparams(("arbitrary", "arbitrary")), name="moe_ffn",
    )(tile_expert, tile_rows, x_sorted, wg, wu, wd)


def _moe_combine_kernel(a_ref, nb_ref, x1_ref, route_ref, y_hbm, g2_ref, b2_ref, out_ref, ybuf_ref, sem):
    i = pl.program_id(0)

    @pl.when(i == 0)
    def _():
        ybuf_ref[...] = jnp.zeros_like(ybuf_ref)

    def block_copy(src, dst):
        return pltpu.make_async_copy(y_hbm.at[pl.ds(src, CMB_BLK), :], ybuf_ref.at[pl.ds(dst, CMB_BLK), :], sem)

    off = jnp.int32(0)
    for e in range(N_EXPERTS):
        a = a_ref[i * N_EXPERTS + e]
        nb = nb_ref[i * N_EXPERTS + e]

        def start(k, c, a=a, off=off):
            block_copy(pl.multiple_of(a + k * CMB_BLK, CMB_BLK), pl.multiple_of(off + k * CMB_BLK, CMB_BLK)).start()
            return c

        lax.fori_loop(0, nb, start, 0)
        off = off + nb * CMB_BLK

    def wait(k, c):
        block_copy(0, 0).wait()
        return c

    lax.fori_loop(0, off // CMB_BLK, wait, 0)

    route = route_ref[...]
    col = lax.broadcasted_iota(jnp.int32, (route.shape[0], ybuf_ref.shape[0]), 1)
    sel = (jnp.where(col == route[:, 4:5].astype(jnp.int32), route[:, 2:3], 0.0)
           + jnp.where(col == route[:, 5:6].astype(jnp.int32), route[:, 3:4], 0.0))
    f = jnp.dot(sel.astype(ybuf_ref.dtype), ybuf_ref[...], preferred_element_type=jnp.float32)
    out_ref[...] = _layer_norm(DN_ALPHA * x1_ref[...] + f, g2_ref[...], b2_ref[...])


def moe_combine(blk_start, blk_count, x1, route, y_sorted, g2, b2):
    n = x1.shape[0]
    tm = min(TM_CMB, n)
    buf_rows = -(-(2 * tm + N_EXPERTS * 2 * (CMB_BLK - 1)) // 256) * 256
    row = lambda i, a, nb: (i, 0)
    fixed = lambda i, a, nb: (0, 0)
    grid_spec = pltpu.PrefetchScalarGridSpec(
        num_scalar_prefetch=2, grid=(n // tm,),
        in_specs=[pl.BlockSpec((tm, D_MODEL), row), pl.BlockSpec((tm, LANES), row),
                  pl.BlockSpec(memory_space=pl.ANY),
                  pl.BlockSpec((1, D_MODEL), fixed), pl.BlockSpec((1, D_MODEL), fixed)],
        out_specs=pl.BlockSpec((tm, D_MODEL), row),
        scratch_shapes=[pltpu.VMEM((buf_rows, D_MODEL), y_sorted.dtype), pltpu.SemaphoreType.DMA(())],
    )
    return pl.pallas_call(
        _moe_combine_kernel, grid_spec=grid_spec,
        out_shape=jax.ShapeDtypeStruct((n, D_MODEL), jnp.float32),
        compiler_params=_cparams(("arbitrary",)), name="moe_combine",
    )(blk_start, blk_count, x1, route, y_sorted, g2, b2)


def _dispatch_plan(expert_idx, tm, tm_tok):
    n = expert_idx.shape[0]
    e_flat = expert_idx.reshape(-1)
    onehot = (e_flat[:, None] == jnp.arange(N_EXPERTS, dtype=jnp.int32)[None, :]).astype(jnp.int32)
    csum = jnp.cumsum(onehot, axis=0)
    before = csum - onehot
    rank = jnp.sum(before * onehot, axis=1)
    counts = csum[-1]
    tiles_per = (counts + (DISP_BLK - 1) + tm - 1) // tm
    tile_end = jnp.cumsum(tiles_per)
    tile_begin = tile_end - tiles_per
    group_start = tile_begin * tm
    pos_flat = jnp.sum(onehot * group_start[None, :], axis=1) + rank
    n_tiles = (2 * n) // tm + N_EXPERTS + 1
    t = jnp.arange(n_tiles, dtype=jnp.int32)
    last_valid = jnp.maximum(tile_end[-1] - 1, 0)
    t_eff = jnp.minimum(t, last_valid)
    tile_expert = jnp.minimum(jnp.sum((t_eff[:, None] >= tile_end[None, :]).astype(jnp.int32), axis=1),
                              N_EXPERTS - 1).astype(jnp.int32)
    of_tile = (tile_expert[:, None] == jnp.arange(N_EXPERTS, dtype=jnp.int32)[None, :]).astype(jnp.int32)
    rows_left = jnp.sum(of_tile * (counts - (t[:, None] - tile_begin[None, :]) * tm), axis=1)
    tile_rows = jnp.where(t < tile_end[-1], jnp.clip(rows_left, 0, tm), 0).astype(jnp.int32)

    first = before[::2 * tm_tok]
    cnt = jnp.concatenate([first[1:], counts[None, :]], axis=0) - first
    start = group_start[None, :] + first
    pair_onehot = onehot.reshape(-1, 2 * tm_tok, N_EXPERTS)

    d_count = (cnt + DISP_BLK - 1) // DISP_BLK
    d_off = (jnp.cumsum(d_count, axis=1) - d_count) * DISP_BLK
    d_local = rank + jnp.sum(pair_onehot * (d_off - first)[:, None, :], axis=2).reshape(-1)
    d_local = d_local.reshape(-1, tm_tok, 2).transpose(0, 2, 1)

    blk_start = (start // CMB_BLK) * CMB_BLK
    blk_count = jnp.where(cnt > 0, (start - blk_start + cnt + CMB_BLK - 1) // CMB_BLK, 0)
    buf_off = (jnp.cumsum(blk_count, axis=1) - blk_count) * CMB_BLK
    local = pos_flat + jnp.sum(pair_onehot * (buf_off - blk_start)[:, None, :], axis=2).reshape(-1)
    flat = lambda v: v.reshape(-1).astype(jnp.int32)
    return dict(n_tiles=n_tiles, tile_expert=tile_expert, tile_rows=tile_rows,
                row_start=flat(start), d_count=flat(d_count), d_local=d_local.astype(jnp.int32),
                blk_start=flat(blk_start), blk_count=flat(blk_count), c_local=local.reshape(n, 2))


def _pad_w_in(w):
    z64 = jnp.zeros((D_MODEL, KR_X1), w.dtype)
    z32 = jnp.zeros((D_MODEL, LANES - KR_X1 - MLA_ROPE), w.dtype)
    return jnp.concatenate([w[:, 0:768], w[:, 1152:1408], w[:, 1440:1696], w[:, 1696:1952],
                            w[:, 768:1152], z64, w[:, 1408:1440], z32], axis=1).astype(MXU_DTYPE)


def _rope_tables(s):
    pos = jnp.arange(s, dtype=jnp.float32)
    inv = ROPE_THETA ** (-jnp.arange(0, MLA_ROPE, 2, dtype=jnp.float32) / MLA_ROPE)
    ang = pos[:, None] * inv[None, :]
    cos, sin = jnp.cos(ang), jnp.sin(ang)
    zl = jnp.zeros((s, KR_X1), jnp.float32)
    zh = jnp.zeros((s, HALF_ROPE), jnp.float32)
    zr = jnp.zeros((s, LANES - KR_X1 - MLA_ROPE), jnp.float32)
    kc = jnp.concatenate([zl, cos, cos, zr], axis=1)
    ksa = jnp.concatenate([zl, -sin, zh, zr], axis=1)
    ksb = jnp.concatenate([zl, zh, sin, zr], axis=1)
    return cos.T, sin.T, kc, ksa, ksb


def _block_diag(w):
    nb, bw, _ = w.shape
    out = jnp.zeros((nb * bw, nb * bw), w.dtype)
    for i in range(nb):
        out = out.at[i * bw:(i + 1) * bw, i * bw:(i + 1) * bw].set(w[i])
    return out


def kernel(x, ln_in_g, ln_in_b, w_in, conv_w, q_norm_g, w_uq, kv_norm_g, w_ukv, lru_conv_w, lru_conv_b,
           lru_wa, lru_ba, lru_wi, lru_bi, lru_lam, mix_norm_g, w_out, ln1_g, ln1_b, dense_w_gate,
           dense_w_up, dense_w_down, moe_w_router, moe_w_gate, moe_w_up, moe_w_down, ln2_g, ln2_b):
    b, s, d = x.shape
    n = b * s
    rope = _rope_tables(s)
    vec = lambda v: v.reshape(1, -1)
    cur = x.reshape(n, d)
    for l in range(DEPTH):
        w_in_pad = _pad_w_in(w_in[l])
        wq_t = jnp.pad(w_uq[l].T.reshape(MLA_HEADS, MLA_NOPE + MLA_ROPE, MLA_Q_RANK),
                       ((0, 0), (0, HEAD_PAD - MLA_NOPE - MLA_ROPE), (0, 0))
                       ).reshape(MLA_HEADS * HEAD_PAD, MLA_Q_RANK).astype(MXU_DTYPE)
        wkv = w_ukv[l].reshape(MLA_KV_RANK, MLA_HEADS, MLA_NOPE + MLA_V)
        wk_pad = jnp.pad(wkv[:, :, :MLA_NOPE], ((0, 0), (0, 0), (0, HEAD_PAD - MLA_NOPE))
                         ).reshape(MLA_KV_RANK, MLA_HEADS * HEAD_PAD).astype(MXU_DTYPE)
        wv_t = wkv[:, :, MLA_NOPE:].reshape(MLA_KV_RANK, MLA_HEADS * MLA_V).T.astype(MXU_DTYPE)
        gate_w = [jnp.concatenate([_block_diag(lru_wa[l, dr]), _block_diag(lru_wi[l, dr])], axis=1
                                  ).astype(MXU_DTYPE) for dr in range(2)]
        gate_b = [jnp.concatenate([lru_ba[l, dr], lru_bi[l, dr]]).reshape(1, -1) for dr in range(2)]

        if l == 0:
            xn, z = in_proj(cur, w_in_pad, ln=(vec(ln_in_g), vec(ln_in_b)))
        else:
            xn, z = in_proj(cur, w_in_pad)
        z3 = z.reshape(b, s, D_IN_PAD)
        q_t, k, v_t = mla_proj(z3, vec(q_norm_g[l]), vec(kv_norm_g[l]), wq_t, wk_pad, wv_t, rope)
        o = attention(q_t, k, v_t).reshape(n, MLA_HEADS * MLA_V)
        h_f, y_conv = lru_scan(z3, lru_conv_w[l], vec(lru_conv_b[l]), gate_w[0], gate_b[0],
                               vec(lru_lam[l, 0]), short_w=conv_w[l])
        h_b = lru_scan(z3, lru_conv_w[l], vec(lru_conv_b[l]), gate_w[1], gate_b[1],
                       vec(lru_lam[l, 1]), reverse=True)
        mixer_args = (y_conv.reshape(n, CONV_DIM), o, h_f.reshape(n, LRU_DIM), h_b.reshape(n, LRU_DIM),
                      z, xn, vec(mix_norm_g[l]), w_out[l].astype(MXU_DTYPE), vec(ln1_g[l]), vec(ln1_b[l]))

        j = l // 2
        if l % 2 == 0:
            x1, x1b = post_mixer(*mixer_args)
            cur = ffn_dense(x1b, x1, dense_w_gate[j].astype(MXU_DTYPE), dense_w_up[j].astype(MXU_DTYPE),
                            dense_w_down[j].astype(MXU_DTYPE), vec(ln2_g[l]), vec(ln2_b[l]))
        else:
            wr_hi = moe_w_router[j].astype(jnp.bfloat16).astype(jnp.float32)
            wr_lo = (moe_w_router[j] - wr_hi).astype(jnp.bfloat16).astype(jnp.float32)
            w_router = jnp.pad(jnp.concatenate([wr_hi, wr_lo], axis=1), ((0, 0), (0, LANES - 2 * N_EXPERTS)))
            x1, route = post_mixer(*mixer_args, w_router=w_router)
            expert_idx = route[:, 0:2].astype(jnp.int32)
            tm = min(TM_FFN, n)
            plan = _dispatch_plan(expert_idx, tm, min(TM_CMB, n))
            x_sorted = moe_dispatch(x1, plan["row_start"], plan["d_count"], plan["d_local"],
                                    plan["n_tiles"] * tm)
            y_sorted = moe_ffn(x_sorted, plan["tile_expert"], plan["tile_rows"],
                               moe_w_gate[j].astype(MXU_DTYPE), moe_w_up[j].astype(MXU_DTYPE),
                               moe_w_down[j].astype(MXU_DTYPE))
            route = jnp.concatenate([route[:, :4], plan["c_local"].astype(jnp.float32), route[:, 6:]], axis=1)
            cur = moe_combine(plan["blk_start"], plan["blk_count"], x1, route, y_sorted,
                              vec(ln2_g[l]), vec(ln2_b[l]))
    return cur.reshape(b, s, d)
```

```python
import functools
import math

import jax
import jax.numpy as jnp
from jax import lax
from jax.experimental import pallas as pl
from jax.experimental.pallas import tpu as pltpu

D_MODEL = 1024
DEPTH = 2
CONV_DIM = 256
MLA_HEADS = 8
MLA_NOPE = 64
MLA_ROPE = 32
MLA_V = 64
MLA_Q_RANK = 384
MLA_KV_RANK = 256
LRU_DIM = 256
LRU_C = 8.0
ROPE_THETA = 10000.0
D_FF = 3584
N_EXPERTS = 8
DN_ALPHA = (2.0 * DEPTH) ** 0.25
LN_EPS = 1e-5
RMS_EPS = 1e-6

D_IN_PAD = 2048
HALF_ROPE = MLA_ROPE // 2
HEAD_PAD = 128
COL_CB, COL_CC, COL_CH, COL_CKV, COL_LG, COL_LX = 0, 1, 2, 3, 4, 5
COL_CQ = 4
COL_KR = 15
KR_X1 = 64
V_ROWS = MLA_V + 16

LANES = 128
SUBLANES = 8
VMEM_LIMIT = 56 * 1024 * 1024
MXU_DTYPE = jnp.bfloat16

TM = 512
TS = 512
TQ = 512
ATTN_GROUP = 2
ATTN_HEADS = 4
TM_FFN = 1024
TF = 512
TM_CMB = 256
CMB_BLK = 16
DISP_BLK = 8
NEG_BIG = -1e30
LOG2E = 1.4426950408889634


def _cparams(sem, vmem=VMEM_LIMIT, flags=None):
    return pltpu.CompilerParams(dimension_semantics=sem, vmem_limit_bytes=vmem, flags=flags)


def _layer_norm(x, g, b):
    mu = jnp.mean(x, axis=-1, keepdims=True)
    xc = x - mu
    var = jnp.mean(xc * xc, axis=-1, keepdims=True)
    return xc * lax.rsqrt(var + LN_EPS) * g + b


def _rms_norm(x, g):
    ms = jnp.mean(x * x, axis=-1, keepdims=True)
    return x * lax.rsqrt(ms + RMS_EPS) * g


def _dot(a, b):
    return jnp.dot(a.astype(MXU_DTYPE), b.astype(MXU_DTYPE), preferred_element_type=jnp.float32)


def _dot_nt(a, b):
    return lax.dot_general(a.astype(MXU_DTYPE), b.astype(MXU_DTYPE), (((1,), (1,)), ((), ())),
                           preferred_element_type=jnp.float32)


def _in_proj_ln_kernel(x_ref, g_ref, b_ref, w_ref, xn_ref, z_ref):
    xn = _layer_norm(x_ref[...], g_ref[...], b_ref[...])
    xn_ref[...] = xn
    z_ref[...] = _dot(xn, w_ref[...])


def _in_proj_kernel(x_ref, w_ref, z_ref):
    z_ref[...] = _dot(x_ref[...], w_ref[...])


def in_proj(x, w_pad, ln=None):
    n = x.shape[0]
    tm = min(TM, n)
    grid = (n // tm,)
    row = lambda i: (i, 0)
    fixed = lambda i: (0, 0)
    x_spec = pl.BlockSpec((tm, D_MODEL), row)
    w_spec = pl.BlockSpec((D_MODEL, D_IN_PAD), fixed)
    z_spec = pl.BlockSpec((tm, D_IN_PAD), row)
    z_shape = jax.ShapeDtypeStruct((n, D_IN_PAD), jnp.float32)
    if ln is None:
        z = pl.pallas_call(
            _in_proj_kernel, grid=grid, in_specs=[x_spec, w_spec], out_specs=z_spec, out_shape=z_shape,
            compiler_params=_cparams(("parallel",)), name="in_proj")(x, w_pad)
        return x, z
    g, b = ln
    vec = pl.BlockSpec((1, D_MODEL), fixed)
    xn, z = pl.pallas_call(
        _in_proj_ln_kernel, grid=grid, in_specs=[x_spec, vec, vec, w_spec],
        out_specs=[x_spec, z_spec],
        out_shape=[jax.ShapeDtypeStruct((n, D_MODEL), jnp.float32), z_shape],
        compiler_params=_cparams(("parallel",)), name="in_proj_ln")(x, g, b, w_pad)
    return xn, z


def _mla_proj_kernel(ckv_ref, cq_ref, kr_ref, gq_ref, gkv_ref, wq_ref, wk_ref, wv_ref,
                     cos_ref, sin_ref, kc_ref, ksa_ref, ksb_ref, q_out, k_out, v_out):
    cqn = _rms_norm(cq_ref[0], gq_ref[...])
    ckvn = _rms_norm(ckv_ref[0], gkv_ref[...])
    kr = kr_ref[0]

    krope = (kr * kc_ref[...]
             + pltpu.roll(kr, LANES - HALF_ROPE, axis=1) * ksa_ref[...]
             + pltpu.roll(kr, HALF_ROPE, axis=1) * ksb_ref[...])
    k_all = _dot(ckvn, wk_ref[...])
    for h in range(MLA_HEADS):
        k_out[0, h] = (k_all[:, h * HEAD_PAD:(h + 1) * HEAD_PAD] + krope).astype(k_out.dtype)

    v_all = _dot_nt(wv_ref[...], ckvn)
    ones = jnp.ones((V_ROWS - MLA_V, v_all.shape[1]), jnp.float32)
    for h in range(MLA_HEADS):
        v_out[0, h, 0] = jnp.concatenate([v_all[h * MLA_V:(h + 1) * MLA_V], ones], axis=0).astype(v_out.dtype)

    q_all = _dot_nt(wq_ref[...], cqn)
    cos_t = cos_ref[...]
    sin_t = sin_ref[...]
    qscale = (MLA_NOPE + MLA_ROPE) ** -0.5 * LOG2E
    for h in range(MLA_HEADS):
        base = h * HEAD_PAD
        nope = q_all[base:base + MLA_NOPE]
        x1 = q_all[base + MLA_NOPE:base + MLA_NOPE + HALF_ROPE]
        x2 = q_all[base + MLA_NOPE + HALF_ROPE:base + MLA_NOPE + MLA_ROPE]
        zero = q_all[base + MLA_NOPE + MLA_ROPE:base + HEAD_PAD]
        qh = jnp.concatenate([nope, x1 * cos_t - x2 * sin_t, x2 * cos_t + x1 * sin_t, zero], axis=0)
        q_out[0, h] = (qh * qscale).astype(q_out.dtype)


def mla_proj(z3, gq, gkv, wq_t, wk_pad, wv_t, rope):
    b, s, _ = z3.shape
    ts = min(TS, s)
    nt = s // ts
    cos_t, sin_t, kc, ksa, ksb = rope
    fixed = lambda bi, i: (0, 0)
    in_specs = [
        pl.BlockSpec((1, ts, MLA_KV_RANK), lambda bi, i: (bi, i, COL_CKV)),
        pl.BlockSpec((1, ts, MLA_Q_RANK), lambda bi, i: (bi, i, COL_CQ)),
        pl.BlockSpec((1, ts, LANES), lambda bi, i: (bi, i, COL_KR)),
        pl.BlockSpec((1, MLA_Q_RANK), fixed),
        pl.BlockSpec((1, MLA_KV_RANK), fixed),
        pl.BlockSpec((MLA_HEADS * HEAD_PAD, MLA_Q_RANK), fixed),
        pl.BlockSpec((MLA_KV_RANK, MLA_HEADS * HEAD_PAD), fixed),
        pl.BlockSpec((MLA_HEADS * MLA_V, MLA_KV_RANK), fixed),
        pl.BlockSpec((HALF_ROPE, ts), lambda bi, i: (0, i)),
        pl.BlockSpec((HALF_ROPE, ts), lambda bi, i: (0, i)),
        pl.BlockSpec((ts, LANES), lambda bi, i: (i, 0)),
        pl.BlockSpec((ts, LANES), lambda bi, i: (i, 0)),
        pl.BlockSpec((ts, LANES), lambda bi, i: (i, 0)),
    ]
    out_specs = [
        pl.BlockSpec((1, MLA_HEADS, HEAD_PAD, ts), lambda bi, i: (bi, 0, 0, i)),
        pl.BlockSpec((1, MLA_HEADS, ts, HEAD_PAD), lambda bi, i: (bi, 0, i, 0)),
        pl.BlockSpec((1, MLA_HEADS, 1, V_ROWS, ts), lambda bi, i: (bi, 0, i, 0, 0)),
    ]
    out_shape = [
        jax.ShapeDtypeStruct((b, MLA_HEADS, HEAD_PAD, s), MXU_DTYPE),
        jax.ShapeDtypeStruct((b, MLA_HEADS, s, HEAD_PAD), MXU_DTYPE),
        jax.ShapeDtypeStruct((b, MLA_HEADS, nt, V_ROWS, ts), MXU_DTYPE),
    ]
    return pl.pallas_call(
        _mla_proj_kernel, grid=(b, nt), in_specs=in_specs, out_specs=out_specs, out_shape=out_shape,
        compiler_params=_cparams(("parallel", "parallel")), name="mla_proj",
    )(z3, z3, z3, gq, gkv, wq_t, wk_pad, wv_t, cos_t, sin_t, kc, ksa, ksb)


def _attn_kernel(k_ref, q_ref, v_ref, o_ref, *scratch, n_chunks, tkc, group_size):
    n_heads = q_ref.shape[1]
    tq = q_ref.shape[3]
    s_bufs = [scratch[4 * hd:4 * hd + 2] for hd in range(n_heads)]
    p_bufs = [scratch[4 * hd + 2:4 * hd + 4] for hd in range(n_heads)]

    def scores(hd, c, s_ref):
        start = pl.multiple_of(c * tkc, tkc)
        s = jnp.dot(k_ref[0, hd, pl.ds(start, tkc), :], q_ref[0, hd], preferred_element_type=jnp.float32)
        s_ref[...] = s
        return jnp.max(s, axis=0, keepdims=True)

    def accumulate(hd, c, s_ref, p_ref, mx, m, acc):
        m_new = jnp.maximum(m, mx)
        alpha = jnp.exp2(m - m_new)
        p_ref[...] = jnp.exp2(s_ref[...] - m_new).astype(p_ref.dtype)
        pv = jnp.dot(v_ref[0, hd, c], p_ref[...], preferred_element_type=jnp.float32)
        return m_new, alpha * acc + pv

    def group(c0, state, prefetch_last):
        state = list(state)
        for g in range(group_size):
            for hd in range(n_heads):
                mx, m, acc = state[hd]
                mx_next = None
                if g + 1 < group_size or prefetch_last:
                    mx_next = scores(hd, c0 + g + 1, s_bufs[hd][(g + 1) % 2])
                m, acc = accumulate(hd, c0 + g, s_bufs[hd][g % 2], p_bufs[hd][g % 2], mx, m, acc)
                state[hd] = (mx_next, m, acc)
        return tuple(state)

    def body(j, state):
        return group(j * group_size, state, prefetch_last=True)

    state = tuple((scores(hd, 0, s_bufs[hd][0]), jnp.full((1, tq), NEG_BIG, jnp.float32),
                   jnp.zeros((v_ref.shape[3], tq), jnp.float32)) for hd in range(n_heads))
    n_groups = n_chunks // group_size
    state = lax.fori_loop(0, n_groups - 1, body, state)
    state = group((n_groups - 1) * group_size, state, prefetch_last=False)
    o_t = jnp.concatenate([acc[:MLA_V] / acc[MLA_V:MLA_V + 1] for _, _, acc in state], axis=0)
    o_ref[0] = o_t.T


def attention(q_t, k, v_t):
    b, h, _, s = q_t.shape
    nc, tkc = v_t.shape[2], v_t.shape[4]
    tq = min(TQ, s)
    nh = ATTN_HEADS
    group_size = min(ATTN_GROUP, nc)
    assert group_size % 2 == 0 and nc % group_size == 0, "chunk groups alternate two buffers"
    kern = functools.partial(_attn_kernel, n_chunks=nc, tkc=tkc, group_size=group_size)
    per_head = [pltpu.VMEM((tkc, tq), jnp.float32), pltpu.VMEM((tkc, tq), jnp.float32),
                pltpu.VMEM((tkc, tq), MXU_DTYPE), pltpu.VMEM((tkc, tq), MXU_DTYPE)]
    return pl.pallas_call(
        kern, grid=(b, h // nh, s // tq),
        in_specs=[
            pl.BlockSpec((1, nh, s, HEAD_PAD), lambda bi, hi, qi: (bi, hi, 0, 0)),
            pl.BlockSpec((1, nh, HEAD_PAD, tq), lambda bi, hi, qi: (bi, hi, 0, qi)),
            pl.BlockSpec((1, nh, nc, V_ROWS, tkc), lambda bi, hi, qi: (bi, hi, 0, 0, 0)),
        ],
        out_specs=pl.BlockSpec((1, tq, nh * MLA_V), lambda bi, hi, qi: (bi, qi, hi)),
        out_shape=jax.ShapeDtypeStruct((b, s, h * MLA_V), jnp.float32),
        scratch_shapes=per_head * nh,
        compiler_params=_cparams(("parallel", "parallel", "parallel")), name="attention",
    )(k, q_t, v_t)


def _shift_rows(x, d, edge_rows, row):
    ts = x.shape[0]
    y = pltpu.roll(x, (-d) % ts, axis=0)
    if d < 0:
        return jnp.where(row == 0, edge_rows[0], y)
    for j in range(d):
        y = jnp.where(row == ts - d + j, edge_rows[j], y)
    return y


def _scan_rows(a, u, reverse):
    ts = a.shape[0]
    row = lax.broadcasted_iota(jnp.int32, a.shape, 0)
    d = 1
    while d < ts:
        if d % SUBLANES == 0:
            one = jnp.ones((d, a.shape[1]), a.dtype)
            zero = jnp.zeros((d, a.shape[1]), a.dtype)
            if reverse:
                a_sh = jnp.concatenate([a[d:], one], axis=0)
                u_sh = jnp.concatenate([u[d:], zero], axis=0)
            else:
                a_sh = jnp.concatenate([one, a[:ts - d]], axis=0)
                u_sh = jnp.concatenate([zero, u[:ts - d]], axis=0)
        else:
            if reverse:
                valid = row < ts - d
                shift = ts - d
            else:
                valid = row >= d
                shift = d
            a_sh = jnp.where(valid, pltpu.roll(a, shift, axis=0), 1.0)
            u_sh = jnp.where(valid, pltpu.roll(u, shift, axis=0), 0.0)
        u = u + a * u_sh
        a = a * a_sh
        d *= 2
    return a, u


def _lru_core(x, xp, xn, first, last, cw_ref, cb_ref, wg_ref, bg_ref, lam_ref, carry_ref, h_ref, reverse):
    ts = x.shape[0]
    row = lax.broadcasted_iota(jnp.int32, x.shape, 0)
    keep_prev = jnp.where(first, 0.0, 1.0)
    keep_next = jnp.where(last, 0.0, 1.0)
    prev_row = xp[SUBLANES - 1:SUBLANES] * keep_prev
    next0 = xn[0:1] * keep_next
    next1 = xn[1:2] * keep_next
    cw = cw_ref[...]
    xc = (cw[0:1] * _shift_rows(x, -1, [prev_row], row) + cw[1:2] * x
          + cw[2:3] * _shift_rows(x, 1, [next0], row)
          + cw[3:4] * _shift_rows(x, 2, [next0, next1], row) + cb_ref[...])
    gates = _dot(xc, wg_ref[...]) + bg_ref[...]
    rec = jax.nn.sigmoid(gates[:, :LRU_DIM])
    inp = jax.nn.sigmoid(gates[:, LRU_DIM:])
    neg_lam = -lam_ref[...]
    softplus = jnp.maximum(neg_lam, 0.0) + jnp.log(1.0 + jnp.exp(-jnp.abs(neg_lam)))
    log_a = -LRU_C * rec * softplus
    a = jnp.exp(log_a)
    u = jnp.sqrt(1.0 - a * a) * (inp * xc)
    a_cum, h0 = _scan_rows(a, u, reverse)

    @pl.when(pl.program_id(1) == 0)
    def _():
        carry_ref[...] = jnp.zeros_like(carry_ref)

    h = h0 + a_cum * carry_ref[0:1]
    h_ref[0] = h
    edge = h[0:1] if reverse else h[ts - 1:ts]
    carry_ref[...] = jnp.broadcast_to(edge, carry_ref.shape)


def _lru_fwd_kernel(x_ref, xp_ref, xn_ref, cc_ref, ccp_ref, ccn_ref, ch_ref, chp_ref, chn_ref, cbg_ref,
                    cw_ref, cb_ref, wg_ref, bg_ref, lam_ref, sw_ref, h_ref, y_ref, carry_ref):
    i = pl.program_id(1)
    first = i == 0
    last = i == pl.num_programs(1) - 1
    _lru_core(x_ref[0], xp_ref[0], xn_ref[0], first, last, cw_ref, cb_ref, wg_ref, bg_ref, lam_ref,
              carry_ref, h_ref, reverse=False)
    g = cc_ref[0] * ch_ref[0]
    row = lax.broadcasted_iota(jnp.int32, g.shape, 0)
    keep_prev = jnp.where(first, 0.0, 1.0)
    keep_next = jnp.where(last, 0.0, 1.0)
    g_prev = ccp_ref[0, SUBLANES - 1:SUBLANES] * chp_ref[0, SUBLANES - 1:SUBLANES] * keep_prev
    g_next = ccn_ref[0, 0:1] * chn_ref[0, 0:1] * keep_next
    sw = sw_ref[...]
    conv = (sw[0:1] * _shift_rows(g, -1, [g_prev], row) + sw[1:2] * g
            + sw[2:3] * _shift_rows(g, 1, [g_next], row))
    y_ref[0] = cbg_ref[0] * conv


def _lru_bwd_kernel(x_ref, xp_ref, xn_ref, cw_ref, cb_ref, wg_ref, bg_ref, lam_ref, h_ref, carry_ref):
    i = pl.program_id(1)
    nt = pl.num_programs(1)
    first = i == nt - 1
    last = i == 0
    _lru_core(x_ref[0], xp_ref[0], xn_ref[0], first, last, cw_ref, cb_ref, wg_ref, bg_ref, lam_ref,
              carry_ref, h_ref, reverse=True)


def lru_scan(z3, conv_w, conv_b, wg, bg, lam, short_w=None, reverse=False):
    b, s, _ = z3.shape
    ts = min(TS, s)
    nt = s // ts
    rb = ts // SUBLANES
    nrb = s // SUBLANES
    tile = (lambda i: nt - 1 - i) if reverse else (lambda i: i)

    def main(col):
        return pl.BlockSpec((1, ts, LRU_DIM), lambda bi, i: (bi, tile(i), col))

    def prev(col):
        return pl.BlockSpec((1, SUBLANES, LRU_DIM),
                            lambda bi, i: (bi, jnp.maximum(tile(i) * rb - 1, 0), col))

    def nxt(col):
        return pl.BlockSpec((1, SUBLANES, LRU_DIM),
                            lambda bi, i: (bi, jnp.minimum((tile(i) + 1) * rb, nrb - 1), col))

    fixed = lambda bi, i: (0, 0)
    par = [pl.BlockSpec((4, LRU_DIM), fixed), pl.BlockSpec((1, LRU_DIM), fixed),
           pl.BlockSpec((LRU_DIM, 2 * LRU_DIM), fixed), pl.BlockSpec((1, 2 * LRU_DIM), fixed),
           pl.BlockSpec((1, LRU_DIM), fixed)]
    h_shape = jax.ShapeDtypeStruct((b, s, LRU_DIM), jnp.float32)
    scratch = [pltpu.VMEM((SUBLANES, LRU_DIM), jnp.float32)]
    if reverse:
        return pl.pallas_call(
            _lru_bwd_kernel, grid=(b, nt),
            in_specs=[main(COL_LX), prev(COL_LX), nxt(COL_LX)] + par,
            out_specs=main(0), out_shape=h_shape, scratch_shapes=scratch,
            compiler_params=_cparams(("parallel", "arbitrary")), name="lru_bwd",
        )(z3, z3, z3, conv_w, conv_b, wg, bg, lam)
    return pl.pallas_call(
        _lru_fwd_kernel, grid=(b, nt),
        in_specs=[main(COL_LX), prev(COL_LX), nxt(COL_LX), main(COL_CC), prev(COL_CC), nxt(COL_CC),
                  main(COL_CH), prev(COL_CH), nxt(COL_CH), main(COL_CB)] + par
                 + [pl.BlockSpec((3, CONV_DIM), fixed)],
        out_specs=[main(0), main(0)], out_shape=[h_shape, h_shape], scratch_shapes=scratch,
        compiler_params=_cparams(("parallel", "arbitrary")), name="lru_fwd",
    )(z3, z3, z3, z3, z3, z3, z3, z3, z3, z3, conv_w, conv_b, wg, bg, lam, short_w)


def _post_mixer_body(yc_ref, o_ref, hf_ref, hb_ref, lg_ref, xn_ref, gm_ref, wo_ref, g1_ref, b1_ref):
    gm = gm_ref[...]
    y_lru = jax.nn.gelu(lg_ref[...], approximate=True) * (hf_ref[...] + hb_ref[...])
    y = jnp.concatenate([
        _rms_norm(yc_ref[...], gm[:, :CONV_DIM]),
        _rms_norm(o_ref[...], gm[:, CONV_DIM:CONV_DIM + MLA_HEADS * MLA_V]),
        _rms_norm(y_lru, gm[:, CONV_DIM + MLA_HEADS * MLA_V:]),
    ], axis=1)
    mix = _dot(y, wo_ref[...])
    return _layer_norm(DN_ALPHA * xn_ref[...] + mix, g1_ref[...], b1_ref[...])


def _post_mixer_kernel(yc_ref, o_ref, hf_ref, hb_ref, lg_ref, xn_ref, gm_ref, wo_ref, g1_ref, b1_ref,
                       x1_ref, x1b_ref):
    x1 = _post_mixer_body(yc_ref, o_ref, hf_ref, hb_ref, lg_ref, xn_ref, gm_ref, wo_ref, g1_ref, b1_ref)
    x1_ref[...] = x1
    x1b_ref[...] = x1.astype(x1b_ref.dtype)


def _post_mixer_router_kernel(yc_ref, o_ref, hf_ref, hb_ref, lg_ref, xn_ref, gm_ref, wo_ref, g1_ref,
                              b1_ref, wr_ref, x1_ref, route_ref):
    x1 = _post_mixer_body(yc_ref, o_ref, hf_ref, hb_ref, lg_ref, xn_ref, gm_ref, wo_ref, g1_ref, b1_ref)
    x1_ref[...] = x1
    wr = wr_ref[...].astype(jnp.bfloat16)
    x_hi = x1.astype(jnp.bfloat16)
    x_lo = (x1 - x_hi.astype(jnp.float32)).astype(jnp.bfloat16)
    t_hi = jnp.dot(x_hi, wr, preferred_element_type=jnp.float32)
    t_lo = jnp.dot(x_lo, wr, preferred_element_type=jnp.float32)
    logits = t_hi + t_lo + pltpu.roll(t_hi, LANES - N_EXPERTS, axis=1)
    lane = lax.broadcasted_iota(jnp.int32, logits.shape, 1)
    logits = jnp.where(lane < N_EXPERTS, logits, NEG_BIG)
    v1 = jnp.max(logits, axis=1, keepdims=True)
    i1 = jnp.min(jnp.where(logits == v1, lane, LANES), axis=1, keepdims=True)
    rest = jnp.where(lane == i1, NEG_BIG, logits)
    v2 = jnp.max(rest, axis=1, keepdims=True)
    i2 = jnp.min(jnp.where(rest == v2, lane, LANES), axis=1, keepdims=True)
    e = jnp.exp(v2 - v1)
    g_top = 1.0 / (1.0 + e)
    g_sec = e * g_top
    route_ref[...] = jnp.where(lane == 0, i1.astype(jnp.float32),
                               jnp.where(lane == 1, i2.astype(jnp.float32),
                                         jnp.where(lane == 2, g_top, jnp.where(lane == 3, g_sec, 0.0))))


def post_mixer(y_conv, o, h_f, h_b, z, xn, gm, wo, g1, b1, w_router=None):
    n = xn.shape[0]
    tm = min(TM, n)
    row = lambda i: (i, 0)
    fixed = lambda i: (0, 0)
    in_specs = [
        pl.BlockSpec((tm, CONV_DIM), row), pl.BlockSpec((tm, MLA_HEADS * MLA_V), row),
        pl.BlockSpec((tm, LRU_DIM), row), pl.BlockSpec((tm, LRU_DIM), row),
        pl.BlockSpec((tm, LRU_DIM), lambda i: (i, COL_LG)), pl.BlockSpec((tm, D_MODEL), row),
        pl.BlockSpec((1, D_MODEL), fixed), pl.BlockSpec((D_MODEL, D_MODEL), fixed),
        pl.BlockSpec((1, D_MODEL), fixed), pl.BlockSpec((1, D_MODEL), fixed),
    ]
    x_spec = pl.BlockSpec((tm, D_MODEL), row)
    x_shape = jax.ShapeDtypeStruct((n, D_MODEL), jnp.float32)
    args = (y_conv, o, h_f, h_b, z, xn, gm, wo, g1, b1)
    if w_router is None:
        return pl.pallas_call(
            _post_mixer_kernel, grid=(n // tm,), in_specs=in_specs, out_specs=[x_spec, x_spec],
            out_shape=[x_shape, jax.ShapeDtypeStruct((n, D_MODEL), MXU_DTYPE)],
            compiler_params=_cparams(("parallel",)), name="post_mixer")(*args)
    return pl.pallas_call(
        _post_mixer_router_kernel, grid=(n // tm,),
        in_specs=in_specs + [pl.BlockSpec((D_MODEL, LANES), fixed)],
        out_specs=[x_spec, pl.BlockSpec((tm, LANES), row)],
        out_shape=[x_shape, jax.ShapeDtypeStruct((n, LANES), jnp.float32)],
        compiler_params=_cparams(("parallel",)), name="post_mixer_router")(*args, w_router)


def _swiglu_chunk(xb, wg, wu, wd):
    gate = jnp.dot(xb, wg, preferred_element_type=jnp.float32)
    up = jnp.dot(xb, wu, preferred_element_type=jnp.float32)
    hidden = (jax.nn.silu(gate) * up).astype(wd.dtype)
    return jnp.dot(hidden, wd, preferred_element_type=jnp.float32)


def _ffn_dense_kernel(xb_ref, x1_ref, wg_ref, wu_ref, wd_ref, g2_ref, b2_ref, out_ref, acc_ref):
    f = pl.program_id(1)

    @pl.when(f == 0)
    def _():
        acc_ref[...] = jnp.zeros_like(acc_ref)

    acc_ref[...] += _swiglu_chunk(xb_ref[...], wg_ref[...], wu_ref[...], wd_ref[...])

    @pl.when(f == pl.num_programs(1) - 1)
    def _():
        out_ref[...] = _layer_norm(DN_ALPHA * x1_ref[...] + acc_ref[...], g2_ref[...], b2_ref[...])


def ffn_dense(x1b, x1, wg, wu, wd, g2, b2):
    n = x1.shape[0]
    tm = min(TM_FFN, n)
    tf = TF
    row = lambda i, f: (i, 0)
    fixed = lambda i, f: (0, 0)
    return pl.pallas_call(
        _ffn_dense_kernel, grid=(n // tm, D_FF // tf),
        in_specs=[pl.BlockSpec((tm, D_MODEL), row), pl.BlockSpec((tm, D_MODEL), row),
                  pl.BlockSpec((D_MODEL, tf), lambda i, f: (0, f)),
                  pl.BlockSpec((D_MODEL, tf), lambda i, f: (0, f)),
                  pl.BlockSpec((tf, D_MODEL), lambda i, f: (f, 0)),
                  pl.BlockSpec((1, D_MODEL), fixed), pl.BlockSpec((1, D_MODEL), fixed)],
        out_specs=pl.BlockSpec((tm, D_MODEL), row),
        out_shape=jax.ShapeDtypeStruct((n, D_MODEL), jnp.float32),
        scratch_shapes=[pltpu.VMEM((tm, D_MODEL), jnp.float32)],
        compiler_params=_cparams(("parallel", "arbitrary")), name="ffn_dense",
    )(x1b, x1, wg, wu, wd, g2, b2)


def _moe_dispatch_kernel(st_ref, nb_ref, x_ref, lr_ref, xs_hbm, buf_ref, sem):
    i = pl.program_id(0)
    slot = i % 2
    n_rows = buf_ref.shape[1] // SUBLANES
    lr = lr_ref[0]
    rows = lax.broadcasted_iota(jnp.int32, (n_rows, lr.shape[1]), 0)
    sel = jnp.where(jnp.logical_or(rows == lr[0:1], rows == lr[1:2]), 1.0, 0.0)
    xc = _dot(sel, x_ref[...])
    for s in range(SUBLANES):
        buf_ref[slot, pl.ds(s, n_rows, stride=SUBLANES), :] = xc[:, s * LANES:(s + 1) * LANES]

    blk = DISP_BLK * SUBLANES

    def block_copy(src, dst, sl):
        return pltpu.make_async_copy(buf_ref.at[sl, pl.ds(src, blk), :], xs_hbm.at[pl.ds(dst, blk), :],
                                     sem.at[sl])

    def wait_tile(t, sl):
        total = nb_ref[t * N_EXPERTS]
        for e in range(1, N_EXPERTS):
            total = total + nb_ref[t * N_EXPERTS + e]

        def wait(k, c):
            block_copy(0, 0, sl).wait()
            return c

        lax.fori_loop(0, total, wait, 0)

    @pl.when(i > 0)
    def _():
        wait_tile(i - 1, 1 - slot)

    off = jnp.int32(0)
    for e in range(N_EXPERTS):
        a = st_ref[i * N_EXPERTS + e]
        nb = nb_ref[i * N_EXPERTS + e]

        def start(k, c, a=a, off=off):
            block_copy(pl.multiple_of((off + k * DISP_BLK) * SUBLANES, blk),
                       pl.multiple_of((a + k * DISP_BLK) * SUBLANES, SUBLANES), slot).start()
            return c

        lax.fori_loop(0, nb, start, 0)
        off = off + nb * DISP_BLK

    @pl.when(i == pl.num_programs(0) - 1)
    def _():
        wait_tile(i, slot)


def moe_dispatch(x1, row_start, blk_count, local_row, n_sorted_rows):
    n = x1.shape[0]
    tm = local_row.shape[2]
    buf_rows = -(-(2 * tm + N_EXPERTS * (DISP_BLK - 1)) // 16) * 16
    grid_spec = pltpu.PrefetchScalarGridSpec(
        num_scalar_prefetch=2, grid=(n // tm,),
        in_specs=[pl.BlockSpec((tm, D_MODEL), lambda i, st, nb: (i, 0)),
                  pl.BlockSpec((1, 2, tm), lambda i, st, nb: (i, 0, 0))],
        out_specs=pl.BlockSpec(memory_space=pl.ANY),
        scratch_shapes=[pltpu.VMEM((2, buf_rows * SUBLANES, LANES), jnp.float32),
                        pltpu.SemaphoreType.DMA((2,))],
    )
    return pl.pallas_call(
        _moe_dispatch_kernel, grid_spec=grid_spec,
        out_shape=jax.ShapeDtypeStruct((n_sorted_rows * SUBLANES, LANES), jnp.float32),
        compiler_params=_cparams(("arbitrary",)), name="moe_dispatch",
    )(row_start, blk_count, x1, local_row)


def _moe_ffn_kernel(te_ref, tr_ref, xs_ref, wg_ref, wu_ref, wd_ref, y_ref, xb_ref, acc_ref):
    i = pl.program_id(0)
    f = pl.program_id(1)
    n_valid = tr_ref[i]
    tm = xb_ref.shape[0]

    @pl.when(jnp.logical_and(n_valid > 0, f == 0))
    def _():
        live = lax.broadcasted_iota(jnp.int32, (tm, LANES), 0) < n_valid
        for s in range(SUBLANES):
            piece = xs_ref[pl.ds(s, tm, stride=SUBLANES), :]
            xb_ref[:, s * LANES:(s + 1) * LANES] = jnp.where(live, piece, 0.0).astype(xb_ref.dtype)
        acc_ref[...] = jnp.zeros_like(acc_ref)

    @pl.when(n_valid > 0)
    def _():
        acc_ref[...] += _swiglu_chunk(xb_ref[...], wg_ref[...], wu_ref[...], wd_ref[...])

    @pl.when(f == pl.num_programs(1) - 1)
    def _():
        y_ref[...] = jnp.where(n_valid > 0, acc_ref[...], 0.0).astype(y_ref.dtype)


def moe_ffn(x_sorted, tile_expert, tile_rows, wg, wu, wd):
    n_tiles = tile_expert.shape[0]
    tm = x_sorted.shape[0] // (n_tiles * SUBLANES)
    tf = TF
    grid_spec = pltpu.PrefetchScalarGridSpec(
        num_scalar_prefetch=2, grid=(n_tiles, D_FF // tf),
        in_specs=[
            pl.BlockSpec((tm * SUBLANES, LANES), lambda i, f, te, tr: (i, 0)),
            pl.BlockSpec((None, D_MODEL, tf), lambda i, f, te, tr: (te[i], 0, f)),
            pl.BlockSpec((None, D_MODEL, tf), lambda i, f, te, tr: (te[i], 0, f)),
            pl.BlockSpec((None, tf, D_MODEL), lambda i, f, te, tr: (te[i], f, 0)),
        ],
        out_specs=pl.BlockSpec((tm, D_MODEL), lambda i, f, te, tr: (i, 0)),
        scratch_shapes=[pltpu.VMEM((tm, D_MODEL), MXU_DTYPE), pltpu.VMEM((tm, D_MODEL), jnp.float32)],
    )
    return pl.pallas_call(
        _moe_ffn_kernel, grid_spec=grid_spec,
        out_shape=jax.ShapeDtypeStruct((n_tiles * tm, D_MODEL), MXU_DTYPE),
        compiler_params=_cparams(("arbitrary", "arbitrary")), name="moe_ffn",
    )(tile_expert, tile_rows, x_sorted, wg, wu, wd)


def _moe_combine_kernel(a_ref, nb_ref, x1_ref, route_ref, y_hbm, g2_ref, b2_ref, out_ref, ybuf_ref, sem):
    i = pl.program_id(0)
    slot = i % 2

    def block_copy(src, dst, sl):
        return pltpu.make_async_copy(y_hbm.at[pl.ds(src, CMB_BLK), :], ybuf_ref.at[sl, pl.ds(dst, CMB_BLK), :],
                                     sem.at[sl])

    def fetch_tile(t, sl):
        off = jnp.int32(0)
        for e in range(N_EXPERTS):
            a = a_ref[t * N_EXPERTS + e]
            nb = nb_ref[t * N_EXPERTS + e]

            def start(k, c, a=a, off=off):
                block_copy(pl.multiple_of(a + k * CMB_BLK, CMB_BLK), pl.multiple_of(off + k * CMB_BLK, CMB_BLK),
                           sl).start()
                return c

            lax.fori_loop(0, nb, start, 0)
            off = off + nb * CMB_BLK

    @pl.when(i == 0)
    def _():
        ybuf_ref[...] = jnp.zeros_like(ybuf_ref)
        fetch_tile(i, slot)

    total = nb_ref[i * N_EXPERTS]
    for e in range(1, N_EXPERTS):
        total = total + nb_ref[i * N_EXPERTS + e]

    def wait(k, c):
        block_copy(0, 0, slot).wait()
        return c

    lax.fori_loop(0, total, wait, 0)

    @pl.when(i + 1 < pl.num_programs(0))
    def _():
        fetch_tile(i + 1, 1 - slot)

    route = route_ref[...]
    col = lax.broadcasted_iota(jnp.int32, (route.shape[0], ybuf_ref.shape[1]), 1)
    sel = (jnp.where(col == route[:, 4:5].astype(jnp.int32), route[:, 2:3], 0.0)
           + jnp.where(col == route[:, 5:6].astype(jnp.int32), route[:, 3:4], 0.0))
    f = jnp.dot(sel.astype(ybuf_ref.dtype), ybuf_ref[slot], preferred_element_type=jnp.float32)
    out_ref[...] = _layer_norm(DN_ALPHA * x1_ref[...] + f, g2_ref[...], b2_ref[...])


def moe_combine(blk_start, blk_count, x1, route, y_sorted, g2, b2):
    n = x1.shape[0]
    tm = min(TM_CMB, n)
    buf_rows = -(-(2 * tm + N_EXPERTS * 2 * (CMB_BLK - 1)) // 256) * 256
    row = lambda i, a, nb: (i, 0)
    fixed = lambda i, a, nb: (0, 0)
    grid_spec = pltpu.PrefetchScalarGridSpec(
        num_scalar_prefetch=2, grid=(n // tm,),
        in_specs=[pl.BlockSpec((tm, D_MODEL), row), pl.BlockSpec((tm, LANES), row),
                  pl.BlockSpec(memory_space=pl.ANY),
                  pl.BlockSpec((1, D_MODEL), fixed), pl.BlockSpec((1, D_MODEL), fixed)],
        out_specs=pl.BlockSpec((tm, D_MODEL), row),
        scratch_shapes=[pltpu.VMEM((2, buf_rows, D_MODEL), y_sorted.dtype), pltpu.SemaphoreType.DMA((2,))],
    )
    return pl.pallas_call(
        _moe_combine_kernel, grid_spec=grid_spec,
        out_shape=jax.ShapeDtypeStruct((n, D_MODEL), jnp.float32),
        compiler_params=_cparams(("arbitrary",)), name="moe_combine",
    )(blk_start, blk_count, x1, route, y_sorted, g2, b2)


def _dispatch_plan(expert_idx, tm, tm_tok):
    n = expert_idx.shape[0]
    e_flat = expert_idx.reshape(-1)
    onehot = (e_flat[:, None] == jnp.arange(N_EXPERTS, dtype=jnp.int32)[None, :]).astype(jnp.int32)
    csum = jnp.cumsum(onehot, axis=0)
    before = csum - onehot
    rank = jnp.sum(before * onehot, axis=1)
    counts = csum[-1]
    tiles_per = (counts + (DISP_BLK - 1) + tm - 1) // tm
    tile_end = jnp.cumsum(tiles_per)
    tile_begin = tile_end - tiles_per
    group_start = tile_begin * tm
    pos_flat = jnp.sum(onehot * group_start[None, :], axis=1) + rank
    n_tiles = (2 * n) // tm + N_EXPERTS + 1
    t = jnp.arange(n_tiles, dtype=jnp.int32)
    last_valid = jnp.maximum(tile_end[-1] - 1, 0)
    t_eff = jnp.minimum(t, last_valid)
    tile_expert = jnp.minimum(jnp.sum((t_eff[:, None] >= tile_end[None, :]).astype(jnp.int32), axis=1),
                              N_EXPERTS - 1).astype(jnp.int32)
    of_tile = (tile_expert[:, None] == jnp.arange(N_EXPERTS, dtype=jnp.int32)[None, :]).astype(jnp.int32)
    rows_left = jnp.sum(of_tile * (counts - (t[:, None] - tile_begin[None, :]) * tm), axis=1)
    tile_rows = jnp.where(t < tile_end[-1], jnp.clip(rows_left, 0, tm), 0).astype(jnp.int32)

    first = before[::2 * tm_tok]
    cnt = jnp.concatenate([first[1:], counts[None, :]], axis=0) - first
    start = group_start[None, :] + first
    pair_onehot = onehot.reshape(-1, 2 * tm_tok, N_EXPERTS)

    d_count = (cnt + DISP_BLK - 1) // DISP_BLK
    d_off = (jnp.cumsum(d_count, axis=1) - d_count) * DISP_BLK
    d_local = rank + jnp.sum(pair_onehot * (d_off - first)[:, None, :], axis=2).reshape(-1)
    d_local = d_local.reshape(-1, tm_tok, 2).transpose(0, 2, 1)

    blk_start = (start // CMB_BLK) * CMB_BLK
    blk_count = jnp.where(cnt > 0, (start - blk_start + cnt + CMB_BLK - 1) // CMB_BLK, 0)
    buf_off = (jnp.cumsum(blk_count, axis=1) - blk_count) * CMB_BLK
    local = pos_flat + jnp.sum(pair_onehot * (buf_off - blk_start)[:, None, :], axis=2).reshape(-1)
    flat = lambda v: v.reshape(-1).astype(jnp.int32)
    return dict(n_tiles=n_tiles, tile_expert=tile_expert, tile_rows=tile_rows,
                row_start=flat(start), d_count=flat(d_count), d_local=d_local.astype(jnp.int32),
                blk_start=flat(blk_start), blk_count=flat(blk_count), c_local=local.reshape(n, 2))


def _pad_w_in(w):
    z64 = jnp.zeros((D_MODEL, KR_X1), w.dtype)
    z32 = jnp.zeros((D_MODEL, LANES - KR_X1 - MLA_ROPE), w.dtype)
    return jnp.concatenate([w[:, 0:768], w[:, 1152:1408], w[:, 1440:1696], w[:, 1696:1952],
                            w[:, 768:1152], z64, w[:, 1408:1440], z32], axis=1).astype(MXU_DTYPE)


def _rope_tables(s):
    pos = jnp.arange(s, dtype=jnp.float32)
    inv = ROPE_THETA ** (-jnp.arange(0, MLA_ROPE, 2, dtype=jnp.float32) / MLA_ROPE)
    ang = pos[:, None] * inv[None, :]
    cos, sin = jnp.cos(ang), jnp.sin(ang)
    zl = jnp.zeros((s, KR_X1), jnp.float32)
    zh = jnp.zeros((s, HALF_ROPE), jnp.float32)
    zr = jnp.zeros((s, LANES - KR_X1 - MLA_ROPE), jnp.float32)
    kc = jnp.concatenate([zl, cos, cos, zr], axis=1)
    ksa = jnp.concatenate([zl, -sin, zh, zr], axis=1)
    ksb = jnp.concatenate([zl, zh, sin, zr], axis=1)
    return cos.T, sin.T, kc, ksa, ksb


def _block_diag(w):
    nb, bw, _ = w.shape
    out = jnp.zeros((nb * bw, nb * bw), w.dtype)
    for i in range(nb):
        out = out.at[i * bw:(i + 1) * bw, i * bw:(i + 1) * bw].set(w[i])
    return out


def kernel(x, ln_in_g, ln_in_b, w_in, conv_w, q_norm_g, w_uq, kv_norm_g, w_ukv, lru_conv_w, lru_conv_b,
           lru_wa, lru_ba, lru_wi, lru_bi, lru_lam, mix_norm_g, w_out, ln1_g, ln1_b, dense_w_gate,
           dense_w_up, dense_w_down, moe_w_router, moe_w_gate, moe_w_up, moe_w_down, ln2_g, ln2_b):
    b, s, d = x.shape
    n = b * s
    rope = _rope_tables(s)
    vec = lambda v: v.reshape(1, -1)
    cur = x.reshape(n, d)
    for l in range(DEPTH):
        w_in_pad = _pad_w_in(w_in[l])
        wq_t = jnp.pad(w_uq[l].T.reshape(MLA_HEADS, MLA_NOPE + MLA_ROPE, MLA_Q_RANK),
                       ((0, 0), (0, HEAD_PAD - MLA_NOPE - MLA_ROPE), (0, 0))
                       ).reshape(MLA_HEADS * HEAD_PAD, MLA_Q_RANK).astype(MXU_DTYPE)
        wkv = w_ukv[l].reshape(MLA_KV_RANK, MLA_HEADS, MLA_NOPE + MLA_V)
        wk_pad = jnp.pad(wkv[:, :, :MLA_NOPE], ((0, 0), (0, 0), (0, HEAD_PAD - MLA_NOPE))
                         ).reshape(MLA_KV_RANK, MLA_HEADS * HEAD_PAD).astype(MXU_DTYPE)
        wv_t = wkv[:, :, MLA_NOPE:].reshape(MLA_KV_RANK, MLA_HEADS * MLA_V).T.astype(MXU_DTYPE)
        gate_w = [jnp.concatenate([_block_diag(lru_wa[l, dr]), _block_diag(lru_wi[l, dr])], axis=1
                                  ).astype(MXU_DTYPE) for dr in range(2)]
        gate_b = [jnp.concatenate([lru_ba[l, dr], lru_bi[l, dr]]).reshape(1, -1) for dr in range(2)]

        if l == 0:
            xn, z = in_proj(cur, w_in_pad, ln=(vec(ln_in_g), vec(ln_in_b)))
        else:
            xn, z = in_proj(cur, w_in_pad)
        z3 = z.reshape(b, s, D_IN_PAD)
        q_t, k, v_t = mla_proj(z3, vec(q_norm_g[l]), vec(kv_norm_g[l]), wq_t, wk_pad, wv_t, rope)
        o = attention(q_t, k, v_t).reshape(n, MLA_HEADS * MLA_V)
        h_f, y_conv = lru_scan(z3, lru_conv_w[l], vec(lru_conv_b[l]), gate_w[0], gate_b[0],
                               vec(lru_lam[l, 0]), short_w=conv_w[l])
        h_b = lru_scan(z3, lru_conv_w[l], vec(lru_conv_b[l]), gate_w[1], gate_b[1],
                       vec(lru_lam[l, 1]), reverse=True)
        mixer_args = (y_conv.reshape(n, CONV_DIM), o, h_f.reshape(n, LRU_DIM), h_b.reshape(n, LRU_DIM),
                      z, xn, vec(mix_norm_g[l]), w_out[l].astype(MXU_DTYPE), vec(ln1_g[l]), vec(ln1_b[l]))

        j = l // 2
        if l % 2 == 0:
            x1, x1b = post_mixer(*mixer_args)
            cur = ffn_dense(x1b, x1, dense_w_gate[j].astype(MXU_DTYPE), dense_w_up[j].astype(MXU_DTYPE),
                            dense_w_down[j].astype(MXU_DTYPE), vec(ln2_g[l]), vec(ln2_b[l]))
        else:
            wr_hi = moe_w_router[j].astype(jnp.bfloat16).astype(jnp.float32)
            wr_lo = (moe_w_router[j] - wr_hi).astype(jnp.bfloat16).astype(jnp.float32)
            w_router = jnp.pad(jnp.concatenate([wr_hi, wr_lo], axis=1), ((0, 0), (0, LANES - 2 * N_EXPERTS)))
            x1, route = post_mixer(*mixer_args, w_router=w_router)
            expert_idx = route[:, 0:2].astype(jnp.int32)
            tm = min(TM_FFN, n)
            plan = _dispatch_plan(expert_idx, tm, min(TM_CMB, n))
            x_sorted = moe_dispatch(x1, plan["row_start"], plan["d_count"], plan["d_local"],
                                    plan["n_tiles"] * tm)
            y_sorted = moe_ffn(x_sorted, plan["tile_expert"], plan["tile_rows"],
                               moe_w_gate[j].astype(MXU_DTYPE), moe_w_up[j].astype(MXU_DTYPE),
                               moe_w_down[j].astype(MXU_DTYPE))
            route = jnp.concatenate([route[:, :4], plan["c_local"].astype(jnp.float32), route[:, 6:]], axis=1)
            cur = moe_combine(plan["blk_start"], plan["blk_count"], x1, route, y_sorted,
                              vec(ln2_g[l]), vec(ln2_b[l]))
    return cur.reshape(b, s, d)
```

```python
import functools

import jax
import jax.numpy as jnp
from jax import lax
from jax.experimental import pallas as pl
from jax.experimental.pallas import tpu as pltpu

D_MODEL = 1024
DEPTH = 2
CONV_DIM = 256
MLA_HEADS = 8
MLA_NOPE = 64
MLA_ROPE = 32
MLA_V = 64
MLA_Q_RANK = 384
MLA_KV_RANK = 256
LRU_DIM = 256
LRU_C = 8.0
ROPE_THETA = 10000.0
D_FF = 3584
N_EXPERTS = 8
DN_ALPHA = (2.0 * DEPTH) ** 0.25
LN_EPS = 1e-5
RMS_EPS = 1e-6

D_IN_PAD = 2048
HALF_ROPE = MLA_ROPE // 2
HEAD_PAD = 128
COL_CB, COL_CC, COL_CH, COL_LG, COL_LX = 0, 1, 2, 3, 4
D_Z = 1280
KR_X1 = 64
V_ROWS = MLA_V + 16

LANES = 128
SUBLANES = 8
VMEM_LIMIT = 56 * 1024 * 1024
MXU_DTYPE = jnp.bfloat16

TM = 1024
TS = 512
TQ = 512
ATTN_GROUP = 2
ATTN_HEADS = 4
TM_FFN = 1024
TF = 512
TM_CMB = 256
CMB_BLK = 16
DISP_BLK = 8
NEG_INF = float("-inf")
LOG2E = 1.4426950408889634


def _cparams(sem, vmem=VMEM_LIMIT, flags=None):
    return pltpu.CompilerParams(dimension_semantics=sem, vmem_limit_bytes=vmem, flags=flags)


def _layer_norm(x, g, b):
    mu = jnp.mean(x, axis=-1, keepdims=True)
    xc = x - mu
    var = jnp.mean(xc * xc, axis=-1, keepdims=True)
    return xc * lax.rsqrt(var + LN_EPS) * g + b


def _rms_norm(x, g):
    ms = jnp.mean(x * x, axis=-1, keepdims=True)
    return x * lax.rsqrt(ms + RMS_EPS) * g


def _dot(a, b):
    return jnp.dot(a.astype(MXU_DTYPE), b.astype(MXU_DTYPE), preferred_element_type=jnp.float32)


def _dot_nt(a, b):
    return lax.dot_general(a.astype(MXU_DTYPE), b.astype(MXU_DTYPE), (((1,), (1,)), ((), ())),
                           preferred_element_type=jnp.float32)


def _mla_outputs(ckv, cq, kr, gq_ref, gkv_ref, wq_ref, wk_ref, wv_ref, cos_ref, sin_ref, kc_ref, ksa_ref,
                 ksb_ref, q_out, k_out, v_out):
    cqn = _rms_norm(cq, gq_ref[...])
    ckvn = _rms_norm(ckv, gkv_ref[...])

    krope = (kr * kc_ref[...]
             + pltpu.roll(kr, LANES - HALF_ROPE, axis=1) * ksa_ref[...]
             + pltpu.roll(kr, HALF_ROPE, axis=1) * ksb_ref[...])
    k_all = _dot(ckvn, wk_ref[...])
    for h in range(MLA_HEADS):
        k_out[0, h] = (k_all[:, h * HEAD_PAD:(h + 1) * HEAD_PAD] + krope).astype(k_out.dtype)

    v_all = _dot_nt(wv_ref[...], ckvn)
    ones = jnp.ones((V_ROWS - MLA_V, v_all.shape[1]), jnp.float32)
    for h in range(MLA_HEADS):
        v_out[0, h, 0] = jnp.concatenate([v_all[h * MLA_V:(h + 1) * MLA_V], ones], axis=0).astype(v_out.dtype)

    q_all = _dot_nt(wq_ref[...], cqn)
    cos_t = cos_ref[...]
    sin_t = sin_ref[...]
    qscale = (MLA_NOPE + MLA_ROPE) ** -0.5 * LOG2E
    for h in range(MLA_HEADS):
        base = h * HEAD_PAD
        nope = q_all[base:base + MLA_NOPE]
        x1 = q_all[base + MLA_NOPE:base + MLA_NOPE + HALF_ROPE]
        x2 = q_all[base + MLA_NOPE + HALF_ROPE:base + MLA_NOPE + MLA_ROPE]
        zero = q_all[base + MLA_NOPE + MLA_ROPE:base + HEAD_PAD]
        qh = jnp.concatenate([nope, x1 * cos_t - x2 * sin_t, x2 * cos_t + x1 * sin_t, zero], axis=0)
        q_out[0, h] = (qh * qscale).astype(q_out.dtype)


def _project(xn, w_ref, z_ref, mla_refs):
    z = _dot(xn, w_ref[...])
    z_ref[0] = z[:, :D_Z]
    _mla_outputs(z[:, D_Z:D_Z + MLA_KV_RANK], z[:, D_Z + MLA_KV_RANK:D_Z + MLA_KV_RANK + MLA_Q_RANK],
                 z[:, D_IN_PAD - LANES:], *mla_refs)


def _in_proj_ln_kernel(x_ref, g_ref, b_ref, w_ref, *refs):
    mla_in, (xn_ref, z_ref), mla_out = refs[:10], refs[10:12], refs[12:]
    xn = _layer_norm(x_ref[0], g_ref[...], b_ref[...])
    xn_ref[0] = xn
    _project(xn, w_ref, z_ref, mla_in + mla_out)


def _in_proj_kernel(x_ref, w_ref, *refs):
    mla_in, z_ref, mla_out = refs[:10], refs[10], refs[11:]
    _project(x_ref[0], w_ref, z_ref, mla_in + mla_out)


def in_proj(x3, w_pad, gq, gkv, wq_t, wk_pad, wv_t, rope, ln=None):
    b, s, _ = x3.shape
    ts = min(TS, s)
    nt = s // ts
    cos_t, sin_t, kc, ksa, ksb = rope
    fixed = lambda bi, i: (0, 0)
    tile = lambda bi, i: (bi, i, 0)
    x_spec = pl.BlockSpec((1, ts, D_MODEL), tile)
    vec = pl.BlockSpec((1, D_MODEL), fixed)
    mla_specs = [
        pl.BlockSpec((1, MLA_Q_RANK), fixed),
        pl.BlockSpec((1, MLA_KV_RANK), fixed),
        pl.BlockSpec((MLA_HEADS * HEAD_PAD, MLA_Q_RANK), fixed),
        pl.BlockSpec((MLA_KV_RANK, MLA_HEADS * HEAD_PAD), fixed),
        pl.BlockSpec((MLA_HEADS * MLA_V, MLA_KV_RANK), fixed),
        pl.BlockSpec((HALF_ROPE, ts), lambda bi, i: (0, i)),
        pl.BlockSpec((HALF_ROPE, ts), lambda bi, i: (0, i)),
        pl.BlockSpec((ts, LANES), lambda bi, i: (i, 0)),
        pl.BlockSpec((ts, LANES), lambda bi, i: (i, 0)),
        pl.BlockSpec((ts, LANES), lambda bi, i: (i, 0)),
    ]
    mla_args = (gq, gkv, wq_t, wk_pad, wv_t, cos_t, sin_t, kc, ksa, ksb)
    w_spec = pl.BlockSpec((D_MODEL, D_IN_PAD), fixed)
    z_spec = pl.BlockSpec((1, ts, D_Z), tile)
    z_shape = jax.ShapeDtypeStruct((b, s, D_Z), jnp.float32)
    qkv_specs = [
        pl.BlockSpec((1, MLA_HEADS, HEAD_PAD, ts), lambda bi, i: (bi, 0, 0, i)),
        pl.BlockSpec((1, MLA_HEADS, ts, HEAD_PAD), lambda bi, i: (bi, 0, i, 0)),
        pl.BlockSpec((1, MLA_HEADS, 1, V_ROWS, ts), lambda bi, i: (bi, 0, i, 0, 0)),
    ]
    qkv_shape = [
        jax.ShapeDtypeStruct((b, MLA_HEADS, HEAD_PAD, s), MXU_DTYPE),
        jax.ShapeDtypeStruct((b, MLA_HEADS, s, HEAD_PAD), MXU_DTYPE),
        jax.ShapeDtypeStruct((b, MLA_HEADS, nt, V_ROWS, ts), MXU_DTYPE),
    ]
    if ln is None:
        z, q_t, k, v_t = pl.pallas_call(
            _in_proj_kernel, grid=(b, nt), in_specs=[x_spec, w_spec] + mla_specs,
            out_specs=[z_spec] + qkv_specs, out_shape=[z_shape] + qkv_shape,
            compiler_params=_cparams(("parallel", "parallel")), name="in_proj")(x3, w_pad, *mla_args)
        return x3, z, q_t, k, v_t
    g, bb = ln
    xn, z, q_t, k, v_t = pl.pallas_call(
        _in_proj_ln_kernel, grid=(b, nt), in_specs=[x_spec, vec, vec, w_spec] + mla_specs,
        out_specs=[x_spec, z_spec] + qkv_specs,
        out_shape=[jax.ShapeDtypeStruct((b, s, D_MODEL), jnp.float32), z_shape] + qkv_shape,
        compiler_params=_cparams(("parallel", "parallel")), name="in_proj_ln")(x3, g, bb, w_pad, *mla_args)
    return xn, z, q_t, k, v_t


def _attn_kernel(k_ref, q_ref, v_ref, o_ref, *scratch, n_chunks, tkc, group_size):
    n_heads = q_ref.shape[1]
    tq = q_ref.shape[3]
    s_bufs = [scratch[4 * hd:4 * hd + 2] for hd in range(n_heads)]
    p_bufs = [scratch[4 * hd + 2:4 * hd + 4] for hd in range(n_heads)]

    def scores(hd, c, s_ref):
        start = pl.multiple_of(c * tkc, tkc)
        s = jnp.dot(k_ref[0, hd, pl.ds(start, tkc), :], q_ref[0, hd], preferred_element_type=jnp.float32)
        s_ref[...] = s
        return jnp.max(s, axis=0, keepdims=True)

    def accumulate(hd, c, s_ref, p_ref, mx, m, acc):
        m_new = jnp.maximum(m, mx)
        alpha = jnp.exp2(m - m_new)
        p_ref[...] = jnp.exp2(s_ref[...] - m_new).astype(p_ref.dtype)
        pv = jnp.dot(v_ref[0, hd, c], p_ref[...], preferred_element_type=jnp.float32)
        return m_new, alpha * acc + pv

    def group(c0, state, prefetch_last):
        state = list(state)
        for g in range(group_size):
            for hd in range(n_heads):
                mx, m, acc = state[hd]
                mx_next = None
                if g + 1 < group_size or prefetch_last:
                    mx_next = scores(hd, c0 + g + 1, s_bufs[hd][(g + 1) % 2])
                m, acc = accumulate(hd, c0 + g, s_bufs[hd][g % 2], p_bufs[hd][g % 2], mx, m, acc)
                state[hd] = (mx_next, m, acc)
        return tuple(state)

    def body(j, state):
        return group(j * group_size, state, prefetch_last=True)

    state = tuple((scores(hd, 0, s_bufs[hd][0]), jnp.full((1, tq), NEG_INF, jnp.float32),
                   jnp.zeros((v_ref.shape[3], tq), jnp.float32)) for hd in range(n_heads))
    n_groups = n_chunks // group_size
    state = lax.fori_loop(0, n_groups - 1, body, state)
    state = group((n_groups - 1) * group_size, state, prefetch_last=False)
    o_t = jnp.concatenate([acc[:MLA_V] / acc[MLA_V:MLA_V + 1] for _, _, acc in state], axis=0)
    o_ref[0] = o_t.T


def attention(q_t, k, v_t):
    b, h, _, s = q_t.shape
    nc, tkc = v_t.shape[2], v_t.shape[4]
    tq = min(TQ, s)
    nh = ATTN_HEADS
    group_size = min(ATTN_GROUP, nc)
    assert group_size % 2 == 0 and nc % group_size == 0, "chunk groups alternate two buffers"
    kern = functools.partial(_attn_kernel, n_chunks=nc, tkc=tkc, group_size=group_size)
    per_head = [pltpu.VMEM((tkc, tq), jnp.float32), pltpu.VMEM((tkc, tq), jnp.float32),
                pltpu.VMEM((tkc, tq), MXU_DTYPE), pltpu.VMEM((tkc, tq), MXU_DTYPE)]
    return pl.pallas_call(
        kern, grid=(b, h // nh, s // tq),
        in_specs=[
            pl.BlockSpec((1, nh, s, HEAD_PAD), lambda bi, hi, qi: (bi, hi, 0, 0)),
            pl.BlockSpec((1, nh, HEAD_PAD, tq), lambda bi, hi, qi: (bi, hi, 0, qi)),
            pl.BlockSpec((1, nh, nc, V_ROWS, tkc), lambda bi, hi, qi: (bi, hi, 0, 0, 0)),
        ],
        out_specs=pl.BlockSpec((1, tq, nh * MLA_V), lambda bi, hi, qi: (bi, qi, hi)),
        out_shape=jax.ShapeDtypeStruct((b, s, h * MLA_V), jnp.float32),
        scratch_shapes=per_head * nh,
        compiler_params=_cparams(("parallel", "parallel", "parallel")), name="attention",
    )(k, q_t, v_t)


def _shift_rows(x, d, edge_rows, row):
    ts = x.shape[0]
    y = pltpu.roll(x, (-d) % ts, axis=0)
    if d < 0:
        return jnp.where(row == 0, edge_rows[0], y)
    for j in range(d):
        y = jnp.where(row == ts - d + j, edge_rows[j], y)
    return y


def _scan_rows(a, u, reverse):
    ts = a.shape[0]
    row = lax.broadcasted_iota(jnp.int32, a.shape, 0)
    d = 1
    while d < ts:
        if reverse:
            valid = row < ts - d
            shift = ts - d
        else:
            valid = row >= d
            shift = d
        a_sh = jnp.where(valid, pltpu.roll(a, shift, axis=0), 1.0)
        u_sh = jnp.where(valid, pltpu.roll(u, shift, axis=0), 0.0)
        u = u + a * u_sh
        a = a * a_sh
        d *= 2
    return a, u


def _lru_core(x, xp, xn, first, last, cw_ref, cb_ref, wg_ref, bg_ref, lam_ref, carry_ref, h_ref, reverse):
    ts = x.shape[0]
    row = lax.broadcasted_iota(jnp.int32, x.shape, 0)
    keep_prev = jnp.where(first, 0.0, 1.0)
    keep_next = jnp.where(last, 0.0, 1.0)
    prev_row = xp[SUBLANES - 1:SUBLANES] * keep_prev
    next0 = xn[0:1] * keep_next
    next1 = xn[1:2] * keep_next
    cw = cw_ref[...]
    xc = (cw[0:1] * _shift_rows(x, -1, [prev_row], row) + cw[1:2] * x
          + cw[2:3] * _shift_rows(x, 1, [next0], row)
          + cw[3:4] * _shift_rows(x, 2, [next0, next1], row) + cb_ref[...])
    gates = _dot(xc, wg_ref[...]) + bg_ref[...]
    rec = jax.nn.sigmoid(gates[:, :LRU_DIM])
    inp = jax.nn.sigmoid(gates[:, LRU_DIM:])
    neg_lam = -lam_ref[...]
    softplus = jnp.maximum(neg_lam, 0.0) + jnp.log(1.0 + jnp.exp(-jnp.abs(neg_lam)))
    log_a = -LRU_C * rec * softplus
    a = jnp.exp(log_a)
    u = jnp.sqrt(1.0 - a * a) * (inp * xc)
    a_cum, h0 = _scan_rows(a, u, reverse)

    @pl.when(pl.program_id(1) == 0)
    def _():
        carry_ref[...] = jnp.zeros_like(carry_ref)

    h = h0 + a_cum * carry_ref[0:1]
    h_ref[0] = h
    edge = h[0:1] if reverse else h[ts - 1:ts]
    carry_ref[...] = jnp.broadcast_to(edge, carry_ref.shape)


def _lru_fwd_kernel(x_ref, xp_ref, xn_ref, cc_ref, ccp_ref, ccn_ref, ch_ref, chp_ref, chn_ref, cbg_ref,
                    cw_ref, cb_ref, wg_ref, bg_ref, lam_ref, sw_ref, h_ref, y_ref, carry_ref):
    i = pl.program_id(1)
    first = i == 0
    last = i == pl.num_programs(1) - 1
    _lru_core(x_ref[0], xp_ref[0], xn_ref[0], first, last, cw_ref, cb_ref, wg_ref, bg_ref, lam_ref,
              carry_ref, h_ref, reverse=False)
    g = cc_ref[0] * ch_ref[0]
    row = lax.broadcasted_iota(jnp.int32, g.shape, 0)
    keep_prev = jnp.where(first, 0.0, 1.0)
    keep_next = jnp.where(last, 0.0, 1.0)
    g_prev = ccp_ref[0, SUBLANES - 1:SUBLANES] * chp_ref[0, SUBLANES - 1:SUBLANES] * keep_prev
    g_next = ccn_ref[0, 0:1] * chn_ref[0, 0:1] * keep_next
    sw = sw_ref[...]
    conv = (sw[0:1] * _shift_rows(g, -1, [g_prev], row) + sw[1:2] * g
            + sw[2:3] * _shift_rows(g, 1, [g_next], row))
    y_ref[0] = cbg_ref[0] * conv


def _lru_bwd_kernel(x_ref, xp_ref, xn_ref, cw_ref, cb_ref, wg_ref, bg_ref, lam_ref, h_ref, carry_ref):
    i = pl.program_id(1)
    nt = pl.num_programs(1)
    first = i == nt - 1
    last = i == 0
    _lru_core(x_ref[0], xp_ref[0], xn_ref[0], first, last, cw_ref, cb_ref, wg_ref, bg_ref, lam_ref,
              carry_ref, h_ref, reverse=True)


def lru_scan(z3, conv_w, conv_b, wg, bg, lam, short_w=None, reverse=False):
    b, s, _ = z3.shape
    ts = min(TS, s)
    nt = s // ts
    rb = ts // SUBLANES
    nrb = s // SUBLANES
    tile = (lambda i: nt - 1 - i) if reverse else (lambda i: i)

    def main(col):
        return pl.BlockSpec((1, ts, LRU_DIM), lambda bi, i: (bi, tile(i), col))

    def prev(col):
        return pl.BlockSpec((1, SUBLANES, LRU_DIM),
                            lambda bi, i: (bi, jnp.maximum(tile(i) * rb - 1, 0), col))

    def nxt(col):
        return pl.BlockSpec((1, SUBLANES, LRU_DIM),
                            lambda bi, i: (bi, jnp.minimum((tile(i) + 1) * rb, nrb - 1), col))

    fixed = lambda bi, i: (0, 0)
    par = [pl.BlockSpec((4, LRU_DIM), fixed), pl.BlockSpec((1, LRU_DIM), fixed),
           pl.BlockSpec((LRU_DIM, 2 * LRU_DIM), fixed), pl.BlockSpec((1, 2 * LRU_DIM), fixed),
           pl.BlockSpec((1, LRU_DIM), fixed)]
    h_shape = jax.ShapeDtypeStruct((b, s, LRU_DIM), jnp.float32)
    scratch = [pltpu.VMEM((SUBLANES, LRU_DIM), jnp.float32)]
    if reverse:
        return pl.pallas_call(
            _lru_bwd_kernel, grid=(b, nt),
            in_specs=[main(COL_LX), prev(COL_LX), nxt(COL_LX)] + par,
            out_specs=main(0), out_shape=h_shape, scratch_shapes=scratch,
            compiler_params=_cparams(("parallel", "arbitrary")), name="lru_bwd",
        )(z3, z3, z3, conv_w, conv_b, wg, bg, lam)
    return pl.pallas_call(
        _lru_fwd_kernel, grid=(b, nt),
        in_specs=[main(COL_LX), prev(COL_LX), nxt(COL_LX), main(COL_CC), prev(COL_CC), nxt(COL_CC),
                  main(COL_CH), prev(COL_CH), nxt(COL_CH), main(COL_CB)] + par
                 + [pl.BlockSpec((3, CONV_DIM), fixed)],
        out_specs=[main(0), main(0)], out_shape=[h_shape, h_shape], scratch_shapes=scratch,
        compiler_params=_cparams(("parallel", "arbitrary")), name="lru_fwd",
    )(z3, z3, z3, z3, z3, z3, z3, z3, z3, z3, conv_w, conv_b, wg, bg, lam, short_w)


def _post_mixer_body(yc_ref, o_ref, hf_ref, hb_ref, lg_ref, xn_ref, gm_ref, wo_ref, g1_ref, b1_ref):
    gm = gm_ref[...]
    y_lru = jax.nn.gelu(lg_ref[...], approximate=True) * (hf_ref[...] + hb_ref[...])
    y = jnp.concatenate([
        _rms_norm(yc_ref[...], gm[:, :CONV_DIM]),
        _rms_norm(o_ref[...], gm[:, CONV_DIM:CONV_DIM + MLA_HEADS * MLA_V]),
        _rms_norm(y_lru, gm[:, CONV_DIM + MLA_HEADS * MLA_V:]),
    ], axis=1)
    mix = _dot(y, wo_ref[...])
    return _layer_norm(DN_ALPHA * xn_ref[...] + mix, g1_ref[...], b1_ref[...])


def _post_mixer_kernel(yc_ref, o_ref, hf_ref, hb_ref, lg_ref, xn_ref, gm_ref, wo_ref, g1_ref, b1_ref,
                       x1_ref):
    x1_ref[...] = _post_mixer_body(yc_ref, o_ref, hf_ref, hb_ref, lg_ref, xn_ref, gm_ref, wo_ref, g1_ref,
                                   b1_ref)


def _post_mixer_router_kernel(yc_ref, o_ref, hf_ref, hb_ref, lg_ref, xn_ref, gm_ref, wo_ref, g1_ref,
                              b1_ref, wr_ref, x1_ref, route_ref):
    x1 = _post_mixer_body(yc_ref, o_ref, hf_ref, hb_ref, lg_ref, xn_ref, gm_ref, wo_ref, g1_ref, b1_ref)
    x1_ref[...] = x1
    wr = wr_ref[...].astype(jnp.bfloat16)
    x_hi = x1.astype(jnp.bfloat16)
    x_lo = (x1 - x_hi.astype(jnp.float32)).astype(jnp.bfloat16)
    t_hi = jnp.dot(x_hi, wr, preferred_element_type=jnp.float32)
    t_lo = jnp.dot(x_lo, wr, preferred_element_type=jnp.float32)
    logits = t_hi + t_lo + pltpu.roll(t_hi, LANES - N_EXPERTS, axis=1)
    lane = lax.broadcasted_iota(jnp.int32, logits.shape, 1)
    logits = jnp.where(lane < N_EXPERTS, logits, NEG_INF)
    v1 = jnp.max(logits, axis=1, keepdims=True)
    i1 = jnp.min(jnp.where(logits == v1, lane, LANES), axis=1, keepdims=True)
    rest = jnp.where(lane == i1, NEG_INF, logits)
    v2 = jnp.max(rest, axis=1, keepdims=True)
    i2 = jnp.min(jnp.where(rest == v2, lane, LANES), axis=1, keepdims=True)
    e = jnp.exp(v2 - v1)
    g_top = 1.0 / (1.0 + e)
    g_sec = e * g_top
    route_ref[...] = jnp.where(lane == 0, i1.astype(jnp.float32),
                               jnp.where(lane == 1, i2.astype(jnp.float32),
                                         jnp.where(lane == 2, g_top, jnp.where(lane == 3, g_sec, 0.0))))


def post_mixer(y_conv, o, h_f, h_b, z, xn, gm, wo, g1, b1, w_router=None):
    n = xn.shape[0]
    tm = min(TM, n)
    row = lambda i: (i, 0)
    fixed = lambda i: (0, 0)
    in_specs = [
        pl.BlockSpec((tm, CONV_DIM), row), pl.BlockSpec((tm, MLA_HEADS * MLA_V), row),
        pl.BlockSpec((tm, LRU_DIM), row), pl.BlockSpec((tm, LRU_DIM), row),
        pl.BlockSpec((tm, LRU_DIM), lambda i: (i, COL_LG)), pl.BlockSpec((tm, D_MODEL), row),
        pl.BlockSpec((1, D_MODEL), fixed), pl.BlockSpec((D_MODEL, D_MODEL), fixed),
        pl.BlockSpec((1, D_MODEL), fixed), pl.BlockSpec((1, D_MODEL), fixed),
    ]
    x_spec = pl.BlockSpec((tm, D_MODEL), row)
    x_shape = jax.ShapeDtypeStruct((n, D_MODEL), jnp.float32)
    args = (y_conv, o, h_f, h_b, z, xn, gm, wo, g1, b1)
    if w_router is None:
        return pl.pallas_call(
            _post_mixer_kernel, grid=(n // tm,), in_specs=in_specs, out_specs=x_spec, out_shape=x_shape,
            compiler_params=_cparams(("parallel",)), name="post_mixer")(*args)
    return pl.pallas_call(
        _post_mixer_router_kernel, grid=(n // tm,),
        in_specs=in_specs + [pl.BlockSpec((D_MODEL, LANES), fixed)],
        out_specs=[x_spec, pl.BlockSpec((tm, LANES), row)],
        out_shape=[x_shape, jax.ShapeDtypeStruct((n, LANES), jnp.float32)],
        compiler_params=_cparams(("parallel",)), name="post_mixer_router")(*args, w_router)


def _swiglu_chunk(xb, wg, wu, wd):
    gate = jnp.dot(xb, wg, preferred_element_type=jnp.float32)
    up = jnp.dot(xb, wu, preferred_element_type=jnp.float32)
    hidden = (jax.nn.silu(gate) * up).astype(wd.dtype)
    return jnp.dot(hidden, wd, preferred_element_type=jnp.float32)


def _ffn_dense_kernel(x1_ref, wg_ref, wu_ref, wd_ref, g2_ref, b2_ref, out_ref, xb_ref, acc_ref):
    f = pl.program_id(1)

    @pl.when(f == 0)
    def _():
        xb_ref[...] = x1_ref[...].astype(xb_ref.dtype)
        acc_ref[...] = jnp.zeros_like(acc_ref)

    acc_ref[...] += _swiglu_chunk(xb_ref[...], wg_ref[...], wu_ref[...], wd_ref[...])

    @pl.when(f == pl.num_programs(1) - 1)
    def _():
        out_ref[...] = _layer_norm(DN_ALPHA * x1_ref[...] + acc_ref[...], g2_ref[...], b2_ref[...])


def ffn_dense(x1, wg, wu, wd, g2, b2):
    n = x1.shape[0]
    tm = min(TM_FFN, n)
    tf = TF
    row = lambda i, f: (i, 0)
    fixed = lambda i, f: (0, 0)
    return pl.pallas_call(
        _ffn_dense_kernel, grid=(n // tm, D_FF // tf),
        in_specs=[pl.BlockSpec((tm, D_MODEL), row),
                  pl.BlockSpec((D_MODEL, tf), lambda i, f: (0, f)),
                  pl.BlockSpec((D_MODEL, tf), lambda i, f: (0, f)),
                  pl.BlockSpec((tf, D_MODEL), lambda i, f: (f, 0)),
                  pl.BlockSpec((1, D_MODEL), fixed), pl.BlockSpec((1, D_MODEL), fixed)],
        out_specs=pl.BlockSpec((tm, D_MODEL), row),
        out_shape=jax.ShapeDtypeStruct((n, D_MODEL), jnp.float32),
        scratch_shapes=[pltpu.VMEM((tm, D_MODEL), MXU_DTYPE), pltpu.VMEM((tm, D_MODEL), jnp.float32)],
        compiler_params=_cparams(("parallel", "arbitrary")), name="ffn_dense",
    )(x1, wg, wu, wd, g2, b2)


def _moe_dispatch_kernel(st_ref, nb_ref, x_ref, lr_ref, xs_hbm, buf_ref, sem):
    i = pl.program_id(0)
    slot = i % 2
    n_rows = buf_ref.shape[1] // SUBLANES
    lr = lr_ref[0]
    rows = lax.broadcasted_iota(jnp.int32, (n_rows, lr.shape[1]), 0)
    sel = jnp.where(jnp.logical_or(rows == lr[0:1], rows == lr[1:2]), 1.0, 0.0)
    xc = _dot(sel, x_ref[...])
    for s in range(SUBLANES):
        buf_ref[slot, pl.ds(s, n_rows, stride=SUBLANES), :] = xc[:, s * LANES:(s + 1) * LANES]

    blk = DISP_BLK * SUBLANES

    def block_copy(src, dst, sl):
        return pltpu.make_async_copy(buf_ref.at[sl, pl.ds(src, blk), :], xs_hbm.at[pl.ds(dst, blk), :],
                                     sem.at[sl])

    def wait_tile(t, sl):
        total = nb_ref[t * N_EXPERTS]
        for e in range(1, N_EXPERTS):
            total = total + nb_ref[t * N_EXPERTS + e]

        def wait(k, c):
            block_copy(0, 0, sl).wait()
            return c

        lax.fori_loop(0, total, wait, 0)

    @pl.when(i > 0)
    def _():
        wait_tile(i - 1, 1 - slot)

    off = jnp.int32(0)
    for e in range(N_EXPERTS):
        a = st_ref[i * N_EXPERTS + e]
        nb = nb_ref[i * N_EXPERTS + e]

        def start(k, c, a=a, off=off):
            block_copy(pl.multiple_of((off + k * DISP_BLK) * SUBLANES, blk),
                       pl.multiple_of((a + k * DISP_BLK) * SUBLANES, SUBLANES), slot).start()
            return c

        lax.fori_loop(0, nb, start, 0)
        off = off + nb * DISP_BLK

    @pl.when(i == pl.num_programs(0) - 1)
    def _():
        wait_tile(i, slot)


def moe_dispatch(x1, row_start, blk_count, local_row, n_sorted_rows):
    n = x1.shape[0]
    tm = local_row.shape[2]
    buf_rows = -(-(2 * tm + N_EXPERTS * (DISP_BLK - 1)) // 16) * 16
    grid_spec = pltpu.PrefetchScalarGridSpec(
        num_scalar_prefetch=2, grid=(n // tm,),
        in_specs=[pl.BlockSpec((tm, D_MODEL), lambda i, st, nb: (i, 0)),
                  pl.BlockSpec((1, 2, tm), lambda i, st, nb: (i, 0, 0))],
        out_specs=pl.BlockSpec(memory_space=pl.ANY),
        scratch_shapes=[pltpu.VMEM((2, buf_rows * SUBLANES, LANES), jnp.float32),
                        pltpu.SemaphoreType.DMA((2,))],
    )
    return pl.pallas_call(
        _moe_dispatch_kernel, grid_spec=grid_spec,
        out_shape=jax.ShapeDtypeStruct((n_sorted_rows * SUBLANES, LANES), jnp.float32),
        compiler_params=_cparams(("arbitrary",)), name="moe_dispatch",
    )(row_start, blk_count, x1, local_row)


def _moe_ffn_kernel(te_ref, tr_ref, xs_ref, wg_ref, wu_ref, wd_ref, y_ref, xb_ref, acc_ref):
    i = pl.program_id(0)
    f = pl.program_id(1)
    n_valid = tr_ref[i]
    tm = xb_ref.shape[0]

    @pl.when(jnp.logical_and(n_valid > 0, f == 0))
    def _():
        live = lax.broadcasted_iota(jnp.int32, (tm, LANES), 0) < n_valid
        for s in range(SUBLANES):
            piece = xs_ref[pl.ds(s, tm, stride=SUBLANES), :]
            xb_ref[:, s * LANES:(s + 1) * LANES] = jnp.where(live, piece, 0.0).astype(xb_ref.dtype)
        acc_ref[...] = jnp.zeros_like(acc_ref)

    @pl.when(n_valid > 0)
    def _():
        acc_ref[...] += _swiglu_chunk(xb_ref[...], wg_ref[...], wu_ref[...], wd_ref[...])

    @pl.when(f == pl.num_programs(1) - 1)
    def _():
        y_ref[...] = jnp.where(n_valid > 0, acc_ref[...], 0.0).astype(y_ref.dtype)


def moe_ffn(x_sorted, tile_expert, tile_rows, wg, wu, wd):
    n_tiles = tile_expert.shape[0]
    tm = x_sorted.shape[0] // (n_tiles * SUBLANES)
    tf = TF
    grid_spec = pltpu.PrefetchScalarGridSpec(
        num_scalar_prefetch=2, grid=(n_tiles, D_FF // tf),
        in_specs=[
            pl.BlockSpec((tm * SUBLANES, LANES), lambda i, f, te, tr: (i, 0)),
            pl.BlockSpec((None, D_MODEL, tf), lambda i, f, te, tr: (te[i], 0, f)),
            pl.BlockSpec((None, D_MODEL, tf), lambda i, f, te, tr: (te[i], 0, f)),
            pl.BlockSpec((None, tf, D_MODEL), lambda i, f, te, tr: (te[i], f, 0)),
        ],
        out_specs=pl.BlockSpec((tm, D_MODEL), lambda i, f, te, tr: (i, 0)),
        scratch_shapes=[pltpu.VMEM((tm, D_MODEL), MXU_DTYPE), pltpu.VMEM((tm, D_MODEL), jnp.float32)],
    )
    return pl.pallas_call(
        _moe_ffn_kernel, grid_spec=grid_spec,
        out_shape=jax.ShapeDtypeStruct((n_tiles * tm, D_MODEL), MXU_DTYPE),
        compiler_params=_cparams(("arbitrary", "arbitrary")), name="moe_ffn",
    )(tile_expert, tile_rows, x_sorted, wg, wu, wd)


def _moe_combine_kernel(a_ref, nb_ref, x1_ref, route_ref, y_hbm, g2_ref, b2_ref, out_ref, ybuf_ref, sem):
    i = pl.program_id(0)
    slot = i % 2

    def block_copy(src, dst, sl):
        return pltpu.make_async_copy(y_hbm.at[pl.ds(src, CMB_BLK), :], ybuf_ref.at[sl, pl.ds(dst, CMB_BLK), :],
                                     sem.at[sl])

    def fetch_tile(t, sl):
        off = jnp.int32(0)
        for e in range(N_EXPERTS):
            a = a_ref[t * N_EXPERTS + e]
            nb = nb_ref[t * N_EXPERTS + e]

            def start(k, c, a=a, off=off):
                block_copy(pl.multiple_of(a + k * CMB_BLK, CMB_BLK), pl.multiple_of(off + k * CMB_BLK, CMB_BLK),
                           sl).start()
                return c

            lax.fori_loop(0, nb, start, 0)
            off = off + nb * CMB_BLK

    @pl.when(i == 0)
    def _():
        ybuf_ref[...] = jnp.zeros_like(ybuf_ref)
        fetch_tile(i, slot)

    total = nb_ref[i * N_EXPERTS]
    for e in range(1, N_EXPERTS):
        total = total + nb_ref[i * N_EXPERTS + e]

    def wait(k, c):
        block_copy(0, 0, slot).wait()
        return c

    lax.fori_loop(0, total, wait, 0)

    @pl.when(i + 1 < pl.num_programs(0))
    def _():
        fetch_tile(i + 1, 1 - slot)

    route = route_ref[...]
    col = lax.broadcasted_iota(jnp.int32, (route.shape[0], ybuf_ref.shape[1]), 1)
    sel = (jnp.where(col == route[:, 4:5].astype(jnp.int32), route[:, 2:3], 0.0)
           + jnp.where(col == route[:, 5:6].astype(jnp.int32), route[:, 3:4], 0.0))
    f = jnp.dot(sel.astype(ybuf_ref.dtype), ybuf_ref[slot], preferred_element_type=jnp.float32)
    out_ref[...] = _layer_norm(DN_ALPHA * x1_ref[...] + f, g2_ref[...], b2_ref[...])


def moe_combine(blk_start, blk_count, x1, route, y_sorted, g2, b2):
    n = x1.shape[0]
    tm = min(TM_CMB, n)
    buf_rows = -(-(2 * tm + N_EXPERTS * 2 * (CMB_BLK - 1)) // 256) * 256
    row = lambda i, a, nb: (i, 0)
    fixed = lambda i, a, nb: (0, 0)
    grid_spec = pltpu.PrefetchScalarGridSpec(
        num_scalar_prefetch=2, grid=(n // tm,),
        in_specs=[pl.BlockSpec((tm, D_MODEL), row), pl.BlockSpec((tm, LANES), row),
                  pl.BlockSpec(memory_space=pl.ANY),
                  pl.BlockSpec((1, D_MODEL), fixed), pl.BlockSpec((1, D_MODEL), fixed)],
        out_specs=pl.BlockSpec((tm, D_MODEL), row),
        scratch_shapes=[pltpu.VMEM((2, buf_rows, D_MODEL), y_sorted.dtype), pltpu.SemaphoreType.DMA((2,))],
    )
    return pl.pallas_call(
        _moe_combine_kernel, grid_spec=grid_spec,
        out_shape=jax.ShapeDtypeStruct((n, D_MODEL), jnp.float32),
        compiler_params=_cparams(("arbitrary",)), name="moe_combine",
    )(blk_start, blk_count, x1, route, y_sorted, g2, b2)


def _dispatch_plan(expert_idx, tm, tm_tok):
    n = expert_idx.shape[0]
    e_flat = expert_idx.reshape(-1)
    onehot = (e_flat[:, None] == jnp.arange(N_EXPERTS, dtype=jnp.int32)[None, :]).astype(jnp.int32)
    csum = jnp.cumsum(onehot, axis=0)
    before = csum - onehot
    rank = jnp.sum(before * onehot, axis=1)
    counts = csum[-1]
    tiles_per = (counts + (DISP_BLK - 1) + tm - 1) // tm
    tile_end = jnp.cumsum(tiles_per)
    tile_begin = tile_end - tiles_per
    group_start = tile_begin * tm
    pos_flat = jnp.sum(onehot * group_start[None, :], axis=1) + rank
    n_tiles = (2 * n) // tm + N_EXPERTS + 1
    t = jnp.arange(n_tiles, dtype=jnp.int32)
    last_valid = jnp.maximum(tile_end[-1] - 1, 0)
    t_eff = jnp.minimum(t, last_valid)
    tile_expert = jnp.minimum(jnp.sum((t_eff[:, None] >= tile_end[None, :]).astype(jnp.int32), axis=1),
                              N_EXPERTS - 1).astype(jnp.int32)
    of_tile = (tile_expert[:, None] == jnp.arange(N_EXPERTS, dtype=jnp.int32)[None, :]).astype(jnp.int32)
    rows_left = jnp.sum(of_tile * (counts - (t[:, None] - tile_begin[None, :]) * tm), axis=1)
    tile_rows = jnp.where(t < tile_end[-1], jnp.clip(rows_left, 0, tm), 0).astype(jnp.int32)

    first = before[::2 * tm_tok]
    cnt = jnp.concatenate([first[1:], counts[None, :]], axis=0) - first
    start = group_start[None, :] + first
    pair_onehot = onehot.reshape(-1, 2 * tm_tok, N_EXPERTS)

    d_count = (cnt + DISP_BLK - 1) // DISP_BLK
    d_off = (jnp.cumsum(d_count, axis=1) - d_count) * DISP_BLK
    d_local = rank + jnp.sum(pair_onehot * (d_off - first)[:, None, :], axis=2).reshape(-1)
    d_local = d_local.reshape(-1, tm_tok, 2).transpose(0, 2, 1)

    blk_start = (start // CMB_BLK) * CMB_BLK
    blk_count = jnp.where(cnt > 0, (start - blk_start + cnt + CMB_BLK - 1) // CMB_BLK, 0)
    buf_off = (jnp.cumsum(blk_count, axis=1) - blk_count) * CMB_BLK
    local = pos_flat + jnp.sum(pair_onehot * (buf_off - blk_start)[:, None, :], axis=2).reshape(-1)
    flat = lambda v: v.reshape(-1).astype(jnp.int32)
    return dict(n_tiles=n_tiles, tile_expert=tile_expert, tile_rows=tile_rows,
                row_start=flat(start), d_count=flat(d_count), d_local=d_local.astype(jnp.int32),
                blk_start=flat(blk_start), blk_count=flat(blk_count), c_local=local.reshape(n, 2))


def _pad_w_in(w):
    z64 = jnp.zeros((D_MODEL, KR_X1), w.dtype)
    z32 = jnp.zeros((D_MODEL, LANES - KR_X1 - MLA_ROPE), w.dtype)
    return jnp.concatenate([w[:, 0:768], w[:, 1440:1696], w[:, 1696:1952], w[:, 1152:1408],
                            w[:, 768:1152], z64, w[:, 1408:1440], z32], axis=1).astype(MXU_DTYPE)


def _rope_tables(s):
    pos = jnp.arange(s, dtype=jnp.float32)
    inv = ROPE_THETA ** (-jnp.arange(0, MLA_ROPE, 2, dtype=jnp.float32) / MLA_ROPE)
    ang = pos[:, None] * inv[None, :]
    cos, sin = jnp.cos(ang), jnp.sin(ang)
    zl = jnp.zeros((s, KR_X1), jnp.float32)
    zh = jnp.zeros((s, HALF_ROPE), jnp.float32)
    zr = jnp.zeros((s, LANES - KR_X1 - MLA_ROPE), jnp.float32)
    kc = jnp.concatenate([zl, cos, cos, zr], axis=1)
    ksa = jnp.concatenate([zl, -sin, zh, zr], axis=1)
    ksb = jnp.concatenate([zl, zh, sin, zr], axis=1)
    return cos.T, sin.T, kc, ksa, ksb


def _block_diag(w):
    nb, bw, _ = w.shape
    out = jnp.zeros((nb * bw, nb * bw), w.dtype)
    for i in range(nb):
        out = out.at[i * bw:(i + 1) * bw, i * bw:(i + 1) * bw].set(w[i])
    return out


def kernel(x, ln_in_g, ln_in_b, w_in, conv_w, q_norm_g, w_uq, kv_norm_g, w_ukv, lru_conv_w, lru_conv_b,
           lru_wa, lru_ba, lru_wi, lru_bi, lru_lam, mix_norm_g, w_out, ln1_g, ln1_b, dense_w_gate,
           dense_w_up, dense_w_down, moe_w_router, moe_w_gate, moe_w_up, moe_w_down, ln2_g, ln2_b):
    b, s, d = x.shape
    n = b * s
    rope = _rope_tables(s)
    vec = lambda v: v.reshape(1, -1)
    cur = x.reshape(n, d)
    for l in range(DEPTH):
        w_in_pad = _pad_w_in(w_in[l])
        wq_t = jnp.pad(w_uq[l].T.reshape(MLA_HEADS, MLA_NOPE + MLA_ROPE, MLA_Q_RANK),
                       ((0, 0), (0, HEAD_PAD - MLA_NOPE - MLA_ROPE), (0, 0))
                       ).reshape(MLA_HEADS * HEAD_PAD, MLA_Q_RANK).astype(MXU_DTYPE)
        wkv = w_ukv[l].reshape(MLA_KV_RANK, MLA_HEADS, MLA_NOPE + MLA_V)
        wk_pad = jnp.pad(wkv[:, :, :MLA_NOPE], ((0, 0), (0, 0), (0, HEAD_PAD - MLA_NOPE))
                         ).reshape(MLA_KV_RANK, MLA_HEADS * HEAD_PAD).astype(MXU_DTYPE)
        wv_t = wkv[:, :, MLA_NOPE:].reshape(MLA_KV_RANK, MLA_HEADS * MLA_V).T.astype(MXU_DTYPE)
        gate_w = [jnp.concatenate([_block_diag(lru_wa[l, dr]), _block_diag(lru_wi[l, dr])], axis=1
                                  ).astype(MXU_DTYPE) for dr in range(2)]
        gate_b = [jnp.concatenate([lru_ba[l, dr], lru_bi[l, dr]]).reshape(1, -1) for dr in range(2)]

        xn3, z3, q_t, k, v_t = in_proj(
            cur.reshape(b, s, d), w_in_pad, vec(q_norm_g[l]), vec(kv_norm_g[l]), wq_t, wk_pad, wv_t, rope,
            ln=(vec(ln_in_g), vec(ln_in_b)) if l == 0 else None)
        xn, z = xn3.reshape(n, d), z3.reshape(n, D_Z)
        o = attention(q_t, k, v_t).reshape(n, MLA_HEADS * MLA_V)
        h_f, y_conv = lru_scan(z3, lru_conv_w[l], vec(lru_conv_b[l]), gate_w[0], gate_b[0],
                               vec(lru_lam[l, 0]), short_w=conv_w[l])
        h_b = lru_scan(z3, lru_conv_w[l], vec(lru_conv_b[l]), gate_w[1], gate_b[1],
                       vec(lru_lam[l, 1]), reverse=True)
        mixer_args = (y_conv.reshape(n, CONV_DIM), o, h_f.reshape(n, LRU_DIM), h_b.reshape(n, LRU_DIM),
                      z, xn, vec(mix_norm_g[l]), w_out[l].astype(MXU_DTYPE), vec(ln1_g[l]), vec(ln1_b[l]))

        j = l // 2
        if l % 2 == 0:
            x1 = post_mixer(*mixer_args)
            cur = ffn_dense(x1, dense_w_gate[j].astype(MXU_DTYPE), dense_w_up[j].astype(MXU_DTYPE),
                            dense_w_down[j].astype(MXU_DTYPE), vec(ln2_g[l]), vec(ln2_b[l]))
        else:
            wr_hi = moe_w_router[j].astype(jnp.bfloat16).astype(jnp.float32)
            wr_lo = (moe_w_router[j] - wr_hi).astype(jnp.bfloat16).astype(jnp.float32)
            w_router = jnp.pad(jnp.concatenate([wr_hi, wr_lo], axis=1), ((0, 0), (0, LANES - 2 * N_EXPERTS)))
            x1, route = post_mixer(*mixer_args, w_router=w_router)
            expert_idx = route[:, 0:2].astype(jnp.int32)
            tm = min(TM_FFN, n)
            plan = _dispatch_plan(expert_idx, tm, min(TM_CMB, n))
            x_sorted = moe_dispatch(x1, plan["row_start"], plan["d_count"], plan["d_local"],
                                    plan["n_tiles"] * tm)
            y_sorted = moe_ffn(x_sorted, plan["tile_expert"], plan["tile_rows"],
                               moe_w_gate[j].astype(MXU_DTYPE), moe_w_up[j].astype(MXU_DTYPE),
                               moe_w_down[j].astype(MXU_DTYPE))
            route = jnp.concatenate([route[:, :4], plan["c_local"].astype(jnp.float32), route[:, 6:]], axis=1)
            cur = moe_combine(plan["blk_start"], plan["blk_count"], x1, route, y_sorted,
                              vec(ln2_g[l]), vec(ln2_b[l]))
    return cur.reshape(b, s, d)
```

```python
import functools

import jax
import jax.numpy as jnp
from jax import lax
from jax.experimental import pallas as pl
from jax.experimental.pallas import tpu as pltpu

D_MODEL = 1024
DEPTH = 2
CONV_DIM = 256
MLA_HEADS = 8
MLA_NOPE = 64
MLA_ROPE = 32
MLA_V = 64
MLA_Q_RANK = 384
MLA_KV_RANK = 256
LRU_DIM = 256
LRU_C = 8.0
ROPE_THETA = 10000.0
D_FF = 3584
N_EXPERTS = 8
DN_ALPHA = (2.0 * DEPTH) ** 0.25
LN_EPS = 1e-5
RMS_EPS = 1e-6

D_IN_PAD = 2048
HALF_ROPE = MLA_ROPE // 2
HEAD_PAD = 128
COL_CB, COL_CC, COL_CH, COL_LG, COL_LX = 0, 1, 2, 3, 4
D_Z = 1280
KR_X1 = 64
V_ROWS = MLA_V + 16

LANES = 128
SUBLANES = 8
VMEM_LIMIT = 56 * 1024 * 1024
MXU_DTYPE = jnp.bfloat16

TM = 1024
TS = 512
TQ = 512
ATTN_GROUP = 2
ATTN_HEADS = 4
TM_FFN = 1024
TF = 512
TM_CMB = 256
CMB_BLK = 16
DISP_BLK = 8
NEG_INF = float("-inf")
LOG2E = 1.4426950408889634


def _cparams(sem, vmem=VMEM_LIMIT, flags=None):
    return pltpu.CompilerParams(dimension_semantics=sem, vmem_limit_bytes=vmem, flags=flags)


def _layer_norm(x, g, b):
    mu = jnp.mean(x, axis=-1, keepdims=True)
    xc = x - mu
    var = jnp.mean(xc * xc, axis=-1, keepdims=True)
    return xc * lax.rsqrt(var + LN_EPS) * g + b


def _rms_norm(x, g):
    ms = jnp.mean(x * x, axis=-1, keepdims=True)
    return x * lax.rsqrt(ms + RMS_EPS) * g


def _dot(a, b):
    return jnp.dot(a.astype(MXU_DTYPE), b.astype(MXU_DTYPE), preferred_element_type=jnp.float32)


def _dot_nt(a, b):
    return lax.dot_general(a.astype(MXU_DTYPE), b.astype(MXU_DTYPE), (((1,), (1,)), ((), ())),
                           preferred_element_type=jnp.float32)


def _mla_outputs(ckv, cq, kr, gq_ref, gkv_ref, wq_ref, wk_ref, wv_ref, cos_ref, sin_ref, kc_ref, ksa_ref,
                 ksb_ref, q_out, k_out, v_out):
    cqn = _rms_norm(cq, gq_ref[...])
    ckvn = _rms_norm(ckv, gkv_ref[...])

    krope = (kr * kc_ref[...]
             + pltpu.roll(kr, LANES - HALF_ROPE, axis=1) * ksa_ref[...]
             + pltpu.roll(kr, HALF_ROPE, axis=1) * ksb_ref[...])
    k_all = _dot(ckvn, wk_ref[...])
    for h in range(MLA_HEADS):
        k_out[0, h] = (k_all[:, h * HEAD_PAD:(h + 1) * HEAD_PAD] + krope).astype(k_out.dtype)

    v_all = _dot_nt(wv_ref[...], ckvn)
    ones = jnp.ones((V_ROWS - MLA_V, v_all.shape[1]), jnp.float32)
    for h in range(MLA_HEADS):
        v_out[0, h, 0] = jnp.concatenate([v_all[h * MLA_V:(h + 1) * MLA_V], ones], axis=0).astype(v_out.dtype)

    q_all = _dot_nt(wq_ref[...], cqn)
    cos_t = cos_ref[...]
    sin_t = sin_ref[...]
    qscale = (MLA_NOPE + MLA_ROPE) ** -0.5 * LOG2E
    for h in range(MLA_HEADS):
        base = h * HEAD_PAD
        nope = q_all[base:base + MLA_NOPE]
        x1 = q_all[base + MLA_NOPE:base + MLA_NOPE + HALF_ROPE]
        x2 = q_all[base + MLA_NOPE + HALF_ROPE:base + MLA_NOPE + MLA_ROPE]
        zero = q_all[base + MLA_NOPE + MLA_ROPE:base + HEAD_PAD]
        qh = jnp.concatenate([nope, x1 * cos_t - x2 * sin_t, x2 * cos_t + x1 * sin_t, zero], axis=0)
        q_out[0, h] = (qh * qscale).astype(q_out.dtype)


def _project(xn, w_ref, z_ref, mla_refs):
    z = _dot(xn, w_ref[...])
    z_ref[0] = z[:, :D_Z]
    _mla_outputs(z[:, D_Z:D_Z + MLA_KV_RANK], z[:, D_Z + MLA_KV_RANK:D_Z + MLA_KV_RANK + MLA_Q_RANK],
                 z[:, D_IN_PAD - LANES:], *mla_refs)


def _in_proj_ln_kernel(x_ref, g_ref, b_ref, w_ref, *refs):
    mla_in, (xn_ref, z_ref), mla_out = refs[:10], refs[10:12], refs[12:]
    xn = _layer_norm(x_ref[0], g_ref[...], b_ref[...])
    xn_ref[0] = xn
    _project(xn, w_ref, z_ref, mla_in + mla_out)


def _in_proj_kernel(x_ref, w_ref, *refs):
    mla_in, z_ref, mla_out = refs[:10], refs[10], refs[11:]
    _project(x_ref[0], w_ref, z_ref, mla_in + mla_out)


def in_proj(x3, w_pad, gq, gkv, wq_t, wk_pad, wv_t, rope, ln=None):
    b, s, _ = x3.shape
    ts = min(TS, s)
    nt = s // ts
    cos_t, sin_t, kc, ksa, ksb = rope
    fixed = lambda bi, i: (0, 0)
    tile = lambda bi, i: (bi, i, 0)
    x_spec = pl.BlockSpec((1, ts, D_MODEL), tile)
    vec = pl.BlockSpec((1, D_MODEL), fixed)
    mla_specs = [
        pl.BlockSpec((1, MLA_Q_RANK), fixed),
        pl.BlockSpec((1, MLA_KV_RANK), fixed),
        pl.BlockSpec((MLA_HEADS * HEAD_PAD, MLA_Q_RANK), fixed),
        pl.BlockSpec((MLA_KV_RANK, MLA_HEADS * HEAD_PAD), fixed),
        pl.BlockSpec((MLA_HEADS * MLA_V, MLA_KV_RANK), fixed),
        pl.BlockSpec((HALF_ROPE, ts), lambda bi, i: (0, i)),
        pl.BlockSpec((HALF_ROPE, ts), lambda bi, i: (0, i)),
        pl.BlockSpec((ts, LANES), lambda bi, i: (i, 0)),
        pl.BlockSpec((ts, LANES), lambda bi, i: (i, 0)),
        pl.BlockSpec((ts, LANES), lambda bi, i: (i, 0)),
    ]
    mla_args = (gq, gkv, wq_t, wk_pad, wv_t, cos_t, sin_t, kc, ksa, ksb)
    w_spec = pl.BlockSpec((D_MODEL, D_IN_PAD), fixed)
    z_spec = pl.BlockSpec((1, ts, D_Z), tile)
    z_shape = jax.ShapeDtypeStruct((b, s, D_Z), jnp.float32)
    qkv_specs = [
        pl.BlockSpec((1, MLA_HEADS, HEAD_PAD, ts), lambda bi, i: (bi, 0, 0, i)),
        pl.BlockSpec((1, MLA_HEADS, ts, HEAD_PAD), lambda bi, i: (bi, 0, i, 0)),
        pl.BlockSpec((1, MLA_HEADS, 1, V_ROWS, ts), lambda bi, i: (bi, 0, i, 0, 0)),
    ]
    qkv_shape = [
        jax.ShapeDtypeStruct((b, MLA_HEADS, HEAD_PAD, s), MXU_DTYPE),
        jax.ShapeDtypeStruct((b, MLA_HEADS, s, HEAD_PAD), MXU_DTYPE),
        jax.ShapeDtypeStruct((b, MLA_HEADS, nt, V_ROWS, ts), MXU_DTYPE),
    ]
    if ln is None:
        z, q_t, k, v_t = pl.pallas_call(
            _in_proj_kernel, grid=(b, nt), in_specs=[x_spec, w_spec] + mla_specs,
            out_specs=[z_spec] + qkv_specs, out_shape=[z_shape] + qkv_shape,
            compiler_params=_cparams(("parallel", "parallel")), name="in_proj")(x3, w_pad, *mla_args)
        return x3, z, q_t, k, v_t
    g, bb = ln
    xn, z, q_t, k, v_t = pl.pallas_call(
        _in_proj_ln_kernel, grid=(b, nt), in_specs=[x_spec, vec, vec, w_spec] + mla_specs,
        out_specs=[x_spec, z_spec] + qkv_specs,
        out_shape=[jax.ShapeDtypeStruct((b, s, D_MODEL), jnp.float32), z_shape] + qkv_shape,
        compiler_params=_cparams(("parallel", "parallel")), name="in_proj_ln")(x3, g, bb, w_pad, *mla_args)
    return xn, z, q_t, k, v_t


def _attn_kernel(k_ref, q_ref, v_ref, o_ref, *scratch, n_chunks, tkc, group_size):
    n_heads = q_ref.shape[1]
    tq = q_ref.shape[3]
    s_bufs = [scratch[4 * hd:4 * hd + 2] for hd in range(n_heads)]
    p_bufs = [scratch[4 * hd + 2:4 * hd + 4] for hd in range(n_heads)]

    def scores(hd, c, s_ref):
        start = pl.multiple_of(c * tkc, tkc)
        s = jnp.dot(k_ref[0, hd, pl.ds(start, tkc), :], q_ref[0, hd], preferred_element_type=jnp.float32)
        s_ref[...] = s
        return jnp.max(s, axis=0, keepdims=True)

    def accumulate(hd, c, s_ref, p_ref, mx, m, acc):
        m_new = jnp.maximum(m, mx)
        alpha = jnp.exp2(m - m_new)
        p_ref[...] = jnp.exp2(s_ref[...] - m_new).astype(p_ref.dtype)
        pv = jnp.dot(v_ref[0, hd, c], p_ref[...], preferred_element_type=jnp.float32)
        return m_new, alpha * acc + pv

    def group(c0, state, prefetch_last):
        state = list(state)
        for g in range(group_size):
            for hd in range(n_heads):
                mx, m, acc = state[hd]
                mx_next = None
                if g + 1 < group_size or prefetch_last:
                    mx_next = scores(hd, c0 + g + 1, s_bufs[hd][(g + 1) % 2])
                m, acc = accumulate(hd, c0 + g, s_bufs[hd][g % 2], p_bufs[hd][g % 2], mx, m, acc)
                state[hd] = (mx_next, m, acc)
        return tuple(state)

    def body(j, state):
        return group(j * group_size, state, prefetch_last=True)

    state = tuple((scores(hd, 0, s_bufs[hd][0]), jnp.full((1, tq), NEG_INF, jnp.float32),
                   jnp.zeros((v_ref.shape[3], tq), jnp.float32)) for hd in range(n_heads))
    n_groups = n_chunks // group_size
    state = lax.fori_loop(0, n_groups - 1, body, state)
    state = group((n_groups - 1) * group_size, state, prefetch_last=False)
    o_t = jnp.concatenate([acc[:MLA_V] / acc[MLA_V:MLA_V + 1] for _, _, acc in state], axis=0)
    o_ref[0] = o_t.T


def attention(q_t, k, v_t):
    b, h, _, s = q_t.shape
    nc, tkc = v_t.shape[2], v_t.shape[4]
    tq = min(TQ, s)
    nh = ATTN_HEADS
    group_size = min(ATTN_GROUP, nc)
    assert group_size % 2 == 0 and nc % group_size == 0, "chunk groups alternate two buffers"
    kern = functools.partial(_attn_kernel, n_chunks=nc, tkc=tkc, group_size=group_size)
    per_head = [pltpu.VMEM((tkc, tq), jnp.float32), pltpu.VMEM((tkc, tq), jnp.float32),
                pltpu.VMEM((tkc, tq), MXU_DTYPE), pltpu.VMEM((tkc, tq), MXU_DTYPE)]
    return pl.pallas_call(
        kern, grid=(b, h // nh, s // tq),
        in_specs=[
            pl.BlockSpec((1, nh, s, HEAD_PAD), lambda bi, hi, qi: (bi, hi, 0, 0)),
            pl.BlockSpec((1, nh, HEAD_PAD, tq), lambda bi, hi, qi: (bi, hi, 0, qi)),
            pl.BlockSpec((1, nh, nc, V_ROWS, tkc), lambda bi, hi, qi: (bi, hi, 0, 0, 0)),
        ],
        out_specs=pl.BlockSpec((1, tq, nh * MLA_V), lambda bi, hi, qi: (bi, qi, hi)),
        out_shape=jax.ShapeDtypeStruct((b, s, h * MLA_V), jnp.float32),
        scratch_shapes=per_head * nh,
        compiler_params=_cparams(("parallel", "parallel", "parallel")), name="attention",
    )(k, q_t, v_t)


def _shift_rows(x, d, edge_rows, row):
    ts = x.shape[0]
    y = pltpu.roll(x, (-d) % ts, axis=0)
    if d < 0:
        return jnp.where(row == 0, edge_rows[0], y)
    for j in range(d):
        y = jnp.where(row == ts - d + j, edge_rows[j], y)
    return y


def _scan_rows(a, u, reverse):
    ts = a.shape[0]
    row = lax.broadcasted_iota(jnp.int32, a.shape, 0)
    d = 1
    while d < ts:
        if reverse:
            valid = row < ts - d
            shift = ts - d
        else:
            valid = row >= d
            shift = d
        a_sh = jnp.where(valid, pltpu.roll(a, shift, axis=0), 1.0)
        u_sh = jnp.where(valid, pltpu.roll(u, shift, axis=0), 0.0)
        u = u + a * u_sh
        a = a * a_sh
        d *= 2
    return a, u


def _lru_core(x, xp, xn, first, last, cw_ref, cb_ref, wg_ref, bg_ref, lam_ref, carry_ref, h_ref, reverse):
    ts = x.shape[0]
    row = lax.broadcasted_iota(jnp.int32, x.shape, 0)
    keep_prev = jnp.where(first, 0.0, 1.0)
    keep_next = jnp.where(last, 0.0, 1.0)
    prev_row = xp[SUBLANES - 1:SUBLANES] * keep_prev
    next0 = xn[0:1] * keep_next
    next1 = xn[1:2] * keep_next
    cw = cw_ref[...]
    xc = (cw[0:1] * _shift_rows(x, -1, [prev_row], row) + cw[1:2] * x
          + cw[2:3] * _shift_rows(x, 1, [next0], row)
          + cw[3:4] * _shift_rows(x, 2, [next0, next1], row) + cb_ref[...])
    gates = _dot(xc, wg_ref[...]) + bg_ref[...]
    rec = jax.nn.sigmoid(gates[:, :LRU_DIM])
    inp = jax.nn.sigmoid(gates[:, LRU_DIM:])
    neg_lam = -lam_ref[...]
    softplus = jnp.maximum(neg_lam, 0.0) + jnp.log(1.0 + jnp.exp(-jnp.abs(neg_lam)))
    log_a = -LRU_C * rec * softplus
    a = jnp.exp(log_a)
    u = jnp.sqrt(1.0 - a * a) * (inp * xc)
    a_cum, h0 = _scan_rows(a, u, reverse)

    @pl.when(pl.program_id(1) == 0)
    def _():
        carry_ref[...] = jnp.zeros_like(carry_ref)

    h = h0 + a_cum * carry_ref[0:1]
    h_ref[0] = h
    edge = h[0:1] if reverse else h[ts - 1:ts]
    carry_ref[...] = jnp.broadcast_to(edge, carry_ref.shape)


def _lru_fwd_kernel(x_ref, xp_ref, xn_ref, cc_ref, ccp_ref, ccn_ref, ch_ref, chp_ref, chn_ref, cbg_ref,
                    cw_ref, cb_ref, wg_ref, bg_ref, lam_ref, sw_ref, h_ref, y_ref, carry_ref):
    i = pl.program_id(1)
    first = i == 0
    last = i == pl.num_programs(1) - 1
    _lru_core(x_ref[0], xp_ref[0], xn_ref[0], first, last, cw_ref, cb_ref, wg_ref, bg_ref, lam_ref,
              carry_ref, h_ref, reverse=False)
    g = cc_ref[0] * ch_ref[0]
    row = lax.broadcasted_iota(jnp.int32, g.shape, 0)
    keep_prev = jnp.where(first, 0.0, 1.0)
    keep_next = jnp.where(last, 0.0, 1.0)
    g_prev = ccp_ref[0, SUBLANES - 1:SUBLANES] * chp_ref[0, SUBLANES - 1:SUBLANES] * keep_prev
    g_next = ccn_ref[0, 0:1] * chn_ref[0, 0:1] * keep_next
    sw = sw_ref[...]
    conv = (sw[0:1] * _shift_rows(g, -1, [g_prev], row) + sw[1:2] * g
            + sw[2:3] * _shift_rows(g, 1, [g_next], row))
    y_ref[0] = cbg_ref[0] * conv


def _lru_bwd_kernel(x_ref, xp_ref, xn_ref, cw_ref, cb_ref, wg_ref, bg_ref, lam_ref, h_ref, carry_ref):
    i = pl.program_id(1)
    nt = pl.num_programs(1)
    first = i == nt - 1
    last = i == 0
    _lru_core(x_ref[0], xp_ref[0], xn_ref[0], first, last, cw_ref, cb_ref, wg_ref, bg_ref, lam_ref,
              carry_ref, h_ref, reverse=True)


def lru_scan(z3, conv_w, conv_b, wg, bg, lam, short_w=None, reverse=False):
    b, s, _ = z3.shape
    ts = min(TS, s)
    nt = s // ts
    rb = ts // SUBLANES
    nrb = s // SUBLANES
    tile = (lambda i: nt - 1 - i) if reverse else (lambda i: i)

    def main(col):
        return pl.BlockSpec((1, ts, LRU_DIM), lambda bi, i: (bi, tile(i), col))

    def prev(col):
        return pl.BlockSpec((1, SUBLANES, LRU_DIM),
                            lambda bi, i: (bi, jnp.maximum(tile(i) * rb - 1, 0), col))

    def nxt(col):
        return pl.BlockSpec((1, SUBLANES, LRU_DIM),
                            lambda bi, i: (bi, jnp.minimum((tile(i) + 1) * rb, nrb - 1), col))

    fixed = lambda bi, i: (0, 0)
    par = [pl.BlockSpec((4, LRU_DIM), fixed), pl.BlockSpec((1, LRU_DIM), fixed),
           pl.BlockSpec((LRU_DIM, 2 * LRU_DIM), fixed), pl.BlockSpec((1, 2 * LRU_DIM), fixed),
           pl.BlockSpec((1, LRU_DIM), fixed)]
    h_shape = jax.ShapeDtypeStruct((b, s, LRU_DIM), jnp.float32)
    scratch = [pltpu.VMEM((SUBLANES, LRU_DIM), jnp.float32)]
    if reverse:
        return pl.pallas_call(
            _lru_bwd_kernel, grid=(b, nt),
            in_specs=[main(COL_LX), prev(COL_LX), nxt(COL_LX)] + par,
            out_specs=main(0), out_shape=h_shape, scratch_shapes=scratch,
            compiler_params=_cparams(("parallel", "arbitrary")), name="lru_bwd",
        )(z3, z3, z3, conv_w, conv_b, wg, bg, lam)
    return pl.pallas_call(
        _lru_fwd_kernel, grid=(b, nt),
        in_specs=[main(COL_LX), prev(COL_LX), nxt(COL_LX), main(COL_CC), prev(COL_CC), nxt(COL_CC),
                  main(COL_CH), prev(COL_CH), nxt(COL_CH), main(COL_CB)] + par
                 + [pl.BlockSpec((3, CONV_DIM), fixed)],
        out_specs=[main(0), main(0)], out_shape=[h_shape, h_shape], scratch_shapes=scratch,
        compiler_params=_cparams(("parallel", "arbitrary")), name="lru_fwd",
    )(z3, z3, z3, z3, z3, z3, z3, z3, z3, z3, conv_w, conv_b, wg, bg, lam, short_w)


def _post_mixer_body(yc_ref, o_ref, hf_ref, hb_ref, lg_ref, xn_ref, gm_ref, wo_ref, g1_ref, b1_ref):
    gm = gm_ref[...]
    y_lru = jax.nn.gelu(lg_ref[...], approximate=True) * (hf_ref[...] + hb_ref[...])
    y = jnp.concatenate([
        _rms_norm(yc_ref[...], gm[:, :CONV_DIM]),
        _rms_norm(o_ref[...], gm[:, CONV_DIM:CONV_DIM + MLA_HEADS * MLA_V]),
        _rms_norm(y_lru, gm[:, CONV_DIM + MLA_HEADS * MLA_V:]),
    ], axis=1)
    mix = _dot(y, wo_ref[...])
    return _layer_norm(DN_ALPHA * xn_ref[...] + mix, g1_ref[...], b1_ref[...])


def _post_mixer_kernel(yc_ref, o_ref, hf_ref, hb_ref, lg_ref, xn_ref, gm_ref, wo_ref, g1_ref, b1_ref,
                       x1_ref):
    x1_ref[...] = _post_mixer_body(yc_ref, o_ref, hf_ref, hb_ref, lg_ref, xn_ref, gm_ref, wo_ref, g1_ref,
                                   b1_ref)


def _post_mixer_router_kernel(yc_ref, o_ref, hf_ref, hb_ref, lg_ref, xn_ref, gm_ref, wo_ref, g1_ref,
                              b1_ref, wr_ref, x1_ref, route_ref):
    x1 = _post_mixer_body(yc_ref, o_ref, hf_ref, hb_ref, lg_ref, xn_ref, gm_ref, wo_ref, g1_ref, b1_ref)
    x1_ref[...] = x1
    wr = wr_ref[...].astype(jnp.bfloat16)
    x_hi = x1.astype(jnp.bfloat16)
    x_lo = (x1 - x_hi.astype(jnp.float32)).astype(jnp.bfloat16)
    t_hi = jnp.dot(x_hi, wr, preferred_element_type=jnp.float32)
    t_lo = jnp.dot(x_lo, wr, preferred_element_type=jnp.float32)
    logits = t_hi + t_lo + pltpu.roll(t_hi, LANES - N_EXPERTS, axis=1)
    lane = lax.broadcasted_iota(jnp.int32, logits.shape, 1)
    logits = jnp.where(lane < N_EXPERTS, logits, NEG_INF)
    v1 = jnp.max(logits, axis=1, keepdims=True)
    i1 = jnp.min(jnp.where(logits == v1, lane, LANES), axis=1, keepdims=True)
    rest = jnp.where(lane == i1, NEG_INF, logits)
    v2 = jnp.max(rest, axis=1, keepdims=True)
    i2 = jnp.min(jnp.where(rest == v2, lane, LANES), axis=1, keepdims=True)
    e = jnp.exp(v2 - v1)
    g_top = 1.0 / (1.0 + e)
    g_sec = e * g_top
    route_ref[...] = jnp.where(lane == 0, i1.astype(jnp.float32),
                               jnp.where(lane == 1, i2.astype(jnp.float32),
                                         jnp.where(lane == 2, g_top, jnp.where(lane == 3, g_sec, 0.0))))


def post_mixer(y_conv, o, h_f, h_b, z, xn, gm, wo, g1, b1, w_router=None):
    n = xn.shape[0]
    tm = min(TM, n)
    row = lambda i: (i, 0)
    fixed = lambda i: (0, 0)
    in_specs = [
        pl.BlockSpec((tm, CONV_DIM), row), pl.BlockSpec((tm, MLA_HEADS * MLA_V), row),
        pl.BlockSpec((tm, LRU_DIM), row), pl.BlockSpec((tm, LRU_DIM), row),
        pl.BlockSpec((tm, LRU_DIM), lambda i: (i, COL_LG)), pl.BlockSpec((tm, D_MODEL), row),
        pl.BlockSpec((1, D_MODEL), fixed), pl.BlockSpec((D_MODEL, D_MODEL), fixed),
        pl.BlockSpec((1, D_MODEL), fixed), pl.BlockSpec((1, D_MODEL), fixed),
    ]
    x_spec = pl.BlockSpec((tm, D_MODEL), row)
    x_shape = jax.ShapeDtypeStruct((n, D_MODEL), jnp.float32)
    args = (y_conv, o, h_f, h_b, z, xn, gm, wo, g1, b1)
    if w_router is None:
        return pl.pallas_call(
            _post_mixer_kernel, grid=(n // tm,), in_specs=in_specs, out_specs=x_spec, out_shape=x_shape,
            compiler_params=_cparams(("parallel",)), name="post_mixer")(*args)
    return pl.pallas_call(
        _post_mixer_router_kernel, grid=(n // tm,),
        in_specs=in_specs + [pl.BlockSpec((D_MODEL, LANES), fixed)],
        out_specs=[x_spec, pl.BlockSpec((tm, LANES), row)],
        out_shape=[x_shape, jax.ShapeDtypeStruct((n, LANES), jnp.float32)],
        compiler_params=_cparams(("parallel",)), name="post_mixer_router")(*args, w_router)


def _swiglu_chunk(xb, wg, wu, wd):
    gate = jnp.dot(xb, wg.astype(xb.dtype), preferred_element_type=jnp.float32)
    up = jnp.dot(xb, wu.astype(xb.dtype), preferred_element_type=jnp.float32)
    hidden = (jax.nn.silu(gate) * up).astype(xb.dtype)
    return jnp.dot(hidden, wd.astype(xb.dtype), preferred_element_type=jnp.float32)


def _ffn_dense_kernel(x1_ref, wg_ref, wu_ref, wd_ref, g2_ref, b2_ref, out_ref, xb_ref, acc_ref):
    f = pl.program_id(1)

    @pl.when(f == 0)
    def _():
        xb_ref[...] = x1_ref[...].astype(xb_ref.dtype)
        acc_ref[...] = jnp.zeros_like(acc_ref)

    acc_ref[...] += _swiglu_chunk(xb_ref[...], wg_ref[...], wu_ref[...], wd_ref[...])

    @pl.when(f == pl.num_programs(1) - 1)
    def _():
        out_ref[...] = _layer_norm(DN_ALPHA * x1_ref[...] + acc_ref[...], g2_ref[...], b2_ref[...])


def ffn_dense(x1, wg, wu, wd, g2, b2):
    n = x1.shape[0]
    tm = min(TM_FFN, n)
    tf = TF
    row = lambda i, f: (i, 0)
    fixed = lambda i, f: (0, 0)
    return pl.pallas_call(
        _ffn_dense_kernel, grid=(n // tm, D_FF // tf),
        in_specs=[pl.BlockSpec((tm, D_MODEL), row),
                  pl.BlockSpec((D_MODEL, tf), lambda i, f: (0, f)),
                  pl.BlockSpec((D_MODEL, tf), lambda i, f: (0, f)),
                  pl.BlockSpec((tf, D_MODEL), lambda i, f: (f, 0)),
                  pl.BlockSpec((1, D_MODEL), fixed), pl.BlockSpec((1, D_MODEL), fixed)],
        out_specs=pl.BlockSpec((tm, D_MODEL), row),
        out_shape=jax.ShapeDtypeStruct((n, D_MODEL), jnp.float32),
        scratch_shapes=[pltpu.VMEM((tm, D_MODEL), MXU_DTYPE), pltpu.VMEM((tm, D_MODEL), jnp.float32)],
        compiler_params=_cparams(("parallel", "arbitrary")), name="ffn_dense",
    )(x1, wg, wu, wd, g2, b2)


def _moe_dispatch_kernel(st_ref, nb_ref, x_ref, lr_ref, xs_hbm, buf_ref, sem):
    i = pl.program_id(0)
    slot = i % 2
    n_rows = buf_ref.shape[1] // SUBLANES
    lr = lr_ref[0]
    rows = lax.broadcasted_iota(jnp.int32, (n_rows, lr.shape[1]), 0)
    sel = jnp.where(jnp.logical_or(rows == lr[0:1], rows == lr[1:2]), 1.0, 0.0)
    xc = _dot(sel, x_ref[...])
    for s in range(SUBLANES):
        buf_ref[slot, pl.ds(s, n_rows, stride=SUBLANES), :] = xc[:, s * LANES:(s + 1) * LANES]

    blk = DISP_BLK * SUBLANES

    def block_copy(src, dst, sl):
        return pltpu.make_async_copy(buf_ref.at[sl, pl.ds(src, blk), :], xs_hbm.at[pl.ds(dst, blk), :],
                                     sem.at[sl])

    def wait_tile(t, sl):
        total = nb_ref[t * N_EXPERTS]
        for e in range(1, N_EXPERTS):
            total = total + nb_ref[t * N_EXPERTS + e]

        def wait(k, c):
            block_copy(0, 0, sl).wait()
            return c

        lax.fori_loop(0, total, wait, 0)

    @pl.when(i > 0)
    def _():
        wait_tile(i - 1, 1 - slot)

    off = jnp.int32(0)
    for e in range(N_EXPERTS):
        a = st_ref[i * N_EXPERTS + e]
        nb = nb_ref[i * N_EXPERTS + e]

        def start(k, c, a=a, off=off):
            block_copy(pl.multiple_of((off + k * DISP_BLK) * SUBLANES, blk),
                       pl.multiple_of((a + k * DISP_BLK) * SUBLANES, SUBLANES), slot).start()
            return c

        lax.fori_loop(0, nb, start, 0)
        off = off + nb * DISP_BLK

    @pl.when(i == pl.num_programs(0) - 1)
    def _():
        wait_tile(i, slot)


def moe_dispatch(x1, row_start, blk_count, local_row, n_sorted_rows):
    n = x1.shape[0]
    tm = local_row.shape[2]
    buf_rows = -(-(2 * tm + N_EXPERTS * (DISP_BLK - 1)) // 16) * 16
    grid_spec = pltpu.PrefetchScalarGridSpec(
        num_scalar_prefetch=2, grid=(n // tm,),
        in_specs=[pl.BlockSpec((tm, D_MODEL), lambda i, st, nb: (i, 0)),
                  pl.BlockSpec((1, 2, tm), lambda i, st, nb: (i, 0, 0))],
        out_specs=pl.BlockSpec(memory_space=pl.ANY),
        scratch_shapes=[pltpu.VMEM((2, buf_rows * SUBLANES, LANES), jnp.float32),
                        pltpu.SemaphoreType.DMA((2,))],
    )
    return pl.pallas_call(
        _moe_dispatch_kernel, grid_spec=grid_spec,
        out_shape=jax.ShapeDtypeStruct((n_sorted_rows * SUBLANES, LANES), jnp.float32),
        compiler_params=_cparams(("arbitrary",)), name="moe_dispatch",
    )(row_start, blk_count, x1, local_row)


def _moe_ffn_kernel(te_ref, tr_ref, xs_ref, wg_ref, wu_ref, wd_ref, y_ref, xb_ref, acc_ref):
    i = pl.program_id(0)
    f = pl.program_id(1)
    n_valid = tr_ref[i]
    tm = xb_ref.shape[0]

    @pl.when(jnp.logical_and(n_valid > 0, f == 0))
    def _():
        live = lax.broadcasted_iota(jnp.int32, (tm, LANES), 0) < n_valid
        for s in range(SUBLANES):
            piece = xs_ref[pl.ds(s, tm, stride=SUBLANES), :]
            xb_ref[:, s * LANES:(s + 1) * LANES] = jnp.where(live, piece, 0.0).astype(xb_ref.dtype)
        acc_ref[...] = jnp.zeros_like(acc_ref)

    @pl.when(n_valid > 0)
    def _():
        acc_ref[...] += _swiglu_chunk(xb_ref[...], wg_ref[...], wu_ref[...], wd_ref[...])

    @pl.when(f == pl.num_programs(1) - 1)
    def _():
        y_ref[...] = jnp.where(n_valid > 0, acc_ref[...], 0.0).astype(y_ref.dtype)


def moe_ffn(x_sorted, tile_expert, tile_rows, wg, wu, wd):
    n_tiles = tile_expert.shape[0]
    tm = x_sorted.shape[0] // (n_tiles * SUBLANES)
    tf = TF
    grid_spec = pltpu.PrefetchScalarGridSpec(
        num_scalar_prefetch=2, grid=(n_tiles, D_FF // tf),
        in_specs=[
            pl.BlockSpec((tm * SUBLANES, LANES), lambda i, f, te, tr: (i, 0)),
            pl.BlockSpec((None, D_MODEL, tf), lambda i, f, te, tr: (te[i], 0, f)),
            pl.BlockSpec((None, D_MODEL, tf), lambda i, f, te, tr: (te[i], 0, f)),
            pl.BlockSpec((None, tf, D_MODEL), lambda i, f, te, tr: (te[i], f, 0)),
        ],
        out_specs=pl.BlockSpec((tm, D_MODEL), lambda i, f, te, tr: (i, 0)),
        scratch_shapes=[pltpu.VMEM((tm, D_MODEL), MXU_DTYPE), pltpu.VMEM((tm, D_MODEL), jnp.float32)],
    )
    return pl.pallas_call(
        _moe_ffn_kernel, grid_spec=grid_spec,
        out_shape=jax.ShapeDtypeStruct((n_tiles * tm, D_MODEL), MXU_DTYPE),
        compiler_params=_cparams(("arbitrary", "arbitrary")), name="moe_ffn",
    )(tile_expert, tile_rows, x_sorted, wg, wu, wd)


def _moe_combine_kernel(a_ref, nb_ref, x1_ref, route_ref, y_hbm, g2_ref, b2_ref, out_ref, ybuf_ref, sem):
    i = pl.program_id(0)
    slot = i % 2

    def block_copy(src, dst, sl):
        return pltpu.make_async_copy(y_hbm.at[pl.ds(src, CMB_BLK), :], ybuf_ref.at[sl, pl.ds(dst, CMB_BLK), :],
                                     sem.at[sl])

    def fetch_tile(t, sl):
        off = jnp.int32(0)
        for e in range(N_EXPERTS):
            a = a_ref[t * N_EXPERTS + e]
            nb = nb_ref[t * N_EXPERTS + e]

            def start(k, c, a=a, off=off):
                block_copy(pl.multiple_of(a + k * CMB_BLK, CMB_BLK), pl.multiple_of(off + k * CMB_BLK, CMB_BLK),
                           sl).start()
                return c

            lax.fori_loop(0, nb, start, 0)
            off = off + nb * CMB_BLK

    @pl.when(i == 0)
    def _():
        ybuf_ref[...] = jnp.zeros_like(ybuf_ref)
        fetch_tile(i, slot)

    total = nb_ref[i * N_EXPERTS]
    for e in range(1, N_EXPERTS):
        total = total + nb_ref[i * N_EXPERTS + e]

    def wait(k, c):
        block_copy(0, 0, slot).wait()
        return c

    lax.fori_loop(0, total, wait, 0)

    @pl.when(i + 1 < pl.num_programs(0))
    def _():
        fetch_tile(i + 1, 1 - slot)

    route = route_ref[...]
    col = lax.broadcasted_iota(jnp.int32, (route.shape[0], ybuf_ref.shape[1]), 1)
    sel = (jnp.where(col == route[:, 4:5].astype(jnp.int32), route[:, 2:3], 0.0)
           + jnp.where(col == route[:, 5:6].astype(jnp.int32), route[:, 3:4], 0.0))
    f = jnp.dot(sel.astype(ybuf_ref.dtype), ybuf_ref[slot], preferred_element_type=jnp.float32)
    out_ref[...] = _layer_norm(DN_ALPHA * x1_ref[...] + f, g2_ref[...], b2_ref[...])


def moe_combine(blk_start, blk_count, x1, route, y_sorted, g2, b2):
    n = x1.shape[0]
    tm = min(TM_CMB, n)
    buf_rows = -(-(2 * tm + N_EXPERTS * 2 * (CMB_BLK - 1)) // 256) * 256
    row = lambda i, a, nb: (i, 0)
    fixed = lambda i, a, nb: (0, 0)
    grid_spec = pltpu.PrefetchScalarGridSpec(
        num_scalar_prefetch=2, grid=(n // tm,),
        in_specs=[pl.BlockSpec((tm, D_MODEL), row), pl.BlockSpec((tm, LANES), row),
                  pl.BlockSpec(memory_space=pl.ANY),
                  pl.BlockSpec((1, D_MODEL), fixed), pl.BlockSpec((1, D_MODEL), fixed)],
        out_specs=pl.BlockSpec((tm, D_MODEL), row),
        scratch_shapes=[pltpu.VMEM((2, buf_rows, D_MODEL), y_sorted.dtype), pltpu.SemaphoreType.DMA((2,))],
    )
    return pl.pallas_call(
        _moe_combine_kernel, grid_spec=grid_spec,
        out_shape=jax.ShapeDtypeStruct((n, D_MODEL), jnp.float32),
        compiler_params=_cparams(("arbitrary",)), name="moe_combine",
    )(blk_start, blk_count, x1, route, y_sorted, g2, b2)


def _dispatch_plan(expert_idx, tm, tm_tok):
    n = expert_idx.shape[0]
    e_flat = expert_idx.reshape(-1)
    onehot = (e_flat[:, None] == jnp.arange(N_EXPERTS, dtype=jnp.int32)[None, :]).astype(jnp.int32)
    csum = jnp.cumsum(onehot, axis=0)
    before = csum - onehot
    rank = jnp.sum(before * onehot, axis=1)
    counts = csum[-1]
    tiles_per = (counts + (DISP_BLK - 1) + tm - 1) // tm
    tile_end = jnp.cumsum(tiles_per)
    tile_begin = tile_end - tiles_per
    group_start = tile_begin * tm
    pos_flat = jnp.sum(onehot * group_start[None, :], axis=1) + rank
    n_tiles = (2 * n) // tm + N_EXPERTS + 1
    t = jnp.arange(n_tiles, dtype=jnp.int32)
    last_valid = jnp.maximum(tile_end[-1] - 1, 0)
    t_eff = jnp.minimum(t, last_valid)
    tile_expert = jnp.minimum(jnp.sum((t_eff[:, None] >= tile_end[None, :]).astype(jnp.int32), axis=1),
                              N_EXPERTS - 1).astype(jnp.int32)
    of_tile = (tile_expert[:, None] == jnp.arange(N_EXPERTS, dtype=jnp.int32)[None, :]).astype(jnp.int32)
    rows_left = jnp.sum(of_tile * (counts - (t[:, None] - tile_begin[None, :]) * tm), axis=1)
    tile_rows = jnp.where(t < tile_end[-1], jnp.clip(rows_left, 0, tm), 0).astype(jnp.int32)

    first = before[::2 * tm_tok]
    cnt = jnp.concatenate([first[1:], counts[None, :]], axis=0) - first
    start = group_start[None, :] + first
    pair_onehot = onehot.reshape(-1, 2 * tm_tok, N_EXPERTS)

    d_count = (cnt + DISP_BLK - 1) // DISP_BLK
    d_off = (jnp.cumsum(d_count, axis=1) - d_count) * DISP_BLK
    d_local = rank + jnp.sum(pair_onehot * (d_off - first)[:, None, :], axis=2).reshape(-1)
    d_local = d_local.reshape(-1, tm_tok, 2).transpose(0, 2, 1)

    blk_start = (start // CMB_BLK) * CMB_BLK
    blk_count = jnp.where(cnt > 0, (start - blk_start + cnt + CMB_BLK - 1) // CMB_BLK, 0)
    buf_off = (jnp.cumsum(blk_count, axis=1) - blk_count) * CMB_BLK
    local = pos_flat + jnp.sum(pair_onehot * (buf_off - blk_start)[:, None, :], axis=2).reshape(-1)
    flat = lambda v: v.reshape(-1).astype(jnp.int32)
    return dict(n_tiles=n_tiles, tile_expert=tile_expert, tile_rows=tile_rows,
                row_start=flat(start), d_count=flat(d_count), d_local=d_local.astype(jnp.int32),
                blk_start=flat(blk_start), blk_count=flat(blk_count), c_local=local.reshape(n, 2))


def _pad_w_in(w):
    z64 = jnp.zeros((D_MODEL, KR_X1), w.dtype)
    z32 = jnp.zeros((D_MODEL, LANES - KR_X1 - MLA_ROPE), w.dtype)
    return jnp.concatenate([w[:, 0:768], w[:, 1440:1696], w[:, 1696:1952], w[:, 1152:1408],
                            w[:, 768:1152], z64, w[:, 1408:1440], z32], axis=1).astype(MXU_DTYPE)


def _rope_tables(s):
    pos = jnp.arange(s, dtype=jnp.float32)
    inv = ROPE_THETA ** (-jnp.arange(0, MLA_ROPE, 2, dtype=jnp.float32) / MLA_ROPE)
    ang = pos[:, None] * inv[None, :]
    cos, sin = jnp.cos(ang), jnp.sin(ang)
    zl = jnp.zeros((s, KR_X1), jnp.float32)
    zh = jnp.zeros((s, HALF_ROPE), jnp.float32)
    zr = jnp.zeros((s, LANES - KR_X1 - MLA_ROPE), jnp.float32)
    kc = jnp.concatenate([zl, cos, cos, zr], axis=1)
    ksa = jnp.concatenate([zl, -sin, zh, zr], axis=1)
    ksb = jnp.concatenate([zl, zh, sin, zr], axis=1)
    return cos.T, sin.T, kc, ksa, ksb


def _block_diag(w):
    nb, bw, _ = w.shape
    out = jnp.zeros((nb * bw, nb * bw), w.dtype)
    for i in range(nb):
        out = out.at[i * bw:(i + 1) * bw, i * bw:(i + 1) * bw].set(w[i])
    return out


def kernel(x, ln_in_g, ln_in_b, w_in, conv_w, q_norm_g, w_uq, kv_norm_g, w_ukv, lru_conv_w, lru_conv_b,
           lru_wa, lru_ba, lru_wi, lru_bi, lru_lam, mix_norm_g, w_out, ln1_g, ln1_b, dense_w_gate,
           dense_w_up, dense_w_down, moe_w_router, moe_w_gate, moe_w_up, moe_w_down, ln2_g, ln2_b):
    b, s, d = x.shape
    n = b * s
    rope = _rope_tables(s)
    vec = lambda v: v.reshape(1, -1)
    cur = x.reshape(n, d)
    for l in range(DEPTH):
        w_in_pad = _pad_w_in(w_in[l])
        wq_t = jnp.pad(w_uq[l].T.reshape(MLA_HEADS, MLA_NOPE + MLA_ROPE, MLA_Q_RANK),
                       ((0, 0), (0, HEAD_PAD - MLA_NOPE - MLA_ROPE), (0, 0))
                       ).reshape(MLA_HEADS * HEAD_PAD, MLA_Q_RANK).astype(MXU_DTYPE)
        wkv = w_ukv[l].reshape(MLA_KV_RANK, MLA_HEADS, MLA_NOPE + MLA_V)
        wk_pad = jnp.pad(wkv[:, :, :MLA_NOPE], ((0, 0), (0, 0), (0, HEAD_PAD - MLA_NOPE))
                         ).reshape(MLA_KV_RANK, MLA_HEADS * HEAD_PAD).astype(MXU_DTYPE)
        wv_t = wkv[:, :, MLA_NOPE:].reshape(MLA_KV_RANK, MLA_HEADS * MLA_V).T.astype(MXU_DTYPE)
        gate_w = [jnp.concatenate([_block_diag(lru_wa[l, dr]), _block_diag(lru_wi[l, dr])], axis=1
                                  ).astype(MXU_DTYPE) for dr in range(2)]
        gate_b = [jnp.concatenate([lru_ba[l, dr], lru_bi[l, dr]]).reshape(1, -1) for dr in range(2)]

        xn3, z3, q_t, k, v_t = in_proj(
            cur.reshape(b, s, d), w_in_pad, vec(q_norm_g[l]), vec(kv_norm_g[l]), wq_t, wk_pad, wv_t, rope,
            ln=(vec(ln_in_g), vec(ln_in_b)) if l == 0 else None)
        xn, z = xn3.reshape(n, d), z3.reshape(n, D_Z)
        o = attention(q_t, k, v_t).reshape(n, MLA_HEADS * MLA_V)
        h_f, y_conv = lru_scan(z3, lru_conv_w[l], vec(lru_conv_b[l]), gate_w[0], gate_b[0],
                               vec(lru_lam[l, 0]), short_w=conv_w[l])
        h_b = lru_scan(z3, lru_conv_w[l], vec(lru_conv_b[l]), gate_w[1], gate_b[1],
                       vec(lru_lam[l, 1]), reverse=True)
        mixer_args = (y_conv.reshape(n, CONV_DIM), o, h_f.reshape(n, LRU_DIM), h_b.reshape(n, LRU_DIM),
                      z, xn, vec(mix_norm_g[l]), w_out[l].astype(MXU_DTYPE), vec(ln1_g[l]), vec(ln1_b[l]))

        j = l // 2
        if l % 2 == 0:
            x1 = post_mixer(*mixer_args)
            cur = ffn_dense(x1, dense_w_gate[j], dense_w_up[j], dense_w_down[j], vec(ln2_g[l]), vec(ln2_b[l]))
        else:
            wr_hi = moe_w_router[j].astype(jnp.bfloat16).astype(jnp.float32)
            wr_lo = (moe_w_router[j] - wr_hi).astype(jnp.bfloat16).astype(jnp.float32)
            w_router = jnp.pad(jnp.concatenate([wr_hi, wr_lo], axis=1), ((0, 0), (0, LANES - 2 * N_EXPERTS)))
            x1, route = post_mixer(*mixer_args, w_router=w_router)
            expert_idx = route[:, 0:2].astype(jnp.int32)
            tm = min(TM_FFN, n)
            plan = _dispatch_plan(expert_idx, tm, min(TM_CMB, n))
            x_sorted = moe_dispatch(x1, plan["row_start"], plan["d_count"], plan["d_local"],
                                    plan["n_tiles"] * tm)
            y_sorted = moe_ffn(x_sorted, plan["tile_expert"], plan["tile_rows"],
                               moe_w_gate[j], moe_w_up[j], moe_w_down[j])
            route = jnp.concatenate([route[:, :4], plan["c_local"].astype(jnp.float32), route[:, 6:]], axis=1)
            cur = moe_combine(plan["blk_start"], plan["blk_count"], x1, route, y_sorted,
                              vec(ln2_g[l]), vec(ln2_b[l]))
    return cur.reshape(b, s, d)
```

```python
import functools

import jax
import jax.numpy as jnp
from jax import lax
from jax.experimental import pallas as pl
from jax.experimental.pallas import tpu as pltpu

D_MODEL = 1024
DEPTH = 2
CONV_DIM = 256
MLA_HEADS = 8
MLA_NOPE = 64
MLA_ROPE = 32
MLA_V = 64
MLA_Q_RANK = 384
MLA_KV_RANK = 256
LRU_DIM = 256
LRU_C = 8.0
ROPE_THETA = 10000.0
D_FF = 3584
N_EXPERTS = 8
DN_ALPHA = (2.0 * DEPTH) ** 0.25
LN_EPS = 1e-5
RMS_EPS = 1e-6

D_IN_PAD = 2048
HALF_ROPE = MLA_ROPE // 2
HEAD_PAD = 128
COL_CB, COL_CC, COL_CH, COL_LG, COL_LX = 0, 1, 2, 3, 4
D_Z = 1280
KR_X1 = 64
V_ROWS = MLA_V + 16

LANES = 128
SUBLANES = 8
VMEM_LIMIT = 56 * 1024 * 1024
MXU_DTYPE = jnp.bfloat16

TM = 1024
TS = 512
TQ = 512
ATTN_GROUP = 2
ATTN_HEADS = 4
TM_FFN = 1024
TF = 512
TM_CMB = 256
CMB_BLK = 16
DISP_BLK = 8
NEG_INF = float("-inf")
LOG2E = 1.4426950408889634


def _cparams(sem, vmem=VMEM_LIMIT, flags=None):
    return pltpu.CompilerParams(dimension_semantics=sem, vmem_limit_bytes=vmem, flags=flags)


def _layer_norm(x, g, b):
    mu = jnp.mean(x, axis=-1, keepdims=True)
    xc = x - mu
    var = jnp.mean(xc * xc, axis=-1, keepdims=True)
    return xc * lax.rsqrt(var + LN_EPS) * g + b


def _rms_norm(x, g):
    ms = jnp.mean(x * x, axis=-1, keepdims=True)
    return x * lax.rsqrt(ms + RMS_EPS) * g


def _dot(a, b):
    return jnp.dot(a.astype(MXU_DTYPE), b.astype(MXU_DTYPE), preferred_element_type=jnp.float32)


def _dot_nt(a, b):
    return lax.dot_general(a.astype(MXU_DTYPE), b.astype(MXU_DTYPE), (((1,), (1,)), ((), ())),
                           preferred_element_type=jnp.float32)


def _mla_outputs(ckv, cq, kr, gq_ref, gkv_ref, wq_ref, wk_ref, wv_ref, cos_ref, sin_ref, kc_ref, ksa_ref,
                 ksb_ref, q_out, k_out, v_out):
    cqn = _rms_norm(cq, gq_ref[...])
    ckvn = _rms_norm(ckv, gkv_ref[...])

    krope = (kr * kc_ref[...]
             + pltpu.roll(kr, LANES - HALF_ROPE, axis=1) * ksa_ref[...]
             + pltpu.roll(kr, HALF_ROPE, axis=1) * ksb_ref[...])
    k_all = _dot(ckvn, wk_ref[...])
    for h in range(MLA_HEADS):
        k_out[0, h] = (k_all[:, h * HEAD_PAD:(h + 1) * HEAD_PAD] + krope).astype(k_out.dtype)

    v_all = _dot_nt(wv_ref[...], ckvn)
    ones = jnp.ones((V_ROWS - MLA_V, v_all.shape[1]), jnp.float32)
    for h in range(MLA_HEADS):
        v_out[0, h, 0] = jnp.concatenate([v_all[h * MLA_V:(h + 1) * MLA_V], ones], axis=0).astype(v_out.dtype)

    q_all = _dot_nt(wq_ref[...], cqn)
    cos_t = cos_ref[...]
    sin_t = sin_ref[...]
    qscale = (MLA_NOPE + MLA_ROPE) ** -0.5 * LOG2E
    for h in range(MLA_HEADS):
        base = h * HEAD_PAD
        nope = q_all[base:base + MLA_NOPE]
        x1 = q_all[base + MLA_NOPE:base + MLA_NOPE + HALF_ROPE]
        x2 = q_all[base + MLA_NOPE + HALF_ROPE:base + MLA_NOPE + MLA_ROPE]
        zero = q_all[base + MLA_NOPE + MLA_ROPE:base + HEAD_PAD]
        qh = jnp.concatenate([nope, x1 * cos_t - x2 * sin_t, x2 * cos_t + x1 * sin_t, zero], axis=0)
        q_out[0, h] = (qh * qscale).astype(q_out.dtype)


def _project(xn, w_ref, z_ref, mla_refs):
    z = _dot(xn, w_ref[...])
    z_ref[0] = z[:, :D_Z]
    _mla_outputs(z[:, D_Z:D_Z + MLA_KV_RANK], z[:, D_Z + MLA_KV_RANK:D_Z + MLA_KV_RANK + MLA_Q_RANK],
                 z[:, D_IN_PAD - LANES:], *mla_refs)


def _in_proj_ln_kernel(x_ref, g_ref, b_ref, w_ref, *refs):
    mla_in, (xn_ref, z_ref), mla_out = refs[:10], refs[10:12], refs[12:]
    xn = _layer_norm(x_ref[0], g_ref[...], b_ref[...])
    xn_ref[0] = xn
    _project(xn, w_ref, z_ref, mla_in + mla_out)


def _in_proj_kernel(x_ref, w_ref, *refs):
    mla_in, z_ref, mla_out = refs[:10], refs[10], refs[11:]
    _project(x_ref[0], w_ref, z_ref, mla_in + mla_out)


def in_proj(x3, w_pad, gq, gkv, wq_t, wk_pad, wv_t, rope, ln=None):
    b, s, _ = x3.shape
    ts = min(TS, s)
    nt = s // ts
    cos_t, sin_t, kc, ksa, ksb = rope
    fixed = lambda bi, i: (0, 0)
    tile = lambda bi, i: (bi, i, 0)
    x_spec = pl.BlockSpec((1, ts, D_MODEL), tile)
    vec = pl.BlockSpec((1, D_MODEL), fixed)
    mla_specs = [
        pl.BlockSpec((1, MLA_Q_RANK), fixed),
        pl.BlockSpec((1, MLA_KV_RANK), fixed),
        pl.BlockSpec((MLA_HEADS * HEAD_PAD, MLA_Q_RANK), fixed),
        pl.BlockSpec((MLA_KV_RANK, MLA_HEADS * HEAD_PAD), fixed),
        pl.BlockSpec((MLA_HEADS * MLA_V, MLA_KV_RANK), fixed),
        pl.BlockSpec((HALF_ROPE, ts), lambda bi, i: (0, i)),
        pl.BlockSpec((HALF_ROPE, ts), lambda bi, i: (0, i)),
        pl.BlockSpec((ts, LANES), lambda bi, i: (i, 0)),
        pl.BlockSpec((ts, LANES), lambda bi, i: (i, 0)),
        pl.BlockSpec((ts, LANES), lambda bi, i: (i, 0)),
    ]
    mla_args = (gq, gkv, wq_t, wk_pad, wv_t, cos_t, sin_t, kc, ksa, ksb)
    w_spec = pl.BlockSpec((D_MODEL, D_IN_PAD), fixed)
    z_spec = pl.BlockSpec((1, ts, D_Z), tile)
    z_shape = jax.ShapeDtypeStruct((b, s, D_Z), jnp.float32)
    qkv_specs = [
        pl.BlockSpec((1, MLA_HEADS, HEAD_PAD, ts), lambda bi, i: (bi, 0, 0, i)),
        pl.BlockSpec((1, MLA_HEADS, ts, HEAD_PAD), lambda bi, i: (bi, 0, i, 0)),
        pl.BlockSpec((1, MLA_HEADS, 1, V_ROWS, ts), lambda bi, i: (bi, 0, i, 0, 0)),
    ]
    qkv_shape = [
        jax.ShapeDtypeStruct((b, MLA_HEADS, HEAD_PAD, s), MXU_DTYPE),
        jax.ShapeDtypeStruct((b, MLA_HEADS, s, HEAD_PAD), MXU_DTYPE),
        jax.ShapeDtypeStruct((b, MLA_HEADS, nt, V_ROWS, ts), MXU_DTYPE),
    ]
    if ln is None:
        z, q_t, k, v_t = pl.pallas_call(
            _in_proj_kernel, grid=(b, nt), in_specs=[x_spec, w_spec] + mla_specs,
            out_specs=[z_spec] + qkv_specs, out_shape=[z_shape] + qkv_shape,
            compiler_params=_cparams(("parallel", "parallel")), name="in_proj")(x3, w_pad, *mla_args)
        return x3, z, q_t, k, v_t
    g, bb = ln
    xn, z, q_t, k, v_t = pl.pallas_call(
        _in_proj_ln_kernel, grid=(b, nt), in_specs=[x_spec, vec, vec, w_spec] + mla_specs,
        out_specs=[x_spec, z_spec] + qkv_specs,
        out_shape=[jax.ShapeDtypeStruct((b, s, D_MODEL), jnp.float32), z_shape] + qkv_shape,
        compiler_params=_cparams(("parallel", "parallel")), name="in_proj_ln")(x3, g, bb, w_pad, *mla_args)
    return xn, z, q_t, k, v_t


def _attn_kernel(k_ref, q_ref, v_ref, o_ref, *scratch, n_chunks, tkc, group_size):
    n_heads = q_ref.shape[1]
    tq = q_ref.shape[3]
    s_bufs = [scratch[4 * hd:4 * hd + 2] for hd in range(n_heads)]
    p_bufs = [scratch[4 * hd + 2:4 * hd + 4] for hd in range(n_heads)]

    def scores(hd, c, s_ref):
        start = pl.multiple_of(c * tkc, tkc)
        s = jnp.dot(k_ref[0, hd, pl.ds(start, tkc), :], q_ref[0, hd], preferred_element_type=jnp.float32)
        s_ref[...] = s
        return jnp.max(s, axis=0, keepdims=True)

    def accumulate(hd, c, s_ref, p_ref, mx, m, acc):
        m_new = jnp.maximum(m, mx)
        alpha = jnp.exp2(m - m_new)
        p_ref[...] = jnp.exp2(s_ref[...] - m_new).astype(p_ref.dtype)
        pv = jnp.dot(v_ref[0, hd, c], p_ref[...], preferred_element_type=jnp.float32)
        return m_new, alpha * acc + pv

    def group(c0, state, prefetch_last):
        state = list(state)
        for g in range(group_size):
            for hd in range(n_heads):
                mx, m, acc = state[hd]
                mx_next = None
                if g + 1 < group_size or prefetch_last:
                    mx_next = scores(hd, c0 + g + 1, s_bufs[hd][(g + 1) % 2])
                m, acc = accumulate(hd, c0 + g, s_bufs[hd][g % 2], p_bufs[hd][g % 2], mx, m, acc)
                state[hd] = (mx_next, m, acc)
        return tuple(state)

    def body(j, state):
        return group(j * group_size, state, prefetch_last=True)

    state = tuple((scores(hd, 0, s_bufs[hd][0]), jnp.full((1, tq), NEG_INF, jnp.float32),
                   jnp.zeros((v_ref.shape[3], tq), jnp.float32)) for hd in range(n_heads))
    n_groups = n_chunks // group_size
    state = lax.fori_loop(0, n_groups - 1, body, state)
    state = group((n_groups - 1) * group_size, state, prefetch_last=False)
    o_t = jnp.concatenate([acc[:MLA_V] / acc[MLA_V:MLA_V + 1] for _, _, acc in state], axis=0)
    o_ref[0] = o_t.T.astype(o_ref.dtype)


def attention(q_t, k, v_t):
    b, h, _, s = q_t.shape
    nc, tkc = v_t.shape[2], v_t.shape[4]
    tq = min(TQ, s)
    nh = ATTN_HEADS
    group_size = min(ATTN_GROUP, nc)
    assert group_size % 2 == 0 and nc % group_size == 0, "chunk groups alternate two buffers"
    kern = functools.partial(_attn_kernel, n_chunks=nc, tkc=tkc, group_size=group_size)
    per_head = [pltpu.VMEM((tkc, tq), jnp.float32), pltpu.VMEM((tkc, tq), jnp.float32),
                pltpu.VMEM((tkc, tq), MXU_DTYPE), pltpu.VMEM((tkc, tq), MXU_DTYPE)]
    return pl.pallas_call(
        kern, grid=(b, h // nh, s // tq),
        in_specs=[
            pl.BlockSpec((1, nh, s, HEAD_PAD), lambda bi, hi, qi: (bi, hi, 0, 0)),
            pl.BlockSpec((1, nh, HEAD_PAD, tq), lambda bi, hi, qi: (bi, hi, 0, qi)),
            pl.BlockSpec((1, nh, nc, V_ROWS, tkc), lambda bi, hi, qi: (bi, hi, 0, 0, 0)),
        ],
        out_specs=pl.BlockSpec((1, tq, nh * MLA_V), lambda bi, hi, qi: (bi, qi, hi)),
        out_shape=jax.ShapeDtypeStruct((b, s, h * MLA_V), MXU_DTYPE),
        scratch_shapes=per_head * nh,
        compiler_params=_cparams(("parallel", "parallel", "parallel")), name="attention",
    )(k, q_t, v_t)


def _shift_rows(x, d, edge_rows, row):
    ts = x.shape[0]
    y = pltpu.roll(x, (-d) % ts, axis=0)
    if d < 0:
        return jnp.where(row == 0, edge_rows[0], y)
    for j in range(d):
        y = jnp.where(row == ts - d + j, edge_rows[j], y)
    return y


def _scan_rows(a, u, reverse):
    ts = a.shape[0]
    row = lax.broadcasted_iota(jnp.int32, a.shape, 0)
    d = 1
    while d < ts:
        if reverse:
            valid = row < ts - d
            shift = ts - d
        else:
            valid = row >= d
            shift = d
        a_sh = jnp.where(valid, pltpu.roll(a, shift, axis=0), 1.0)
        u_sh = jnp.where(valid, pltpu.roll(u, shift, axis=0), 0.0)
        u = u + a * u_sh
        a = a * a_sh
        d *= 2
    return a, u


def _lru_core(x, xp, xn, first, last, cw_ref, cb_ref, wg_ref, bg_ref, lam_ref, carry_ref, h_ref, reverse):
    ts = x.shape[0]
    row = lax.broadcasted_iota(jnp.int32, x.shape, 0)
    keep_prev = jnp.where(first, 0.0, 1.0)
    keep_next = jnp.where(last, 0.0, 1.0)
    prev_row = xp[SUBLANES - 1:SUBLANES] * keep_prev
    next0 = xn[0:1] * keep_next
    next1 = xn[1:2] * keep_next
    cw = cw_ref[...]
    xc = (cw[0:1] * _shift_rows(x, -1, [prev_row], row) + cw[1:2] * x
          + cw[2:3] * _shift_rows(x, 1, [next0], row)
          + cw[3:4] * _shift_rows(x, 2, [next0, next1], row) + cb_ref[...])
    gates = _dot(xc, wg_ref[...]) + bg_ref[...]
    rec = jax.nn.sigmoid(gates[:, :LRU_DIM])
    inp = jax.nn.sigmoid(gates[:, LRU_DIM:])
    neg_lam = -lam_ref[...]
    softplus = jnp.maximum(neg_lam, 0.0) + jnp.log(1.0 + jnp.exp(-jnp.abs(neg_lam)))
    log_a = -LRU_C * rec * softplus
    a = jnp.exp(log_a)
    u = jnp.sqrt(1.0 - a * a) * (inp * xc)
    a_cum, h0 = _scan_rows(a, u, reverse)

    @pl.when(pl.program_id(1) == 0)
    def _():
        carry_ref[...] = jnp.zeros_like(carry_ref)

    h = h0 + a_cum * carry_ref[0:1]
    h_ref[0] = h
    edge = h[0:1] if reverse else h[ts - 1:ts]
    carry_ref[...] = jnp.broadcast_to(edge, carry_ref.shape)


def _lru_fwd_kernel(x_ref, xp_ref, xn_ref, cc_ref, ccp_ref, ccn_ref, ch_ref, chp_ref, chn_ref, cbg_ref,
                    cw_ref, cb_ref, wg_ref, bg_ref, lam_ref, sw_ref, h_ref, y_ref, carry_ref):
    i = pl.program_id(1)
    first = i == 0
    last = i == pl.num_programs(1) - 1
    _lru_core(x_ref[0], xp_ref[0], xn_ref[0], first, last, cw_ref, cb_ref, wg_ref, bg_ref, lam_ref,
              carry_ref, h_ref, reverse=False)
    g = cc_ref[0] * ch_ref[0]
    row = lax.broadcasted_iota(jnp.int32, g.shape, 0)
    keep_prev = jnp.where(first, 0.0, 1.0)
    keep_next = jnp.where(last, 0.0, 1.0)
    g_prev = ccp_ref[0, SUBLANES - 1:SUBLANES] * chp_ref[0, SUBLANES - 1:SUBLANES] * keep_prev
    g_next = ccn_ref[0, 0:1] * chn_ref[0, 0:1] * keep_next
    sw = sw_ref[...]
    conv = (sw[0:1] * _shift_rows(g, -1, [g_prev], row) + sw[1:2] * g
            + sw[2:3] * _shift_rows(g, 1, [g_next], row))
    y_ref[0] = cbg_ref[0] * conv


def _lru_bwd_kernel(x_ref, xp_ref, xn_ref, cw_ref, cb_ref, wg_ref, bg_ref, lam_ref, h_ref, carry_ref):
    i = pl.program_id(1)
    nt = pl.num_programs(1)
    first = i == nt - 1
    last = i == 0
    _lru_core(x_ref[0], xp_ref[0], xn_ref[0], first, last, cw_ref, cb_ref, wg_ref, bg_ref, lam_ref,
              carry_ref, h_ref, reverse=True)


def lru_scan(z3, conv_w, conv_b, wg, bg, lam, short_w=None, reverse=False):
    b, s, _ = z3.shape
    ts = min(TS, s)
    nt = s // ts
    rb = ts // SUBLANES
    nrb = s // SUBLANES
    tile = (lambda i: nt - 1 - i) if reverse else (lambda i: i)

    def main(col):
        return pl.BlockSpec((1, ts, LRU_DIM), lambda bi, i: (bi, tile(i), col))

    def prev(col):
        return pl.BlockSpec((1, SUBLANES, LRU_DIM),
                            lambda bi, i: (bi, jnp.maximum(tile(i) * rb - 1, 0), col))

    def nxt(col):
        return pl.BlockSpec((1, SUBLANES, LRU_DIM),
                            lambda bi, i: (bi, jnp.minimum((tile(i) + 1) * rb, nrb - 1), col))

    fixed = lambda bi, i: (0, 0)
    par = [pl.BlockSpec((4, LRU_DIM), fixed), pl.BlockSpec((1, LRU_DIM), fixed),
           pl.BlockSpec((LRU_DIM, 2 * LRU_DIM), fixed), pl.BlockSpec((1, 2 * LRU_DIM), fixed),
           pl.BlockSpec((1, LRU_DIM), fixed)]
    h_shape = jax.ShapeDtypeStruct((b, s, LRU_DIM), jnp.float32)
    scratch = [pltpu.VMEM((SUBLANES, LRU_DIM), jnp.float32)]
    if reverse:
        return pl.pallas_call(
            _lru_bwd_kernel, grid=(b, nt),
            in_specs=[main(COL_LX), prev(COL_LX), nxt(COL_LX)] + par,
            out_specs=main(0), out_shape=h_shape, scratch_shapes=scratch,
            compiler_params=_cparams(("parallel", "arbitrary")), name="lru_bwd",
        )(z3, z3, z3, conv_w, conv_b, wg, bg, lam)
    return pl.pallas_call(
        _lru_fwd_kernel, grid=(b, nt),
        in_specs=[main(COL_LX), prev(COL_LX), nxt(COL_LX), main(COL_CC), prev(COL_CC), nxt(COL_CC),
                  main(COL_CH), prev(COL_CH), nxt(COL_CH), main(COL_CB)] + par
                 + [pl.BlockSpec((3, CONV_DIM), fixed)],
        out_specs=[main(0), main(0)], out_shape=[h_shape, h_shape], scratch_shapes=scratch,
        compiler_params=_cparams(("parallel", "arbitrary")), name="lru_fwd",
    )(z3, z3, z3, z3, z3, z3, z3, z3, z3, z3, conv_w, conv_b, wg, bg, lam, short_w)


def _post_mixer_body(yc_ref, o_ref, hf_ref, hb_ref, lg_ref, xn_ref, gm_ref, wo_ref, g1_ref, b1_ref):
    gm = gm_ref[...]
    y_lru = jax.nn.gelu(lg_ref[...], approximate=True) * (hf_ref[...] + hb_ref[...])
    y = jnp.concatenate([
        _rms_norm(yc_ref[...], gm[:, :CONV_DIM]),
        _rms_norm(o_ref[...].astype(jnp.float32), gm[:, CONV_DIM:CONV_DIM + MLA_HEADS * MLA_V]),
        _rms_norm(y_lru, gm[:, CONV_DIM + MLA_HEADS * MLA_V:]),
    ], axis=1)
    mix = _dot(y, wo_ref[...])
    return _layer_norm(DN_ALPHA * xn_ref[...] + mix, g1_ref[...], b1_ref[...])


def _post_mixer_kernel(yc_ref, o_ref, hf_ref, hb_ref, lg_ref, xn_ref, gm_ref, wo_ref, g1_ref, b1_ref,
                       x1_ref):
    x1_ref[...] = _post_mixer_body(yc_ref, o_ref, hf_ref, hb_ref, lg_ref, xn_ref, gm_ref, wo_ref, g1_ref,
                                   b1_ref)


def _post_mixer_router_kernel(yc_ref, o_ref, hf_ref, hb_ref, lg_ref, xn_ref, gm_ref, wo_ref, g1_ref,
                              b1_ref, wr_ref, x1_ref, route_ref):
    x1 = _post_mixer_body(yc_ref, o_ref, hf_ref, hb_ref, lg_ref, xn_ref, gm_ref, wo_ref, g1_ref, b1_ref)
    x1_ref[...] = x1
    wr = wr_ref[...].astype(jnp.bfloat16)
    x_hi = x1.astype(jnp.bfloat16)
    x_lo = (x1 - x_hi.astype(jnp.float32)).astype(jnp.bfloat16)
    t_hi = jnp.dot(x_hi, wr, preferred_element_type=jnp.float32)
    t_lo = jnp.dot(x_lo, wr, preferred_element_type=jnp.float32)
    logits = t_hi + t_lo + pltpu.roll(t_hi, LANES - N_EXPERTS, axis=1)
    lane = lax.broadcasted_iota(jnp.int32, logits.shape, 1)
    logits = jnp.where(lane < N_EXPERTS, logits, NEG_INF)
    v1 = jnp.max(logits, axis=1, keepdims=True)
    i1 = jnp.min(jnp.where(logits == v1, lane, LANES), axis=1, keepdims=True)
    rest = jnp.where(lane == i1, NEG_INF, logits)
    v2 = jnp.max(rest, axis=1, keepdims=True)
    i2 = jnp.min(jnp.where(rest == v2, lane, LANES), axis=1, keepdims=True)
    e = jnp.exp(v2 - v1)
    g_top = 1.0 / (1.0 + e)
    g_sec = e * g_top
    route_ref[...] = jnp.where(lane == 0, i1.astype(jnp.float32),
                               jnp.where(lane == 1, i2.astype(jnp.float32),
                                         jnp.where(lane == 2, g_top, jnp.where(lane == 3, g_sec, 0.0))))


def post_mixer(y_conv, o, h_f, h_b, z, xn, gm, wo, g1, b1, w_router=None):
    n = xn.shape[0]
    tm = min(TM, n)
    row = lambda i: (i, 0)
    fixed = lambda i: (0, 0)
    in_specs = [
        pl.BlockSpec((tm, CONV_DIM), row), pl.BlockSpec((tm, MLA_HEADS * MLA_V), row),
        pl.BlockSpec((tm, LRU_DIM), row), pl.BlockSpec((tm, LRU_DIM), row),
        pl.BlockSpec((tm, LRU_DIM), lambda i: (i, COL_LG)), pl.BlockSpec((tm, D_MODEL), row),
        pl.BlockSpec((1, D_MODEL), fixed), pl.BlockSpec((D_MODEL, D_MODEL), fixed),
        pl.BlockSpec((1, D_MODEL), fixed), pl.BlockSpec((1, D_MODEL), fixed),
    ]
    x_spec = pl.BlockSpec((tm, D_MODEL), row)
    x_shape = jax.ShapeDtypeStruct((n, D_MODEL), jnp.float32)
    args = (y_conv, o, h_f, h_b, z, xn, gm, wo, g1, b1)
    if w_router is None:
        return pl.pallas_call(
            _post_mixer_kernel, grid=(n // tm,), in_specs=in_specs, out_specs=x_spec, out_shape=x_shape,
            compiler_params=_cparams(("parallel",)), name="post_mixer")(*args)
    return pl.pallas_call(
        _post_mixer_router_kernel, grid=(n // tm,),
        in_specs=in_specs + [pl.BlockSpec((D_MODEL, LANES), fixed)],
        out_specs=[x_spec, pl.BlockSpec((tm, LANES), row)],
        out_shape=[x_shape, jax.ShapeDtypeStruct((n, LANES), jnp.float32)],
        compiler_params=_cparams(("parallel",)), name="post_mixer_router")(*args, w_router)


def _swiglu_chunk(xb, wg, wu, wd):
    gate = jnp.dot(xb, wg.astype(xb.dtype), preferred_element_type=jnp.float32)
    up = jnp.dot(xb, wu.astype(xb.dtype), preferred_element_type=jnp.float32)
    hidden = (jax.nn.silu(gate) * up).astype(xb.dtype)
    return jnp.dot(hidden, wd.astype(xb.dtype), preferred_element_type=jnp.float32)


def _ffn_dense_kernel(x1_ref, wg_ref, wu_ref, wd_ref, g2_ref, b2_ref, out_ref, xb_ref, acc_ref):
    f = pl.program_id(1)

    @pl.when(f == 0)
    def _():
        xb_ref[...] = x1_ref[...].astype(xb_ref.dtype)
        acc_ref[...] = jnp.zeros_like(acc_ref)

    acc_ref[...] += _swiglu_chunk(xb_ref[...], wg_ref[...], wu_ref[...], wd_ref[...])

    @pl.when(f == pl.num_programs(1) - 1)
    def _():
        out_ref[...] = _layer_norm(DN_ALPHA * x1_ref[...] + acc_ref[...], g2_ref[...], b2_ref[...])


def ffn_dense(x1, wg, wu, wd, g2, b2):
    n = x1.shape[0]
    tm = min(TM_FFN, n)
    tf = TF
    row = lambda i, f: (i, 0)
    fixed = lambda i, f: (0, 0)
    return pl.pallas_call(
        _ffn_dense_kernel, grid=(n // tm, D_FF // tf),
        in_specs=[pl.BlockSpec((tm, D_MODEL), row),
                  pl.BlockSpec((D_MODEL, tf), lambda i, f: (0, f)),
                  pl.BlockSpec((D_MODEL, tf), lambda i, f: (0, f)),
                  pl.BlockSpec((tf, D_MODEL), lambda i, f: (f, 0)),
                  pl.BlockSpec((1, D_MODEL), fixed), pl.BlockSpec((1, D_MODEL), fixed)],
        out_specs=pl.BlockSpec((tm, D_MODEL), row),
        out_shape=jax.ShapeDtypeStruct((n, D_MODEL), jnp.float32),
        scratch_shapes=[pltpu.VMEM((tm, D_MODEL), MXU_DTYPE), pltpu.VMEM((tm, D_MODEL), jnp.float32)],
        compiler_params=_cparams(("parallel", "arbitrary")), name="ffn_dense",
    )(x1, wg, wu, wd, g2, b2)


def _moe_dispatch_kernel(st_ref, nb_ref, x_ref, lr_ref, xs_hbm, buf_ref, sem):
    i = pl.program_id(0)
    slot = i % 2
    n_rows = buf_ref.shape[1] // SUBLANES
    lr = lr_ref[0]
    rows = lax.broadcasted_iota(jnp.int32, (n_rows, lr.shape[1]), 0)
    sel = jnp.where(jnp.logical_or(rows == lr[0:1], rows == lr[1:2]), 1.0, 0.0)
    xc = _dot(sel, x_ref[...])
    for s in range(SUBLANES):
        buf_ref[slot, pl.ds(s, n_rows, stride=SUBLANES), :] = xc[:, s * LANES:(s + 1) * LANES]

    blk = DISP_BLK * SUBLANES

    def block_copy(src, dst, sl):
        return pltpu.make_async_copy(buf_ref.at[sl, pl.ds(src, blk), :], xs_hbm.at[pl.ds(dst, blk), :],
                                     sem.at[sl])

    def wait_tile(t, sl):
        total = nb_ref[t * N_EXPERTS]
        for e in range(1, N_EXPERTS):
            total = total + nb_ref[t * N_EXPERTS + e]

        def wait(k, c):
            block_copy(0, 0, sl).wait()
            return c

        lax.fori_loop(0, total, wait, 0)

    @pl.when(i > 0)
    def _():
        wait_tile(i - 1, 1 - slot)

    off = jnp.int32(0)
    for e in range(N_EXPERTS):
        a = st_ref[i * N_EXPERTS + e]
        nb = nb_ref[i * N_EXPERTS + e]

        def start(k, c, a=a, off=off):
            block_copy(pl.multiple_of((off + k * DISP_BLK) * SUBLANES, blk),
                       pl.multiple_of((a + k * DISP_BLK) * SUBLANES, SUBLANES), slot).start()
            return c

        lax.fori_loop(0, nb, start, 0)
        off = off + nb * DISP_BLK

    @pl.when(i == pl.num_programs(0) - 1)
    def _():
        wait_tile(i, slot)


def moe_dispatch(x1, row_start, blk_count, local_row, n_sorted_rows):
    n = x1.shape[0]
    tm = local_row.shape[2]
    buf_rows = -(-(2 * tm + N_EXPERTS * (DISP_BLK - 1)) // 16) * 16
    grid_spec = pltpu.PrefetchScalarGridSpec(
        num_scalar_prefetch=2, grid=(n // tm,),
        in_specs=[pl.BlockSpec((tm, D_MODEL), lambda i, st, nb: (i, 0)),
                  pl.BlockSpec((1, 2, tm), lambda i, st, nb: (i, 0, 0))],
        out_specs=pl.BlockSpec(memory_space=pl.ANY),
        scratch_shapes=[pltpu.VMEM((2, buf_rows * SUBLANES, LANES), jnp.float32),
                        pltpu.SemaphoreType.DMA((2,))],
    )
    return pl.pallas_call(
        _moe_dispatch_kernel, grid_spec=grid_spec,
        out_shape=jax.ShapeDtypeStruct((n_sorted_rows * SUBLANES, LANES), jnp.float32),
        compiler_params=_cparams(("arbitrary",)), name="moe_dispatch",
    )(row_start, blk_count, x1, local_row)


def _moe_ffn_kernel(te_ref, tr_ref, xs_ref, wg_ref, wu_ref, wd_ref, y_ref, xb_ref, acc_ref):
    i = pl.program_id(0)
    f = pl.program_id(1)
    n_valid = tr_ref[i]
    tm = xb_ref.shape[0]

    @pl.when(jnp.logical_and(n_valid > 0, f == 0))
    def _():
        live = lax.broadcasted_iota(jnp.int32, (tm, LANES), 0) < n_valid
        for s in range(SUBLANES):
            piece = xs_ref[pl.ds(s, tm, stride=SUBLANES), :]
            xb_ref[:, s * LANES:(s + 1) * LANES] = jnp.where(live, piece, 0.0).astype(xb_ref.dtype)
        acc_ref[...] = jnp.zeros_like(acc_ref)

    @pl.when(n_valid > 0)
    def _():
        acc_ref[...] += _swiglu_chunk(xb_ref[...], wg_ref[...], wu_ref[...], wd_ref[...])

    @pl.when(f == pl.num_programs(1) - 1)
    def _():
        y_ref[...] = jnp.where(n_valid > 0, acc_ref[...], 0.0).astype(y_ref.dtype)


def moe_ffn(x_sorted, tile_expert, tile_rows, wg, wu, wd):
    n_tiles = tile_expert.shape[0]
    tm = x_sorted.shape[0] // (n_tiles * SUBLANES)
    tf = TF
    grid_spec = pltpu.PrefetchScalarGridSpec(
        num_scalar_prefetch=2, grid=(n_tiles, D_FF // tf),
        in_specs=[
            pl.BlockSpec((tm * SUBLANES, LANES), lambda i, f, te, tr: (i, 0)),
            pl.BlockSpec((None, D_MODEL, tf), lambda i, f, te, tr: (te[i], 0, f)),
            pl.BlockSpec((None, D_MODEL, tf), lambda i, f, te, tr: (te[i], 0, f)),
            pl.BlockSpec((None, tf, D_MODEL), lambda i, f, te, tr: (te[i], f, 0)),
        ],
        out_specs=pl.BlockSpec((tm, D_MODEL), lambda i, f, te, tr: (i, 0)),
        scratch_shapes=[pltpu.VMEM((tm, D_MODEL), MXU_DTYPE), pltpu.VMEM((tm, D_MODEL), jnp.float32)],
    )
    return pl.pallas_call(
        _moe_ffn_kernel, grid_spec=grid_spec,
        out_shape=jax.ShapeDtypeStruct((n_tiles * tm, D_MODEL), MXU_DTYPE),
        compiler_params=_cparams(("arbitrary", "arbitrary")), name="moe_ffn",
    )(tile_expert, tile_rows, x_sorted, wg, wu, wd)


def _moe_combine_kernel(a_ref, nb_ref, x1_ref, route_ref, y_hbm, g2_ref, b2_ref, out_ref, ybuf_ref, sem):
    i = pl.program_id(0)
    slot = i % 2

    def block_copy(src, dst, sl):
        return pltpu.make_async_copy(y_hbm.at[pl.ds(src, CMB_BLK), :], ybuf_ref.at[sl, pl.ds(dst, CMB_BLK), :],
                                     sem.at[sl])

    def fetch_tile(t, sl):
        off = jnp.int32(0)
        for e in range(N_EXPERTS):
            a = a_ref[t * N_EXPERTS + e]
            nb = nb_ref[t * N_EXPERTS + e]

            def start(k, c, a=a, off=off):
                block_copy(pl.multiple_of(a + k * CMB_BLK, CMB_BLK), pl.multiple_of(off + k * CMB_BLK, CMB_BLK),
                           sl).start()
                return c

            lax.fori_loop(0, nb, start, 0)
            off = off + nb * CMB_BLK

    @pl.when(i == 0)
    def _():
        ybuf_ref[...] = jnp.zeros_like(ybuf_ref)
        fetch_tile(i, slot)

    total = nb_ref[i * N_EXPERTS]
    for e in range(1, N_EXPERTS):
        total = total + nb_ref[i * N_EXPERTS + e]

    def wait(k, c):
        block_copy(0, 0, slot).wait()
        return c

    lax.fori_loop(0, total, wait, 0)

    @pl.when(i + 1 < pl.num_programs(0))
    def _():
        fetch_tile(i + 1, 1 - slot)

    route = route_ref[...]
    col = lax.broadcasted_iota(jnp.int32, (route.shape[0], ybuf_ref.shape[1]), 1)
    sel = (jnp.where(col == route[:, 4:5].astype(jnp.int32), route[:, 2:3], 0.0)
           + jnp.where(col == route[:, 5:6].astype(jnp.int32), route[:, 3:4], 0.0))
    f = jnp.dot(sel.astype(ybuf_ref.dtype), ybuf_ref[slot], preferred_element_type=jnp.float32)
    out_ref[...] = _layer_norm(DN_ALPHA * x1_ref[...] + f, g2_ref[...], b2_ref[...])


def moe_combine(blk_start, blk_count, x1, route, y_sorted, g2, b2):
    n = x1.shape[0]
    tm = min(TM_CMB, n)
    buf_rows = -(-(2 * tm + N_EXPERTS * 2 * (CMB_BLK - 1)) // 256) * 256
    row = lambda i, a, nb: (i, 0)
    fixed = lambda i, a, nb: (0, 0)
    grid_spec = pltpu.PrefetchScalarGridSpec(
        num_scalar_prefetch=2, grid=(n // tm,),
        in_specs=[pl.BlockSpec((tm, D_MODEL), row), pl.BlockSpec((tm, LANES), row),
                  pl.BlockSpec(memory_space=pl.ANY),
                  pl.BlockSpec((1, D_MODEL), fixed), pl.BlockSpec((1, D_MODEL), fixed)],
        out_specs=pl.BlockSpec((tm, D_MODEL), row),
        scratch_shapes=[pltpu.VMEM((2, buf_rows, D_MODEL), y_sorted.dtype), pltpu.SemaphoreType.DMA((2,))],
    )
    return pl.pallas_call(
        _moe_combine_kernel, grid_spec=grid_spec,
        out_shape=jax.ShapeDtypeStruct((n, D_MODEL), jnp.float32),
        compiler_params=_cparams(("arbitrary",)), name="moe_combine",
    )(blk_start, blk_count, x1, route, y_sorted, g2, b2)


def _dispatch_plan(expert_idx, tm, tm_tok):
    n = expert_idx.shape[0]
    e_flat = expert_idx.reshape(-1)
    onehot = (e_flat[:, None] == jnp.arange(N_EXPERTS, dtype=jnp.int32)[None, :]).astype(jnp.int32)
    csum = jnp.cumsum(onehot, axis=0)
    before = csum - onehot
    rank = jnp.sum(before * onehot, axis=1)
    counts = csum[-1]
    tiles_per = (counts + (DISP_BLK - 1) + tm - 1) // tm
    tile_end = jnp.cumsum(tiles_per)
    tile_begin = tile_end - tiles_per
    group_start = tile_begin * tm
    pos_flat = jnp.sum(onehot * group_start[None, :], axis=1) + rank
    n_tiles = (2 * n) // tm + N_EXPERTS + 1
    t = jnp.arange(n_tiles, dtype=jnp.int32)
    last_valid = jnp.maximum(tile_end[-1] - 1, 0)
    t_eff = jnp.minimum(t, last_valid)
    tile_expert = jnp.minimum(jnp.sum((t_eff[:, None] >= tile_end[None, :]).astype(jnp.int32), axis=1),
                              N_EXPERTS - 1).astype(jnp.int32)
    of_tile = (tile_expert[:, None] == jnp.arange(N_EXPERTS, dtype=jnp.int32)[None, :]).astype(jnp.int32)
    rows_left = jnp.sum(of_tile * (counts - (t[:, None] - tile_begin[None, :]) * tm), axis=1)
    tile_rows = jnp.where(t < tile_end[-1], jnp.clip(rows_left, 0, tm), 0).astype(jnp.int32)

    first = before[::2 * tm_tok]
    cnt = jnp.concatenate([first[1:], counts[None, :]], axis=0) - first
    start = group_start[None, :] + first
    pair_onehot = onehot.reshape(-1, 2 * tm_tok, N_EXPERTS)

    d_count = (cnt + DISP_BLK - 1) // DISP_BLK
    d_off = (jnp.cumsum(d_count, axis=1) - d_count) * DISP_BLK
    d_local = rank + jnp.sum(pair_onehot * (d_off - first)[:, None, :], axis=2).reshape(-1)
    d_local = d_local.reshape(-1, tm_tok, 2).transpose(0, 2, 1)

    blk_start = (start // CMB_BLK) * CMB_BLK
    blk_count = jnp.where(cnt > 0, (start - blk_start + cnt + CMB_BLK - 1) // CMB_BLK, 0)
    buf_off = (jnp.cumsum(blk_count, axis=1) - blk_count) * CMB_BLK
    local = pos_flat + jnp.sum(pair_onehot * (buf_off - blk_start)[:, None, :], axis=2).reshape(-1)
    flat = lambda v: v.reshape(-1).astype(jnp.int32)
    return dict(n_tiles=n_tiles, tile_expert=tile_expert, tile_rows=tile_rows,
                row_start=flat(start), d_count=flat(d_count), d_local=d_local.astype(jnp.int32),
                blk_start=flat(blk_start), blk_count=flat(blk_count), c_local=local.reshape(n, 2))


def _pad_w_in(w):
    z64 = jnp.zeros((D_MODEL, KR_X1), w.dtype)
    z32 = jnp.zeros((D_MODEL, LANES - KR_X1 - MLA_ROPE), w.dtype)
    return jnp.concatenate([w[:, 0:768], w[:, 1440:1696], w[:, 1696:1952], w[:, 1152:1408],
                            w[:, 768:1152], z64, w[:, 1408:1440], z32], axis=1).astype(MXU_DTYPE)


def _rope_tables(s):
    pos = jnp.arange(s, dtype=jnp.float32)
    inv = ROPE_THETA ** (-jnp.arange(0, MLA_ROPE, 2, dtype=jnp.float32) / MLA_ROPE)
    ang = pos[:, None] * inv[None, :]
    cos, sin = jnp.cos(ang), jnp.sin(ang)
    zl = jnp.zeros((s, KR_X1), jnp.float32)
    zh = jnp.zeros((s, HALF_ROPE), jnp.float32)
    zr = jnp.zeros((s, LANES - KR_X1 - MLA_ROPE), jnp.float32)
    kc = jnp.concatenate([zl, cos, cos, zr], axis=1)
    ksa = jnp.concatenate([zl, -sin, zh, zr], axis=1)
    ksb = jnp.concatenate([zl, zh, sin, zr], axis=1)
    return cos.T, sin.T, kc, ksa, ksb


def _block_diag(w):
    nb, bw, _ = w.shape
    out = jnp.zeros((nb * bw, nb * bw), w.dtype)
    for i in range(nb):
        out = out.at[i * bw:(i + 1) * bw, i * bw:(i + 1) * bw].set(w[i])
    return out


def kernel(x, ln_in_g, ln_in_b, w_in, conv_w, q_norm_g, w_uq, kv_norm_g, w_ukv, lru_conv_w, lru_conv_b,
           lru_wa, lru_ba, lru_wi, lru_bi, lru_lam, mix_norm_g, w_out, ln1_g, ln1_b, dense_w_gate,
           dense_w_up, dense_w_down, moe_w_router, moe_w_gate, moe_w_up, moe_w_down, ln2_g, ln2_b):
    b, s, d = x.shape
    n = b * s
    rope = _rope_tables(s)
    vec = lambda v: v.reshape(1, -1)
    cur = x.reshape(n, d)
    for l in range(DEPTH):
        w_in_pad = _pad_w_in(w_in[l])
        wq_t = jnp.pad(w_uq[l].T.reshape(MLA_HEADS, MLA_NOPE + MLA_ROPE, MLA_Q_RANK),
                       ((0, 0), (0, HEAD_PAD - MLA_NOPE - MLA_ROPE), (0, 0))
                       ).reshape(MLA_HEADS * HEAD_PAD, MLA_Q_RANK).astype(MXU_DTYPE)
        wkv = w_ukv[l].reshape(MLA_KV_RANK, MLA_HEADS, MLA_NOPE + MLA_V)
        wk_pad = jnp.pad(wkv[:, :, :MLA_NOPE], ((0, 0), (0, 0), (0, HEAD_PAD - MLA_NOPE))
                         ).reshape(MLA_KV_RANK, MLA_HEADS * HEAD_PAD).astype(MXU_DTYPE)
        wv_t = wkv[:, :, MLA_NOPE:].reshape(MLA_KV_RANK, MLA_HEADS * MLA_V).T.astype(MXU_DTYPE)
        gate_w = [jnp.concatenate([_block_diag(lru_wa[l, dr]), _block_diag(lru_wi[l, dr])], axis=1
                                  ).astype(MXU_DTYPE) for dr in range(2)]
        gate_b = [jnp.concatenate([lru_ba[l, dr], lru_bi[l, dr]]).reshape(1, -1) for dr in range(2)]

        xn3, z3, q_t, k, v_t = in_proj(
            cur.reshape(b, s, d), w_in_pad, vec(q_norm_g[l]), vec(kv_norm_g[l]), wq_t, wk_pad, wv_t, rope,
            ln=(vec(ln_in_g), vec(ln_in_b)) if l == 0 else None)
        xn, z = xn3.reshape(n, d), z3.reshape(n, D_Z)
        o = attention(q_t, k, v_t).reshape(n, MLA_HEADS * MLA_V)
        h_f, y_conv = lru_scan(z3, lru_conv_w[l], vec(lru_conv_b[l]), gate_w[0], gate_b[0],
                               vec(lru_lam[l, 0]), short_w=conv_w[l])
        h_b = lru_scan(z3, lru_conv_w[l], vec(lru_conv_b[l]), gate_w[1], gate_b[1],
                       vec(lru_lam[l, 1]), reverse=True)
        mixer_args = (y_conv.reshape(n, CONV_DIM), o, h_f.reshape(n, LRU_DIM), h_b.reshape(n, LRU_DIM),
                      z, xn, vec(mix_norm_g[l]), w_out[l].astype(MXU_DTYPE), vec(ln1_g[l]), vec(ln1_b[l]))

        j = l // 2
        if l % 2 == 0:
            x1 = post_mixer(*mixer_args)
            cur = ffn_dense(x1, dense_w_gate[j], dense_w_up[j], dense_w_down[j], vec(ln2_g[l]), vec(ln2_b[l]))
        else:
            wr_hi = moe_w_router[j].astype(jnp.bfloat16).astype(jnp.float32)
            wr_lo = (moe_w_router[j] - wr_hi).astype(jnp.bfloat16).astype(jnp.float32)
            w_router = jnp.pad(jnp.concatenate([wr_hi, wr_lo], axis=1), ((0, 0), (0, LANES - 2 * N_EXPERTS)))
            x1, route = post_mixer(*mixer_args, w_router=w_router)
            expert_idx = route[:, 0:2].astype(jnp.int32)
            tm = min(TM_FFN, n)
            plan = _dispatch_plan(expert_idx, tm, min(TM_CMB, n))
            x_sorted = moe_dispatch(x1, plan["row_start"], plan["d_count"], plan["d_local"],
                                    plan["n_tiles"] * tm)
            y_sorted = moe_ffn(x_sorted, plan["tile_expert"], plan["tile_rows"],
                               moe_w_gate[j], moe_w_up[j], moe_w_down[j])
            route = jnp.concatenate([route[:, :4], plan["c_local"].astype(jnp.float32), route[:, 6:]], axis=1)
            cur = moe_combine(plan["blk_start"], plan["blk_count"], x1, route, y_sorted,
                              vec(ln2_g[l]), vec(ln2_b[l]))
    return cur.reshape(b, s, d)
```

```python
import functools

import jax
import jax.numpy as jnp
from jax import lax
from jax.experimental import pallas as pl
from jax.experimental.pallas import tpu as pltpu

D_MODEL = 1024
DEPTH = 2
CONV_DIM = 256
MLA_HEADS = 8
MLA_NOPE = 64
MLA_ROPE = 32
MLA_V = 64
MLA_Q_RANK = 384
MLA_KV_RANK = 256
LRU_DIM = 256
LRU_C = 8.0
ROPE_THETA = 10000.0
D_FF = 3584
N_EXPERTS = 8
DN_ALPHA = (2.0 * DEPTH) ** 0.25
LN_EPS = 1e-5
RMS_EPS = 1e-6

D_IN_PAD = 2048
HALF_ROPE = MLA_ROPE // 2
HEAD_PAD = 128
COL_CB, COL_CC, COL_CH, COL_LG, COL_LX = 0, 1, 2, 3, 4
D_Z = 1280
KR_X1 = 64
V_ROWS = MLA_V + 16

LANES = 128
SUBLANES = 8
VMEM_LIMIT = 56 * 1024 * 1024
MXU_DTYPE = jnp.bfloat16

TM = 1024
TS = 512
TQ = 512
ATTN_GROUP = 2
ATTN_HEADS = 4
TM_FFN = 1024
TF = 512
TM_CMB = 256
CMB_BLK = 16
DISP_BLK = 8
NEG_INF = float("-inf")
LOG2E = 1.4426950408889634


def _cparams(sem, vmem=VMEM_LIMIT, flags=None):
    return pltpu.CompilerParams(dimension_semantics=sem, vmem_limit_bytes=vmem, flags=flags)


def _layer_norm(x, g, b):
    mu = jnp.mean(x, axis=-1, keepdims=True)
    xc = x - mu
    var = jnp.mean(xc * xc, axis=-1, keepdims=True)
    return xc * lax.rsqrt(var + LN_EPS) * g + b


def _rms_norm(x, g):
    ms = jnp.mean(x * x, axis=-1, keepdims=True)
    return x * lax.rsqrt(ms + RMS_EPS) * g


def _dot(a, b):
    return jnp.dot(a.astype(MXU_DTYPE), b.astype(MXU_DTYPE), preferred_element_type=jnp.float32)


def _dot_nt(a, b):
    return lax.dot_general(a.astype(MXU_DTYPE), b.astype(MXU_DTYPE), (((1,), (1,)), ((), ())),
                           preferred_element_type=jnp.float32)


def _mla_outputs(ckv, cq, kr, gq_ref, gkv_ref, wq_ref, wk_ref, wv_ref, cos_ref, sin_ref, kc_ref, ksa_ref,
                 ksb_ref, q_out, k_out, v_out):
    cqn = _rms_norm(cq, gq_ref[...])
    ckvn = _rms_norm(ckv, gkv_ref[...])

    krope = (kr * kc_ref[...]
             + pltpu.roll(kr, LANES - HALF_ROPE, axis=1) * ksa_ref[...]
             + pltpu.roll(kr, HALF_ROPE, axis=1) * ksb_ref[...])
    k_all = _dot(ckvn, wk_ref[...])
    for h in range(MLA_HEADS):
        k_out[0, h] = (k_all[:, h * HEAD_PAD:(h + 1) * HEAD_PAD] + krope).astype(k_out.dtype)

    v_all = _dot_nt(wv_ref[...], ckvn)
    ones = jnp.ones((V_ROWS - MLA_V, v_all.shape[1]), jnp.float32)
    for h in range(MLA_HEADS):
        v_out[0, h, 0] = jnp.concatenate([v_all[h * MLA_V:(h + 1) * MLA_V], ones], axis=0).astype(v_out.dtype)

    q_all = _dot_nt(wq_ref[...], cqn)
    cos_t = cos_ref[...]
    sin_t = sin_ref[...]
    qscale = (MLA_NOPE + MLA_ROPE) ** -0.5 * LOG2E
    for h in range(MLA_HEADS):
        base = h * HEAD_PAD
        nope = q_all[base:base + MLA_NOPE]
        x1 = q_all[base + MLA_NOPE:base + MLA_NOPE + HALF_ROPE]
        x2 = q_all[base + MLA_NOPE + HALF_ROPE:base + MLA_NOPE + MLA_ROPE]
        zero = q_all[base + MLA_NOPE + MLA_ROPE:base + HEAD_PAD]
        qh = jnp.concatenate([nope, x1 * cos_t - x2 * sin_t, x2 * cos_t + x1 * sin_t, zero], axis=0)
        q_out[0, h] = (qh * qscale).astype(q_out.dtype)


def _project(xn, w_ref, z_ref, mla_refs):
    z = _dot(xn, w_ref[...])
    z_ref[0] = z[:, :D_Z]
    _mla_outputs(z[:, D_Z:D_Z + MLA_KV_RANK], z[:, D_Z + MLA_KV_RANK:D_Z + MLA_KV_RANK + MLA_Q_RANK],
                 z[:, D_IN_PAD - LANES:], *mla_refs)


def _in_proj_ln_kernel(x_ref, g_ref, b_ref, w_ref, *refs):
    mla_in, (xn_ref, z_ref), mla_out = refs[:10], refs[10:12], refs[12:]
    xn = _layer_norm(x_ref[0], g_ref[...], b_ref[...])
    xn_ref[0] = xn
    _project(xn, w_ref, z_ref, mla_in + mla_out)


def _in_proj_kernel(x_ref, w_ref, *refs):
    mla_in, z_ref, mla_out = refs[:10], refs[10], refs[11:]
    _project(x_ref[0], w_ref, z_ref, mla_in + mla_out)


def in_proj(x3, w_pad, gq, gkv, wq_t, wk_pad, wv_t, rope, ln=None):
    b, s, _ = x3.shape
    ts = min(TS, s)
    nt = s // ts
    cos_t, sin_t, kc, ksa, ksb = rope
    fixed = lambda bi, i: (0, 0)
    tile = lambda bi, i: (bi, i, 0)
    x_spec = pl.BlockSpec((1, ts, D_MODEL), tile)
    vec = pl.BlockSpec((1, D_MODEL), fixed)
    mla_specs = [
        pl.BlockSpec((1, MLA_Q_RANK), fixed),
        pl.BlockSpec((1, MLA_KV_RANK), fixed),
        pl.BlockSpec((MLA_HEADS * HEAD_PAD, MLA_Q_RANK), fixed),
        pl.BlockSpec((MLA_KV_RANK, MLA_HEADS * HEAD_PAD), fixed),
        pl.BlockSpec((MLA_HEADS * MLA_V, MLA_KV_RANK), fixed),
        pl.BlockSpec((HALF_ROPE, ts), lambda bi, i: (0, i)),
        pl.BlockSpec((HALF_ROPE, ts), lambda bi, i: (0, i)),
        pl.BlockSpec((ts, LANES), lambda bi, i: (i, 0)),
        pl.BlockSpec((ts, LANES), lambda bi, i: (i, 0)),
        pl.BlockSpec((ts, LANES), lambda bi, i: (i, 0)),
    ]
    mla_args = (gq, gkv, wq_t, wk_pad, wv_t, cos_t, sin_t, kc, ksa, ksb)
    w_spec = pl.BlockSpec((D_MODEL, D_IN_PAD), fixed)
    z_spec = pl.BlockSpec((1, ts, D_Z), tile)
    z_shape = jax.ShapeDtypeStruct((b, s, D_Z), jnp.float32)
    qkv_specs = [
        pl.BlockSpec((1, MLA_HEADS, HEAD_PAD, ts), lambda bi, i: (bi, 0, 0, i)),
        pl.BlockSpec((1, MLA_HEADS, ts, HEAD_PAD), lambda bi, i: (bi, 0, i, 0)),
        pl.BlockSpec((1, MLA_HEADS, 1, V_ROWS, ts), lambda bi, i: (bi, 0, i, 0, 0)),
    ]
    qkv_shape = [
        jax.ShapeDtypeStruct((b, MLA_HEADS, HEAD_PAD, s), MXU_DTYPE),
        jax.ShapeDtypeStruct((b, MLA_HEADS, s, HEAD_PAD), MXU_DTYPE),
        jax.ShapeDtypeStruct((b, MLA_HEADS, nt, V_ROWS, ts), MXU_DTYPE),
    ]
    if ln is None:
        z, q_t, k, v_t = pl.pallas_call(
            _in_proj_kernel, grid=(b, nt), in_specs=[x_spec, w_spec] + mla_specs,
            out_specs=[z_spec] + qkv_specs, out_shape=[z_shape] + qkv_shape,
            compiler_params=_cparams(("parallel", "parallel")), name="in_proj")(x3, w_pad, *mla_args)
        return x3, z, q_t, k, v_t
    g, bb = ln
    xn, z, q_t, k, v_t = pl.pallas_call(
        _in_proj_ln_kernel, grid=(b, nt), in_specs=[x_spec, vec, vec, w_spec] + mla_specs,
        out_specs=[x_spec, z_spec] + qkv_specs,
        out_shape=[jax.ShapeDtypeStruct((b, s, D_MODEL), jnp.float32), z_shape] + qkv_shape,
        compiler_params=_cparams(("parallel", "parallel")), name="in_proj_ln")(x3, g, bb, w_pad, *mla_args)
    return xn, z, q_t, k, v_t


def _attn_kernel(k_ref, q_ref, v_ref, o_ref, *scratch, n_chunks, tkc, group_size):
    n_heads = q_ref.shape[1]
    tq = q_ref.shape[3]
    s_bufs = [scratch[4 * hd:4 * hd + 2] for hd in range(n_heads)]
    p_bufs = [scratch[4 * hd + 2:4 * hd + 4] for hd in range(n_heads)]

    def scores(hd, c, s_ref):
        start = pl.multiple_of(c * tkc, tkc)
        s = jnp.dot(k_ref[0, hd, pl.ds(start, tkc), :], q_ref[0, hd], preferred_element_type=jnp.float32)
        s_ref[...] = s
        return jnp.max(s, axis=0, keepdims=True)

    def accumulate(hd, c, s_ref, p_ref, mx, m, acc):
        m_new = jnp.maximum(m, mx)
        alpha = jnp.exp2(m - m_new)
        p_ref[...] = jnp.exp2(s_ref[...] - m_new).astype(p_ref.dtype)
        pv = jnp.dot(v_ref[0, hd, c], p_ref[...], preferred_element_type=jnp.float32)
        return m_new, alpha * acc + pv

    def group(c0, state, prefetch_last):
        state = list(state)
        for g in range(group_size):
            for hd in range(n_heads):
                mx, m, acc = state[hd]
                mx_next = None
                if g + 1 < group_size or prefetch_last:
                    mx_next = scores(hd, c0 + g + 1, s_bufs[hd][(g + 1) % 2])
                m, acc = accumulate(hd, c0 + g, s_bufs[hd][g % 2], p_bufs[hd][g % 2], mx, m, acc)
                state[hd] = (mx_next, m, acc)
        return tuple(state)

    def body(j, state):
        return group(j * group_size, state, prefetch_last=True)

    state = tuple((scores(hd, 0, s_bufs[hd][0]), jnp.full((1, tq), NEG_INF, jnp.float32),
                   jnp.zeros((v_ref.shape[3], tq), jnp.float32)) for hd in range(n_heads))
    n_groups = n_chunks // group_size
    state = lax.fori_loop(0, n_groups - 1, body, state)
    state = group((n_groups - 1) * group_size, state, prefetch_last=False)
    o_t = jnp.concatenate([acc[:MLA_V] / acc[MLA_V:MLA_V + 1] for _, _, acc in state], axis=0)
    o_ref[0] = o_t.T.astype(o_ref.dtype)


def attention(q_t, k, v_t):
    b, h, _, s = q_t.shape
    nc, tkc = v_t.shape[2], v_t.shape[4]
    tq = min(TQ, s)
    nh = ATTN_HEADS
    group_size = min(ATTN_GROUP, nc)
    assert group_size % 2 == 0 and nc % group_size == 0, "chunk groups alternate two buffers"
    kern = functools.partial(_attn_kernel, n_chunks=nc, tkc=tkc, group_size=group_size)
    per_head = [pltpu.VMEM((tkc, tq), jnp.float32), pltpu.VMEM((tkc, tq), jnp.float32),
                pltpu.VMEM((tkc, tq), MXU_DTYPE), pltpu.VMEM((tkc, tq), MXU_DTYPE)]
    return pl.pallas_call(
        kern, grid=(b, h // nh, s // tq),
        in_specs=[
            pl.BlockSpec((1, nh, s, HEAD_PAD), lambda bi, hi, qi: (bi, hi, 0, 0)),
            pl.BlockSpec((1, nh, HEAD_PAD, tq), lambda bi, hi, qi: (bi, hi, 0, qi)),
            pl.BlockSpec((1, nh, nc, V_ROWS, tkc), lambda bi, hi, qi: (bi, hi, 0, 0, 0)),
        ],
        out_specs=pl.BlockSpec((1, tq, nh * MLA_V), lambda bi, hi, qi: (bi, qi, hi)),
        out_shape=jax.ShapeDtypeStruct((b, s, h * MLA_V), MXU_DTYPE),
        scratch_shapes=per_head * nh,
        compiler_params=_cparams(("parallel", "parallel", "parallel")), name="attention",
    )(k, q_t, v_t)


def _shift_rows(x, d, edge_rows, row):
    ts = x.shape[0]
    y = pltpu.roll(x, (-d) % ts, axis=0)
    if d < 0:
        return jnp.where(row == 0, edge_rows[0], y)
    for j in range(d):
        y = jnp.where(row == ts - d + j, edge_rows[j], y)
    return y


def _scan_rows(a, u, reverse):
    ts = a.shape[0]
    row = lax.broadcasted_iota(jnp.int32, a.shape, 0)
    d = 1
    while d < ts:
        if reverse:
            valid = row < ts - d
            shift = ts - d
        else:
            valid = row >= d
            shift = d
        a_sh = jnp.where(valid, pltpu.roll(a, shift, axis=0), 1.0)
        u_sh = jnp.where(valid, pltpu.roll(u, shift, axis=0), 0.0)
        u = u + a * u_sh
        a = a * a_sh
        d *= 2
    return a, u


def _lru_core(x, xp, xn, first, last, cw_ref, cb_ref, wg_ref, bg_ref, lam_ref, carry_ref, h_ref, reverse):
    ts = x.shape[0]
    row = lax.broadcasted_iota(jnp.int32, x.shape, 0)
    keep_prev = jnp.where(first, 0.0, 1.0)
    keep_next = jnp.where(last, 0.0, 1.0)
    prev_row = xp[SUBLANES - 1:SUBLANES] * keep_prev
    next0 = xn[0:1] * keep_next
    next1 = xn[1:2] * keep_next
    cw = cw_ref[...]
    xc = (cw[0:1] * _shift_rows(x, -1, [prev_row], row) + cw[1:2] * x
          + cw[2:3] * _shift_rows(x, 1, [next0], row)
          + cw[3:4] * _shift_rows(x, 2, [next0, next1], row) + cb_ref[...])
    gates = _dot(xc, wg_ref[...]) + bg_ref[...]
    rec = jax.nn.sigmoid(gates[:, :LRU_DIM])
    inp = jax.nn.sigmoid(gates[:, LRU_DIM:])
    neg_lam = -lam_ref[...]
    softplus = jnp.maximum(neg_lam, 0.0) + jnp.log(1.0 + jnp.exp(-jnp.abs(neg_lam)))
    log_a = -LRU_C * rec * softplus
    a = jnp.exp(log_a)
    u = jnp.sqrt(1.0 - a * a) * (inp * xc)
    a_cum, h0 = _scan_rows(a, u, reverse)

    @pl.when(pl.program_id(1) == 0)
    def _():
        carry_ref[...] = jnp.zeros_like(carry_ref)

    h = h0 + a_cum * carry_ref[0:1]
    h_ref[0] = h.astype(h_ref.dtype)
    edge = h[0:1] if reverse else h[ts - 1:ts]
    carry_ref[...] = jnp.broadcast_to(edge, carry_ref.shape)


def _lru_fwd_kernel(x_ref, xp_ref, xn_ref, cc_ref, ccp_ref, ccn_ref, ch_ref, chp_ref, chn_ref, cbg_ref,
                    cw_ref, cb_ref, wg_ref, bg_ref, lam_ref, sw_ref, h_ref, y_ref, carry_ref):
    i = pl.program_id(1)
    first = i == 0
    last = i == pl.num_programs(1) - 1
    _lru_core(x_ref[0], xp_ref[0], xn_ref[0], first, last, cw_ref, cb_ref, wg_ref, bg_ref, lam_ref,
              carry_ref, h_ref, reverse=False)
    g = cc_ref[0] * ch_ref[0]
    row = lax.broadcasted_iota(jnp.int32, g.shape, 0)
    keep_prev = jnp.where(first, 0.0, 1.0)
    keep_next = jnp.where(last, 0.0, 1.0)
    g_prev = ccp_ref[0, SUBLANES - 1:SUBLANES] * chp_ref[0, SUBLANES - 1:SUBLANES] * keep_prev
    g_next = ccn_ref[0, 0:1] * chn_ref[0, 0:1] * keep_next
    sw = sw_ref[...]
    conv = (sw[0:1] * _shift_rows(g, -1, [g_prev], row) + sw[1:2] * g
            + sw[2:3] * _shift_rows(g, 1, [g_next], row))
    y_ref[0] = (cbg_ref[0] * conv).astype(y_ref.dtype)


def _lru_bwd_kernel(x_ref, xp_ref, xn_ref, cw_ref, cb_ref, wg_ref, bg_ref, lam_ref, h_ref, carry_ref):
    i = pl.program_id(1)
    nt = pl.num_programs(1)
    first = i == nt - 1
    last = i == 0
    _lru_core(x_ref[0], xp_ref[0], xn_ref[0], first, last, cw_ref, cb_ref, wg_ref, bg_ref, lam_ref,
              carry_ref, h_ref, reverse=True)


def lru_scan(z3, conv_w, conv_b, wg, bg, lam, short_w=None, reverse=False):
    b, s, _ = z3.shape
    ts = min(TS, s)
    nt = s // ts
    rb = ts // SUBLANES
    nrb = s // SUBLANES
    tile = (lambda i: nt - 1 - i) if reverse else (lambda i: i)

    def main(col):
        return pl.BlockSpec((1, ts, LRU_DIM), lambda bi, i: (bi, tile(i), col))

    def prev(col):
        return pl.BlockSpec((1, SUBLANES, LRU_DIM),
                            lambda bi, i: (bi, jnp.maximum(tile(i) * rb - 1, 0), col))

    def nxt(col):
        return pl.BlockSpec((1, SUBLANES, LRU_DIM),
                            lambda bi, i: (bi, jnp.minimum((tile(i) + 1) * rb, nrb - 1), col))

    fixed = lambda bi, i: (0, 0)
    par = [pl.BlockSpec((4, LRU_DIM), fixed), pl.BlockSpec((1, LRU_DIM), fixed),
           pl.BlockSpec((LRU_DIM, 2 * LRU_DIM), fixed), pl.BlockSpec((1, 2 * LRU_DIM), fixed),
           pl.BlockSpec((1, LRU_DIM), fixed)]
    h_shape = jax.ShapeDtypeStruct((b, s, LRU_DIM), MXU_DTYPE)
    scratch = [pltpu.VMEM((SUBLANES, LRU_DIM), jnp.float32)]
    if reverse:
        return pl.pallas_call(
            _lru_bwd_kernel, grid=(b, nt),
            in_specs=[main(COL_LX), prev(COL_LX), nxt(COL_LX)] + par,
            out_specs=main(0), out_shape=h_shape, scratch_shapes=scratch,
            compiler_params=_cparams(("parallel", "arbitrary")), name="lru_bwd",
        )(z3, z3, z3, conv_w, conv_b, wg, bg, lam)
    return pl.pallas_call(
        _lru_fwd_kernel, grid=(b, nt),
        in_specs=[main(COL_LX), prev(COL_LX), nxt(COL_LX), main(COL_CC), prev(COL_CC), nxt(COL_CC),
                  main(COL_CH), prev(COL_CH), nxt(COL_CH), main(COL_CB)] + par
                 + [pl.BlockSpec((3, CONV_DIM), fixed)],
        out_specs=[main(0), main(0)], out_shape=[h_shape, h_shape], scratch_shapes=scratch,
        compiler_params=_cparams(("parallel", "arbitrary")), name="lru_fwd",
    )(z3, z3, z3, z3, z3, z3, z3, z3, z3, z3, conv_w, conv_b, wg, bg, lam, short_w)


def _post_mixer_body(yc_ref, o_ref, hf_ref, hb_ref, lg_ref, xn_ref, gm_ref, wo_ref, g1_ref, b1_ref):
    gm = gm_ref[...]
    y_lru = jax.nn.gelu(lg_ref[...], approximate=True) * (hf_ref[...].astype(jnp.float32)
                                                          + hb_ref[...].astype(jnp.float32))
    y = jnp.concatenate([
        _rms_norm(yc_ref[...].astype(jnp.float32), gm[:, :CONV_DIM]),
        _rms_norm(o_ref[...].astype(jnp.float32), gm[:, CONV_DIM:CONV_DIM + MLA_HEADS * MLA_V]),
        _rms_norm(y_lru, gm[:, CONV_DIM + MLA_HEADS * MLA_V:]),
    ], axis=1)
    mix = _dot(y, wo_ref[...])
    return _layer_norm(DN_ALPHA * xn_ref[...] + mix, g1_ref[...], b1_ref[...])


def _post_mixer_kernel(yc_ref, o_ref, hf_ref, hb_ref, lg_ref, xn_ref, gm_ref, wo_ref, g1_ref, b1_ref,
                       x1_ref):
    x1_ref[...] = _post_mixer_body(yc_ref, o_ref, hf_ref, hb_ref, lg_ref, xn_ref, gm_ref, wo_ref, g1_ref,
                                   b1_ref)


def _post_mixer_router_kernel(yc_ref, o_ref, hf_ref, hb_ref, lg_ref, xn_ref, gm_ref, wo_ref, g1_ref,
                              b1_ref, wr_ref, x1_ref, route_ref):
    x1 = _post_mixer_body(yc_ref, o_ref, hf_ref, hb_ref, lg_ref, xn_ref, gm_ref, wo_ref, g1_ref, b1_ref)
    x1_ref[...] = x1
    wr = wr_ref[...].astype(jnp.bfloat16)
    x_hi = x1.astype(jnp.bfloat16)
    x_lo = (x1 - x_hi.astype(jnp.float32)).astype(jnp.bfloat16)
    t_hi = jnp.dot(x_hi, wr, preferred_element_type=jnp.float32)
    t_lo = jnp.dot(x_lo, wr, preferred_element_type=jnp.float32)
    logits = t_hi + t_lo + pltpu.roll(t_hi, LANES - N_EXPERTS, axis=1)
    lane = lax.broadcasted_iota(jnp.int32, logits.shape, 1)
    logits = jnp.where(lane < N_EXPERTS, logits, NEG_INF)
    v1 = jnp.max(logits, axis=1, keepdims=True)
    i1 = jnp.min(jnp.where(logits == v1, lane, LANES), axis=1, keepdims=True)
    rest = jnp.where(lane == i1, NEG_INF, logits)
    v2 = jnp.max(rest, axis=1, keepdims=True)
    i2 = jnp.min(jnp.where(rest == v2, lane, LANES), axis=1, keepdims=True)
    e = jnp.exp(v2 - v1)
    g_top = 1.0 / (1.0 + e)
    g_sec = e * g_top
    route_ref[...] = jnp.where(lane == 0, i1.astype(jnp.float32),
                               jnp.where(lane == 1, i2.astype(jnp.float32),
                                         jnp.where(lane == 2, g_top, jnp.where(lane == 3, g_sec, 0.0))))


def post_mixer(y_conv, o, h_f, h_b, z, xn, gm, wo, g1, b1, w_router=None):
    n = xn.shape[0]
    tm = min(TM, n)
    row = lambda i: (i, 0)
    fixed = lambda i: (0, 0)
    in_specs = [
        pl.BlockSpec((tm, CONV_DIM), row), pl.BlockSpec((tm, MLA_HEADS * MLA_V), row),
        pl.BlockSpec((tm, LRU_DIM), row), pl.BlockSpec((tm, LRU_DIM), row),
        pl.BlockSpec((tm, LRU_DIM), lambda i: (i, COL_LG)), pl.BlockSpec((tm, D_MODEL), row),
        pl.BlockSpec((1, D_MODEL), fixed), pl.BlockSpec((D_MODEL, D_MODEL), fixed),
        pl.BlockSpec((1, D_MODEL), fixed), pl.BlockSpec((1, D_MODEL), fixed),
    ]
    x_spec = pl.BlockSpec((tm, D_MODEL), row)
    x_shape = jax.ShapeDtypeStruct((n, D_MODEL), jnp.float32)
    args = (y_conv, o, h_f, h_b, z, xn, gm, wo, g1, b1)
    if w_router is None:
        return pl.pallas_call(
            _post_mixer_kernel, grid=(n // tm,), in_specs=in_specs, out_specs=x_spec, out_shape=x_shape,
            compiler_params=_cparams(("parallel",)), name="post_mixer")(*args)
    return pl.pallas_call(
        _post_mixer_router_kernel, grid=(n // tm,),
        in_specs=in_specs + [pl.BlockSpec((D_MODEL, LANES), fixed)],
        out_specs=[x_spec, pl.BlockSpec((tm, LANES), row)],
        out_shape=[x_shape, jax.ShapeDtypeStruct((n, LANES), jnp.float32)],
        compiler_params=_cparams(("parallel",)), name="post_mixer_router")(*args, w_router)


def _swiglu_chunk(xb, wg, wu, wd):
    gate = jnp.dot(xb, wg.astype(xb.dtype), preferred_element_type=jnp.float32)
    up = jnp.dot(xb, wu.astype(xb.dtype), preferred_element_type=jnp.float32)
    hidden = (jax.nn.silu(gate) * up).astype(xb.dtype)
    return jnp.dot(hidden, wd.astype(xb.dtype), preferred_element_type=jnp.float32)


def _ffn_dense_kernel(x1_ref, wg_ref, wu_ref, wd_ref, g2_ref, b2_ref, out_ref, xb_ref, acc_ref):
    f = pl.program_id(1)

    @pl.when(f == 0)
    def _():
        xb_ref[...] = x1_ref[...].astype(xb_ref.dtype)
        acc_ref[...] = jnp.zeros_like(acc_ref)

    acc_ref[...] += _swiglu_chunk(xb_ref[...], wg_ref[...], wu_ref[...], wd_ref[...])

    @pl.when(f == pl.num_programs(1) - 1)
    def _():
        out_ref[...] = _layer_norm(DN_ALPHA * x1_ref[...] + acc_ref[...], g2_ref[...], b2_ref[...])


def ffn_dense(x1, wg, wu, wd, g2, b2):
    n = x1.shape[0]
    tm = min(TM_FFN, n)
    tf = TF
    row = lambda i, f: (i, 0)
    fixed = lambda i, f: (0, 0)
    return pl.pallas_call(
        _ffn_dense_kernel, grid=(n // tm, D_FF // tf),
        in_specs=[pl.BlockSpec((tm, D_MODEL), row),
                  pl.BlockSpec((D_MODEL, tf), lambda i, f: (0, f)),
                  pl.BlockSpec((D_MODEL, tf), lambda i, f: (0, f)),
                  pl.BlockSpec((tf, D_MODEL), lambda i, f: (f, 0)),
                  pl.BlockSpec((1, D_MODEL), fixed), pl.BlockSpec((1, D_MODEL), fixed)],
        out_specs=pl.BlockSpec((tm, D_MODEL), row),
        out_shape=jax.ShapeDtypeStruct((n, D_MODEL), jnp.float32),
        scratch_shapes=[pltpu.VMEM((tm, D_MODEL), MXU_DTYPE), pltpu.VMEM((tm, D_MODEL), jnp.float32)],
        compiler_params=_cparams(("parallel", "arbitrary")), name="ffn_dense",
    )(x1, wg, wu, wd, g2, b2)


def _moe_dispatch_kernel(st_ref, nb_ref, x_ref, lr_ref, xs_hbm, buf_ref, sem):
    i = pl.program_id(0)
    slot = i % 2
    n_rows = buf_ref.shape[1] // SUBLANES
    lr = lr_ref[0]
    rows = lax.broadcasted_iota(jnp.int32, (n_rows, lr.shape[1]), 0)
    sel = jnp.where(jnp.logical_or(rows == lr[0:1], rows == lr[1:2]), 1.0, 0.0)
    xc = _dot(sel, x_ref[...])
    for s in range(SUBLANES):
        buf_ref[slot, pl.ds(s, n_rows, stride=SUBLANES), :] = xc[:, s * LANES:(s + 1) * LANES]

    blk = DISP_BLK * SUBLANES

    def block_copy(src, dst, sl):
        return pltpu.make_async_copy(buf_ref.at[sl, pl.ds(src, blk), :], xs_hbm.at[pl.ds(dst, blk), :],
                                     sem.at[sl])

    def wait_tile(t, sl):
        total = nb_ref[t * N_EXPERTS]
        for e in range(1, N_EXPERTS):
            total = total + nb_ref[t * N_EXPERTS + e]

        def wait(k, c):
            block_copy(0, 0, sl).wait()
            return c

        lax.fori_loop(0, total, wait, 0)

    @pl.when(i > 0)
    def _():
        wait_tile(i - 1, 1 - slot)

    off = jnp.int32(0)
    for e in range(N_EXPERTS):
        a = st_ref[i * N_EXPERTS + e]
        nb = nb_ref[i * N_EXPERTS + e]

        def start(k, c, a=a, off=off):
            block_copy(pl.multiple_of((off + k * DISP_BLK) * SUBLANES, blk),
                       pl.multiple_of((a + k * DISP_BLK) * SUBLANES, SUBLANES), slot).start()
            return c

        lax.fori_loop(0, nb, start, 0)
        off = off + nb * DISP_BLK

    @pl.when(i == pl.num_programs(0) - 1)
    def _():
        wait_tile(i, slot)


def moe_dispatch(x1, row_start, blk_count, local_row, n_sorted_rows):
    n = x1.shape[0]
    tm = local_row.shape[2]
    buf_rows = -(-(2 * tm + N_EXPERTS * (DISP_BLK - 1)) // 16) * 16
    grid_spec = pltpu.PrefetchScalarGridSpec(
        num_scalar_prefetch=2, grid=(n // tm,),
        in_specs=[pl.BlockSpec((tm, D_MODEL), lambda i, st, nb: (i, 0)),
                  pl.BlockSpec((1, 2, tm), lambda i, st, nb: (i, 0, 0))],
        out_specs=pl.BlockSpec(memory_space=pl.ANY),
        scratch_shapes=[pltpu.VMEM((2, buf_rows * SUBLANES, LANES), jnp.float32),
                        pltpu.SemaphoreType.DMA((2,))],
    )
    return pl.pallas_call(
        _moe_dispatch_kernel, grid_spec=grid_spec,
        out_shape=jax.ShapeDtypeStruct((n_sorted_rows * SUBLANES, LANES), jnp.float32),
        compiler_params=_cparams(("arbitrary",)), name="moe_dispatch",
    )(row_start, blk_count, x1, local_row)


def _moe_ffn_kernel(te_ref, tr_ref, xs_ref, wg_ref, wu_ref, wd_ref, y_ref, xb_ref, acc_ref):
    i = pl.program_id(0)
    f = pl.program_id(1)
    n_valid = tr_ref[i]
    tm = xb_ref.shape[0]

    @pl.when(jnp.logical_and(n_valid > 0, f == 0))
    def _():
        live = lax.broadcasted_iota(jnp.int32, (tm, LANES), 0) < n_valid
        for s in range(SUBLANES):
            piece = xs_ref[pl.ds(s, tm, stride=SUBLANES), :]
            xb_ref[:, s * LANES:(s + 1) * LANES] = jnp.where(live, piece, 0.0).astype(xb_ref.dtype)
        acc_ref[...] = jnp.zeros_like(acc_ref)

    @pl.when(n_valid > 0)
    def _():
        acc_ref[...] += _swiglu_chunk(xb_ref[...], wg_ref[...], wu_ref[...], wd_ref[...])

    @pl.when(f == pl.num_programs(1) - 1)
    def _():
        y_ref[...] = jnp.where(n_valid > 0, acc_ref[...], 0.0).astype(y_ref.dtype)


def moe_ffn(x_sorted, tile_expert, tile_rows, wg, wu, wd):
    n_tiles = tile_expert.shape[0]
    tm = x_sorted.shape[0] // (n_tiles * SUBLANES)
    tf = TF
    grid_spec = pltpu.PrefetchScalarGridSpec(
        num_scalar_prefetch=2, grid=(n_tiles, D_FF // tf),
        in_specs=[
            pl.BlockSpec((tm * SUBLANES, LANES), lambda i, f, te, tr: (i, 0)),
            pl.BlockSpec((None, D_MODEL, tf), lambda i, f, te, tr: (te[i], 0, f)),
            pl.BlockSpec((None, D_MODEL, tf), lambda i, f, te, tr: (te[i], 0, f)),
            pl.BlockSpec((None, tf, D_MODEL), lambda i, f, te, tr: (te[i], f, 0)),
        ],
        out_specs=pl.BlockSpec((tm, D_MODEL), lambda i, f, te, tr: (i, 0)),
        scratch_shapes=[pltpu.VMEM((tm, D_MODEL), MXU_DTYPE), pltpu.VMEM((tm, D_MODEL), jnp.float32)],
    )
    return pl.pallas_call(
        _moe_ffn_kernel, grid_spec=grid_spec,
        out_shape=jax.ShapeDtypeStruct((n_tiles * tm, D_MODEL), MXU_DTYPE),
        compiler_params=_cparams(("arbitrary", "arbitrary")), name="moe_ffn",
    )(tile_expert, tile_rows, x_sorted, wg, wu, wd)


def _moe_combine_kernel(a_ref, nb_ref, x1_ref, route_ref, y_hbm, g2_ref, b2_ref, out_ref, ybuf_ref, sem):
    i = pl.program_id(0)
    slot = i % 2

    def block_copy(src, dst, sl):
        return pltpu.make_async_copy(y_hbm.at[pl.ds(src, CMB_BLK), :], ybuf_ref.at[sl, pl.ds(dst, CMB_BLK), :],
                                     sem.at[sl])

    def fetch_tile(t, sl):
        off = jnp.int32(0)
        for e in range(N_EXPERTS):
            a = a_ref[t * N_EXPERTS + e]
            nb = nb_ref[t * N_EXPERTS + e]

            def start(k, c, a=a, off=off):
                block_copy(pl.multiple_of(a + k * CMB_BLK, CMB_BLK), pl.multiple_of(off + k * CMB_BLK, CMB_BLK),
                           sl).start()
                return c

            lax.fori_loop(0, nb, start, 0)
            off = off + nb * CMB_BLK

    @pl.when(i == 0)
    def _():
        ybuf_ref[...] = jnp.zeros_like(ybuf_ref)
        fetch_tile(i, slot)

    total = nb_ref[i * N_EXPERTS]
    for e in range(1, N_EXPERTS):
        total = total + nb_ref[i * N_EXPERTS + e]

    def wait(k, c):
        block_copy(0, 0, slot).wait()
        return c

    lax.fori_loop(0, total, wait, 0)

    @pl.when(i + 1 < pl.num_programs(0))
    def _():
        fetch_tile(i + 1, 1 - slot)

    route = route_ref[...]
    col = lax.broadcasted_iota(jnp.int32, (route.shape[0], ybuf_ref.shape[1]), 1)
    sel = (jnp.where(col == route[:, 4:5].astype(jnp.int32), route[:, 2:3], 0.0)
           + jnp.where(col == route[:, 5:6].astype(jnp.int32), route[:, 3:4], 0.0))
    f = jnp.dot(sel.astype(ybuf_ref.dtype), ybuf_ref[slot], preferred_element_type=jnp.float32)
    out_ref[...] = _layer_norm(DN_ALPHA * x1_ref[...] + f, g2_ref[...], b2_ref[...])


def moe_combine(blk_start, blk_count, x1, route, y_sorted, g2, b2):
    n = x1.shape[0]
    tm = min(TM_CMB, n)
    buf_rows = -(-(2 * tm + N_EXPERTS * 2 * (CMB_BLK - 1)) // 256) * 256
    row = lambda i, a, nb: (i, 0)
    fixed = lambda i, a, nb: (0, 0)
    grid_spec = pltpu.PrefetchScalarGridSpec(
        num_scalar_prefetch=2, grid=(n // tm,),
        in_specs=[pl.BlockSpec((tm, D_MODEL), row), pl.BlockSpec((tm, LANES), row),
                  pl.BlockSpec(memory_space=pl.ANY),
                  pl.BlockSpec((1, D_MODEL), fixed), pl.BlockSpec((1, D_MODEL), fixed)],
        out_specs=pl.BlockSpec((tm, D_MODEL), row),
        scratch_shapes=[pltpu.VMEM((2, buf_rows, D_MODEL), y_sorted.dtype), pltpu.SemaphoreType.DMA((2,))],
    )
    return pl.pallas_call(
        _moe_combine_kernel, grid_spec=grid_spec,
        out_shape=jax.ShapeDtypeStruct((n, D_MODEL), jnp.float32),
        compiler_params=_cparams(("arbitrary",)), name="moe_combine",
    )(blk_start, blk_count, x1, route, y_sorted, g2, b2)


def _dispatch_plan(expert_idx, tm, tm_tok):
    n = expert_idx.shape[0]
    e_flat = expert_idx.reshape(-1)
    onehot = (e_flat[:, None] == jnp.arange(N_EXPERTS, dtype=jnp.int32)[None, :]).astype(jnp.int32)
    csum = jnp.cumsum(onehot, axis=0)
    before = csum - onehot
    rank = jnp.sum(before * onehot, axis=1)
    counts = csum[-1]
    tiles_per = (counts + (DISP_BLK - 1) + tm - 1) // tm
    tile_end = jnp.cumsum(tiles_per)
    tile_begin = tile_end - tiles_per
    group_start = tile_begin * tm
    pos_flat = jnp.sum(onehot * group_start[None, :], axis=1) + rank
    n_tiles = (2 * n) // tm + N_EXPERTS + 1
    t = jnp.arange(n_tiles, dtype=jnp.int32)
    last_valid = jnp.maximum(tile_end[-1] - 1, 0)
    t_eff = jnp.minimum(t, last_valid)
    tile_expert = jnp.minimum(jnp.sum((t_eff[:, None] >= tile_end[None, :]).astype(jnp.int32), axis=1),
                              N_EXPERTS - 1).astype(jnp.int32)
    of_tile = (tile_expert[:, None] == jnp.arange(N_EXPERTS, dtype=jnp.int32)[None, :]).astype(jnp.int32)
    rows_left = jnp.sum(of_tile * (counts - (t[:, None] - tile_begin[None, :]) * tm), axis=1)
    tile_rows = jnp.where(t < tile_end[-1], jnp.clip(rows_left, 0, tm), 0).astype(jnp.int32)

    first = before[::2 * tm_tok]
    cnt = jnp.concatenate([first[1:], counts[None, :]], axis=0) - first
    start = group_start[None, :] + first
    pair_onehot = onehot.reshape(-1, 2 * tm_tok, N_EXPERTS)

    d_count = (cnt + DISP_BLK - 1) // DISP_BLK
    d_off = (jnp.cumsum(d_count, axis=1) - d_count) * DISP_BLK
    d_local = rank + jnp.sum(pair_onehot * (d_off - first)[:, None, :], axis=2).reshape(-1)
    d_local = d_local.reshape(-1, tm_tok, 2).transpose(0, 2, 1)

    blk_start = (start // CMB_BLK) * CMB_BLK
    blk_count = jnp.where(cnt > 0, (start - blk_start + cnt + CMB_BLK - 1) // CMB_BLK, 0)
    buf_off = (jnp.cumsum(blk_count, axis=1) - blk_count) * CMB_BLK
    local = pos_flat + jnp.sum(pair_onehot * (buf_off - blk_start)[:, None, :], axis=2).reshape(-1)
    flat = lambda v: v.reshape(-1).astype(jnp.int32)
    return dict(n_tiles=n_tiles, tile_expert=tile_expert, tile_rows=tile_rows,
                row_start=flat(start), d_count=flat(d_count), d_local=d_local.astype(jnp.int32),
                blk_start=flat(blk_start), blk_count=flat(blk_count), c_local=local.reshape(n, 2))


def _pad_w_in(w):
    z64 = jnp.zeros((D_MODEL, KR_X1), w.dtype)
    z32 = jnp.zeros((D_MODEL, LANES - KR_X1 - MLA_ROPE), w.dtype)
    return jnp.concatenate([w[:, 0:768], w[:, 1440:1696], w[:, 1696:1952], w[:, 1152:1408],
                            w[:, 768:1152], z64, w[:, 1408:1440], z32], axis=1).astype(MXU_DTYPE)


def _rope_tables(s):
    pos = jnp.arange(s, dtype=jnp.float32)
    inv = ROPE_THETA ** (-jnp.arange(0, MLA_ROPE, 2, dtype=jnp.float32) / MLA_ROPE)
    ang = pos[:, None] * inv[None, :]
    cos, sin = jnp.cos(ang), jnp.sin(ang)
    zl = jnp.zeros((s, KR_X1), jnp.float32)
    zh = jnp.zeros((s, HALF_ROPE), jnp.float32)
    zr = jnp.zeros((s, LANES - KR_X1 - MLA_ROPE), jnp.float32)
    kc = jnp.concatenate([zl, cos, cos, zr], axis=1)
    ksa = jnp.concatenate([zl, -sin, zh, zr], axis=1)
    ksb = jnp.concatenate([zl, zh, sin, zr], axis=1)
    return cos.T, sin.T, kc, ksa, ksb


def _block_diag(w):
    nb, bw, _ = w.shape
    out = jnp.zeros((nb * bw, nb * bw), w.dtype)
    for i in range(nb):
        out = out.at[i * bw:(i + 1) * bw, i * bw:(i + 1) * bw].set(w[i])
    return out


def kernel(x, ln_in_g, ln_in_b, w_in, conv_w, q_norm_g, w_uq, kv_norm_g, w_ukv, lru_conv_w, lru_conv_b,
           lru_wa, lru_ba, lru_wi, lru_bi, lru_lam, mix_norm_g, w_out, ln1_g, ln1_b, dense_w_gate,
           dense_w_up, dense_w_down, moe_w_router, moe_w_gate, moe_w_up, moe_w_down, ln2_g, ln2_b):
    b, s, d = x.shape
    n = b * s
    rope = _rope_tables(s)
    vec = lambda v: v.reshape(1, -1)
    cur = x.reshape(n, d)
    for l in range(DEPTH):
        w_in_pad = _pad_w_in(w_in[l])
        wq_t = jnp.pad(w_uq[l].T.reshape(MLA_HEADS, MLA_NOPE + MLA_ROPE, MLA_Q_RANK),
                       ((0, 0), (0, HEAD_PAD - MLA_NOPE - MLA_ROPE), (0, 0))
                       ).reshape(MLA_HEADS * HEAD_PAD, MLA_Q_RANK).astype(MXU_DTYPE)
        wkv = w_ukv[l].reshape(MLA_KV_RANK, MLA_HEADS, MLA_NOPE + MLA_V)
        wk_pad = jnp.pad(wkv[:, :, :MLA_NOPE], ((0, 0), (0, 0), (0, HEAD_PAD - MLA_NOPE))
                         ).reshape(MLA_KV_RANK, MLA_HEADS * HEAD_PAD).astype(MXU_DTYPE)
        wv_t = wkv[:, :, MLA_NOPE:].reshape(MLA_KV_RANK, MLA_HEADS * MLA_V).T.astype(MXU_DTYPE)
        gate_w = [jnp.concatenate([_block_diag(lru_wa[l, dr]), _block_diag(lru_wi[l, dr])], axis=1
                                  ).astype(MXU_DTYPE) for dr in range(2)]
        gate_b = [jnp.concatenate([lru_ba[l, dr], lru_bi[l, dr]]).reshape(1, -1) for dr in range(2)]

        xn3, z3, q_t, k, v_t = in_proj(
            cur.reshape(b, s, d), w_in_pad, vec(q_norm_g[l]), vec(kv_norm_g[l]), wq_t, wk_pad, wv_t, rope,
            ln=(vec(ln_in_g), vec(ln_in_b)) if l == 0 else None)
        xn, z = xn3.reshape(n, d), z3.reshape(n, D_Z)
        o = attention(q_t, k, v_t).reshape(n, MLA_HEADS * MLA_V)
        h_f, y_conv = lru_scan(z3, lru_conv_w[l], vec(lru_conv_b[l]), gate_w[0], gate_b[0],
                               vec(lru_lam[l, 0]), short_w=conv_w[l])
        h_b = lru_scan(z3, lru_conv_w[l], vec(lru_conv_b[l]), gate_w[1], gate_b[1],
                       vec(lru_lam[l, 1]), reverse=True)
        mixer_args = (y_conv.reshape(n, CONV_DIM), o, h_f.reshape(n, LRU_DIM), h_b.reshape(n, LRU_DIM),
                      z, xn, vec(mix_norm_g[l]), w_out[l].astype(MXU_DTYPE), vec(ln1_g[l]), vec(ln1_b[l]))

        j = l // 2
        if l % 2 == 0:
            x1 = post_mixer(*mixer_args)
            cur = ffn_dense(x1, dense_w_gate[j], dense_w_up[j], dense_w_down[j], vec(ln2_g[l]), vec(ln2_b[l]))
        else:
            wr_hi = moe_w_router[j].astype(jnp.bfloat16).astype(jnp.float32)
            wr_lo = (moe_w_router[j] - wr_hi).astype(jnp.bfloat16).astype(jnp.float32)
            w_router = jnp.pad(jnp.concatenate([wr_hi, wr_lo], axis=1), ((0, 0), (0, LANES - 2 * N_EXPERTS)))
            x1, route = post_mixer(*mixer_args, w_router=w_router)
            expert_idx = route[:, 0:2].astype(jnp.int32)
            tm = min(TM_FFN, n)
            plan = _dispatch_plan(expert_idx, tm, min(TM_CMB, n))
            x_sorted = moe_dispatch(x1, plan["row_start"], plan["d_count"], plan["d_local"],
                                    plan["n_tiles"] * tm)
            y_sorted = moe_ffn(x_sorted, plan["tile_expert"], plan["tile_rows"],
                               moe_w_gate[j], moe_w_up[j], moe_w_down[j])
            route = jnp.concatenate([route[:, :4], plan["c_local"].astype(jnp.float32), route[:, 6:]], axis=1)
            cur = moe_combine(plan["blk_start"], plan["blk_count"], x1, route, y_sorted,
                              vec(ln2_g[l]), vec(ln2_b[l]))
    return cur.reshape(b, s, d)
```
